```python
import jax, jax.numpy as jnp
from jax import lax
import numpy as np

D_MODEL = 2048
BATCH = 4
SEQ = 2048
DEPTH = 1
DEC_BATCH = 128
DEC_SEQ = 8
PAST_LEN = 16384
PAGE_SIZE = 128

LRU_WIDTH = D_MODEL
LRU_HEADS = 16
LRU_BLOCK = LRU_WIDTH // LRU_HEADS
LRU_C = 8.0
CONV_WIDTH = 4
SSD_WIDTH = D_MODEL
SSD_HEADDIM = 64
SSD_HEADS = SSD_WIDTH // SSD_HEADDIM
SSD_GROUPS = 2
SSD_STATE = 128
SSD_CHUNK = 128
SSD_CONV_DIM = SSD_WIDTH + 2 * SSD_GROUPS * SSD_STATE
D_MIX = LRU_WIDTH + SSD_WIDTH
IN_SPLITS = [LRU_WIDTH, LRU_WIDTH, SSD_WIDTH, SSD_CONV_DIM, SSD_HEADS]
D_IN = sum(IN_SPLITS)
D_FF = ((8 * D_MODEL // 3 + 255) // 256) * 256
N_MEM = 256
XATTN_HEADS = 4
XATTN_HEAD_DIM = D_MODEL // XATTN_HEADS
EPS = 1e-6

kernel_name = "hymba_rglru_ssd_macaron_memxattn_step"


def rmsnorm(x, g):
    xf = x.astype(jnp.float32)
    xf = xf * lax.rsqrt(jnp.mean(xf * xf, axis=-1, keepdims=True) + EPS)
    return (xf * g.astype(jnp.float32)).astype(x.dtype)


def group_rmsnorm(x, g, n_groups):
    shp = x.shape
    xf = x.astype(jnp.float32).reshape(shp[:-1] + (n_groups, shp[-1] // n_groups))
    xf = xf * lax.rsqrt(jnp.mean(xf * xf, axis=-1, keepdims=True) + EPS)
    return xf.reshape(shp) * g.astype(jnp.float32)


def swiglu(x, w_gate, w_up, w_down):
    return (jax.nn.silu(x @ w_gate) * (x @ w_up)) @ w_down


def causal_conv(u, buf, w, b):
    ext = jnp.concatenate([buf.astype(u.dtype), u], axis=1)
    T = u.shape[1]
    y = b + ext[:, 0:T] * w[0]
    for k in range(1, CONV_WIDTH):
        y = y + ext[:, k:k + T] * w[k]
    return y, ext[:, -(CONV_WIDTH - 1):]


def linear_recurrence(a, b, h0):
    def combine(l, r):
        return (l[0] * r[0], r[0] * l[1] + r[1])
    a_cum, b_cum = lax.associative_scan(combine, (a, b), axis=1)
    h = a_cum * h0[:, None] + b_cum
    return h, h[:, -1]


def ssd_chunked(x, dt, A, B, C, h0):
    bsz, T, H, P = x.shape
    G, N = B.shape[2], B.shape[3]
    Q = H // G
    L = min(SSD_CHUNK, T)
    pad = (-T) % L
    if pad:
        pw = lambda t: jnp.pad(t, [(0, 0), (0, pad)] + [(0, 0)] * (t.ndim - 2))
        x, dt, B, C = pw(x), pw(dt), pw(B), pw(C)
    nc = (T + pad) // L
    x = x.reshape(bsz, nc, L, G, Q, P)
    dt = dt.reshape(bsz, nc, L, G, Q)
    B = B.reshape(bsz, nc, L, G, N)
    C = C.reshape(bsz, nc, L, G, N)
    cs = jnp.cumsum(dt * A.reshape(G, Q), axis=2)
    cs_t = jnp.moveaxis(cs, 2, -1)
    causal = jnp.tril(jnp.ones((L, L), dtype=bool))
    decay = jnp.exp(jnp.where(causal, cs_t[..., :, None] - cs_t[..., None, :], -jnp.inf))
    cb = jnp.einsum('bclgn,bcsgn->bcgls', C, B)
    dt_t = jnp.moveaxis(dt, 2, -1)
    w_diag = cb[:, :, :, None] * decay * dt_t[..., None, :]
    y_diag = jnp.einsum('bcgqls,bcsgqp->bclgqp', w_diag, x)
    to_end = jnp.exp(cs[:, :, -1:] - cs) * dt
    chunk_states = jnp.einsum('bclgn,bclgq,bclgqp->bcgqpn', B, to_end, x)
    chunk_decay = jnp.exp(cs[:, :, -1])

    def step(h, inp):
        dec, st = inp
        return dec[..., None, None] * h + st, h

    h_last, h_in = lax.scan(step, h0.reshape(bsz, G, Q, P, N),
                            (jnp.moveaxis(chunk_decay, 1, 0), jnp.moveaxis(chunk_states, 1, 0)))
    h_in = jnp.moveaxis(h_in, 0, 1)
    y_off = jnp.einsum('bclgn,bcgqpn,bclgq->bclgqp', C, h_in, jnp.exp(cs))
    y = (y_diag + y_off).reshape(bsz, nc * L, H, P)[:, :T]
    return y, h_last.reshape(bsz, H, P, N)


def mem_kv(mem, g, w_k, w_v):
    m = rmsnorm(mem, g)
    b = m.shape[0]
    k = (m @ w_k).reshape(b, -1, XATTN_HEADS, XATTN_HEAD_DIM)
    v = (m @ w_v).reshape(b, -1, XATTN_HEADS, XATTN_HEAD_DIM)
    return k, v


def cross_attn(h, k, v, w_q, w_o):
    b, T, _ = h.shape
    q = (h @ w_q).reshape(b, T, XATTN_HEADS, XATTN_HEAD_DIM)
    s = jnp.einsum('bthd,bmhd->bhtm', q.astype(jnp.float32), k.astype(jnp.float32)) * (XATTN_HEAD_DIM ** -0.5)
    p = jax.nn.softmax(s, axis=-1)
    o = jnp.einsum('bhtm,bmhd->bthd', p, v.astype(jnp.float32)).reshape(b, T, D_MODEL)
    return o.astype(h.dtype) @ w_o


def layer(x, k_mem, v_mem, lru_conv0, lru_h0, ssd_conv0, ssd_h0, w):
    (ffn1_g, ffn1_wg, ffn1_wu, ffn1_wd, mix_g, w_in,
     lru_conv_w, lru_conv_b, lru_w_a, lru_b_a, lru_w_x, lru_b_x, lru_lambda, lru_out_g,
     ssd_conv_w, ssd_conv_b, ssd_dt_bias, ssd_a_log, ssd_d, ssd_out_g, w_out,
     xattn_g, w_q, w_o, ffn2_g, ffn2_wg, ffn2_wu, ffn2_wd) = w
    f32 = jnp.float32
    x = x + 0.5 * swiglu(rmsnorm(x, ffn1_g), ffn1_wg, ffn1_wu, ffn1_wd)
    h = rmsnorm(x, mix_g)
    proj = h @ w_in
    idx = [int(v) for v in np.cumsum(IN_SPLITS)[:-1]]
    x_lru, g_lru, z, xbc, dt_raw = jnp.split(proj, idx, axis=-1)
    bsz, T = x.shape[0], x.shape[1]
    u, lru_conv1 = causal_conv(x_lru, lru_conv0, lru_conv_w, lru_conv_b)
    uf = u.astype(f32)
    uh = uf.reshape(bsz, T, LRU_HEADS, LRU_BLOCK)
    r = jax.nn.sigmoid(jnp.einsum('bthi,hij->bthj', uh, lru_w_a.astype(f32)).reshape(bsz, T, LRU_WIDTH) + lru_b_a)
    i = jax.nn.sigmoid(jnp.einsum('bthi,hij->bthj', uh, lru_w_x.astype(f32)).reshape(bsz, T, LRU_WIDTH) + lru_b_x)
    log_a = -LRU_C * r * jax.nn.softplus(-lru_lambda.astype(f32))
    a = jnp.exp(log_a)
    beta = jnp.sqrt(-jnp.expm1(2.0 * log_a))
    hs, lru_h1 = linear_recurrence(a, beta * i * uf, lru_h0.astype(f32))
    y_lru = rmsnorm(hs * jax.nn.gelu(g_lru.astype(f32)), lru_out_g)
    xbc_c, ssd_conv1 = causal_conv(xbc, ssd_conv0, ssd_conv_w, ssd_conv_b)
    xbc_c = jax.nn.silu(xbc_c.astype(f32))
    xs, Bm, Cm = jnp.split(xbc_c, [SSD_WIDTH, SSD_WIDTH + SSD_GROUPS * SSD_STATE], axis=-1)
    dt = jax.nn.softplus(dt_raw.astype(f32) + ssd_dt_bias)
    A = -jnp.exp(ssd_a_log.astype(f32))
    xh = xs.reshape(bsz, T, SSD_HEADS, SSD_HEADDIM)
    y, ssd_h1 = ssd_chunked(xh, dt, A,
                            Bm.reshape(bsz, T, SSD_GROUPS, SSD_STATE),
                            Cm.reshape(bsz, T, SSD_GROUPS, SSD_STATE),
                            ssd_h0.astype(f32))
    y = (y + ssd_d.astype(f32)[:, None] * xh).reshape(bsz, T, SSD_WIDTH)
    y_ssd = group_rmsnorm(y * jax.nn.silu(z.astype(f32)), ssd_out_g, SSD_GROUPS)
    mixed = jnp.concatenate([y_lru, y_ssd], axis=-1).astype(x.dtype) @ w_out
    x = x + mixed
    x = x + cross_attn(rmsnorm(x, xattn_g), k_mem, v_mem, w_q, w_o)
    x = x + 0.5 * swiglu(rmsnorm(x, ffn2_g), ffn2_wg, ffn2_wu, ffn2_wd)
    return x, (lru_conv1, lru_h1, ssd_conv1, ssd_h1)


def setup_inputs(seed: int = 0) -> dict:
    key = jax.random.key(seed)
    ks = iter(jax.random.split(key, 64))
    f32 = jnp.float32

    def normal(shape, scale):
        return scale * jax.random.normal(next(ks), shape, f32)

    def gain(shape):
        return 1.0 + normal(shape, 0.01)

    Lz = DEPTH
    a_init = jax.random.uniform(next(ks), (Lz, LRU_WIDTH), f32, 0.9, 0.999)
    s = a_init ** (1.0 / LRU_C)
    lru_lambda = jnp.log(s) - jnp.log1p(-s)
    dt_init = jnp.exp(jax.random.uniform(next(ks), (Lz, SSD_HEADS), f32, np.log(1e-3), np.log(1e-1)))
    ssd_dt_bias = dt_init + jnp.log(-jnp.expm1(-dt_init))
    ssd_a_log = jnp.log(jax.random.uniform(next(ks), (Lz, SSD_HEADS), f32, 1.0, 16.0))
    return {
        'x_prompt': normal((BATCH, SEQ, D_MODEL), 1.0),
        'mem_prompt': normal((BATCH, N_MEM, D_MODEL), 1.0),
        'x_sample': normal((DEC_BATCH, DEC_SEQ, D_MODEL), 1.0),
        'cache_mem_k': normal((Lz, DEC_BATCH, N_MEM, XATTN_HEADS, XATTN_HEAD_DIM), 1.0),
        'cache_mem_v': normal((Lz, DEC_BATCH, N_MEM, XATTN_HEADS, XATTN_HEAD_DIM), 1.0),
        'state_lru_conv': normal((Lz, DEC_BATCH, CONV_WIDTH - 1, LRU_WIDTH), 1.0),
        'state_lru_h': normal((Lz, DEC_BATCH, LRU_WIDTH), 0.5),
        'state_ssd_conv': normal((Lz, DEC_BATCH, CONV_WIDTH - 1, SSD_CONV_DIM), 1.0),
        'state_ssd_h': normal((Lz, DEC_BATCH, SSD_HEADS, SSD_HEADDIM, SSD_STATE), 0.1),
        'ffn1_norm_g': gain((Lz, D_MODEL)),
        'ffn1_w_gate': normal((Lz, D_MODEL, D_FF), D_MODEL ** -0.5),
        'ffn1_w_up': normal((Lz, D_MODEL, D_FF), D_MODEL ** -0.5),
        'ffn1_w_down': normal((Lz, D_FF, D_MODEL), D_FF ** -0.5),
        'mix_norm_g': gain((Lz, D_MODEL)),
        'w_in': normal((Lz, D_MODEL, D_IN), D_MODEL ** -0.5),
        'lru_conv_w': normal((Lz, CONV_WIDTH, LRU_WIDTH), CONV_WIDTH ** -0.5),
        'lru_conv_b': normal((Lz, LRU_WIDTH), 0.01),
        'lru_w_a': normal((Lz, LRU_HEADS, LRU_BLOCK, LRU_BLOCK), LRU_BLOCK ** -0.5),
        'lru_b_a': normal((Lz, LRU_WIDTH), 0.01),
        'lru_w_x': normal((Lz, LRU_HEADS, LRU_BLOCK, LRU_BLOCK), LRU_BLOCK ** -0.5),
        'lru_b_x': normal((Lz, LRU_WIDTH), 0.01),
        'lru_lambda': lru_lambda,
        'lru_out_norm_g': gain((Lz, LRU_WIDTH)),
        'ssd_conv_w': normal((Lz, CONV_WIDTH, SSD_CONV_DIM), CONV_WIDTH ** -0.5),
        'ssd_conv_b': normal((Lz, SSD_CONV_DIM), 0.01),
        'ssd_dt_bias': ssd_dt_bias,
        'ssd_a_log': ssd_a_log,
        'ssd_d': gain((Lz, SSD_HEADS)),
        'ssd_out_norm_g': gain((Lz, SSD_WIDTH)),
        'w_out': normal((Lz, D_MIX, D_MODEL), D_MIX ** -0.5),
        'xattn_norm_g': gain((Lz, D_MODEL)),
        'mem_norm_g': gain((Lz, D_MODEL)),
        'xattn_w_q': normal((Lz, D_MODEL, D_MODEL), D_MODEL ** -0.5),
        'xattn_w_k': normal((Lz, D_MODEL, D_MODEL), D_MODEL ** -0.5),
        'xattn_w_v': normal((Lz, D_MODEL, D_MODEL), D_MODEL ** -0.5),
        'xattn_w_o': normal((Lz, D_MODEL, D_MODEL), D_MODEL ** -0.5),
        'ffn2_norm_g': gain((Lz, D_MODEL)),
        'ffn2_w_gate': normal((Lz, D_MODEL, D_FF), D_MODEL ** -0.5),
        'ffn2_w_up': normal((Lz, D_MODEL, D_FF), D_MODEL ** -0.5),
        'ffn2_w_down': normal((Lz, D_FF, D_MODEL), D_FF ** -0.5),
        'final_norm_g': gain((D_MODEL,)),
    }


def reference(x_prompt, mem_prompt, x_sample, cache_mem_k, cache_mem_v,
              state_lru_conv, state_lru_h, state_ssd_conv, state_ssd_h,
              ffn1_norm_g, ffn1_w_gate, ffn1_w_up, ffn1_w_down, mix_norm_g, w_in,
              lru_conv_w, lru_conv_b, lru_w_a, lru_b_a, lru_w_x, lru_b_x, lru_lambda, lru_out_norm_g,
              ssd_conv_w, ssd_conv_b, ssd_dt_bias, ssd_a_log, ssd_d, ssd_out_norm_g, w_out,
              xattn_norm_g, mem_norm_g, xattn_w_q, xattn_w_k, xattn_w_v, xattn_w_o,
              ffn2_norm_g, ffn2_w_gate, ffn2_w_up, ffn2_w_down, final_norm_g):
    layer_weights = (ffn1_norm_g, ffn1_w_gate, ffn1_w_up, ffn1_w_down, mix_norm_g, w_in,
                     lru_conv_w, lru_conv_b, lru_w_a, lru_b_a, lru_w_x, lru_b_x, lru_lambda, lru_out_norm_g,
                     ssd_conv_w, ssd_conv_b, ssd_dt_bias, ssd_a_log, ssd_d, ssd_out_norm_g, w_out,
                     xattn_norm_g, xattn_w_q, xattn_w_o, ffn2_norm_g, ffn2_w_gate, ffn2_w_up, ffn2_w_down)
    bp = x_prompt.shape[0]
    xp, xs = x_prompt, x_sample
    p_lc, p_lh, p_sc, p_sh, p_mk, p_mv = [], [], [], [], [], []
    s_lc, s_lh, s_sc, s_sh = [], [], [], []
    for l in range(DEPTH):
        w = tuple(t[l] for t in layer_weights)
        mk, mv = mem_kv(mem_prompt, mem_norm_g[l], xattn_w_k[l], xattn_w_v[l])
        xp, (lc, lh, sc, sh) = layer(
            xp, mk, mv,
            jnp.zeros((bp, CONV_WIDTH - 1, LRU_WIDTH), xp.dtype),
            jnp.zeros((bp, LRU_WIDTH), jnp.float32),
            jnp.zeros((bp, CONV_WIDTH - 1, SSD_CONV_DIM), xp.dtype),
            jnp.zeros((bp, SSD_HEADS, SSD_HEADDIM, SSD_STATE), jnp.float32),
            w)
        p_lc.append(lc); p_lh.append(lh); p_sc.append(sc); p_sh.append(sh); p_mk.append(mk); p_mv.append(mv)
        xs, (lc, lh, sc, sh) = layer(
            xs, cache_mem_k[l], cache_mem_v[l],
            state_lru_conv[l], state_lru_h[l], state_ssd_conv[l], state_ssd_h[l], w)
        s_lc.append(lc); s_lh.append(lh); s_sc.append(sc); s_sh.append(sh)
    y_prompt = rmsnorm(xp, final_norm_g)
    y_sample = rmsnorm(xs, final_norm_g)
    return (y_prompt, y_sample,
            jnp.stack(p_lc), jnp.stack(p_lh), jnp.stack(p_sc), jnp.stack(p_sh), jnp.stack(p_mk), jnp.stack(p_mv),
            jnp.stack(s_lc), jnp.stack(s_lh), jnp.stack(s_sc), jnp.stack(s_sh))
```

```python
import functools

import jax
import jax.numpy as jnp
from jax import lax
from jax.experimental import pallas as pl
from jax.experimental.pallas import tpu as pltpu

F32 = jnp.float32
BF16 = jnp.bfloat16

D_MODEL = 2048
LRU_WIDTH = 2048
LRU_HEADS = 16
LRU_C = 8.0
CONV_WIDTH = 4
SSD_WIDTH = 2048
SSD_HEADDIM = 64
SSD_HEADS = 32
SSD_GROUPS = 2
SSD_STATE = 128
SSD_CONV_DIM = SSD_WIDTH + 2 * SSD_GROUPS * SSD_STATE
D_IN = 3 * 2048 + SSD_CONV_DIM + SSD_HEADS
N_MEM = 256
XATTN_HEADS = 4
XATTN_HEAD_DIM = 512
EPS = 1e-6

LANES = 128
SUBLANES = 8
VMEM_LIMIT_BYTES = 56 * 1024 * 1024

D_IN_PAD = 9216
COL_XLRU = 0
COL_GLRU = 2048
COL_Z = 4096
COL_XBC = 6144
COL_DT = 8704
ROWS = 128
GROUP_COLS = SSD_WIDTH // SSD_GROUPS


def _cparams(n_axes):
    return pltpu.CompilerParams(
        dimension_semantics=("arbitrary",) * n_axes,
        vmem_limit_bytes=VMEM_LIMIT_BYTES)


def _rmsnorm_rows(x, g):
    ms = jnp.mean(x * x, axis=-1, keepdims=True)
    return (x * lax.rsqrt(ms + EPS)) * g


def _softplus(x):
    return jnp.maximum(x, 0.0) + jnp.log1p(jnp.exp(-jnp.abs(x)))


def _silu(x):
    return x * jax.nn.sigmoid(x)


def _ffn_kernel(final_norm, x_ref, g_ref, wg_ref, wu_ref, wd_ref, gf_ref, o_ref, h_scr):
    j = pl.program_id(1)

    @pl.when(j == 0)
    def _():
        x = x_ref[...]
        h_scr[...] = _rmsnorm_rows(x, g_ref[...]).astype(BF16)
        o_ref[...] = x

    h = h_scr[...]
    gate = jnp.dot(h, wg_ref[...], preferred_element_type=F32)
    up = jnp.dot(h, wu_ref[...], preferred_element_type=F32)
    a = (0.5 * _silu(gate) * up).astype(BF16)
    o_ref[...] += jnp.dot(a, wd_ref[...], preferred_element_type=F32)

    if final_norm:
        @pl.when(j == pl.num_programs(1) - 1)
        def _():
            o_ref[...] = _rmsnorm_rows(o_ref[...], gf_ref[...])


def _ffn(x, g, wg, wu, wd, gf, *, final_norm, tm=512, tf=512):
    m, d = x.shape
    f = wg.shape[1]
    return pl.pallas_call(
        functools.partial(_ffn_kernel, final_norm),
        grid=(m // tm, f // tf),
        in_specs=[
            pl.BlockSpec((tm, d), lambda i, j: (i, 0)),
            pl.BlockSpec((1, d), lambda i, j: (0, 0)),
            pl.BlockSpec((d, tf), lambda i, j: (0, j)),
            pl.BlockSpec((d, tf), lambda i, j: (0, j)),
            pl.BlockSpec((tf, d), lambda i, j: (j, 0)),
            pl.BlockSpec((1, d), lambda i, j: (0, 0)),
        ],
        out_specs=pl.BlockSpec((tm, d), lambda i, j: (i, 0)),
        out_shape=jax.ShapeDtypeStruct((m, d), F32),
        scratch_shapes=[pltpu.VMEM((tm, d), BF16)],
        compiler_params=_cparams(2),
        name="ffn_final" if final_norm else "ffn",
    )(x, g, wg, wu, wd, gf)


def _norm_matmul_kernel(x_ref, g_ref, w_ref, o_ref, h_scr):
    @pl.when(pl.program_id(1) == 0)
    def _():
        h_scr[...] = _rmsnorm_rows(x_ref[...], g_ref[...]).astype(BF16)

    o_ref[...] = jnp.dot(h_scr[...], w_ref[...], preferred_element_type=F32).astype(o_ref.dtype)


def _norm_matmul(x, g, w, *, tm, tn, out_dtype, name):
    m, k = x.shape
    n = w.shape[1]
    return pl.pallas_call(
        _norm_matmul_kernel,
        grid=(m // tm, n // tn),
        in_specs=[
            pl.BlockSpec((tm, k), lambda i, j: (i, 0)),
            pl.BlockSpec((1, k), lambda i, j: (0, 0)),
            pl.BlockSpec((k, tn), lambda i, j: (0, j)),
        ],
        out_specs=pl.BlockSpec((tm, tn), lambda i, j: (i, j)),
        out_shape=jax.ShapeDtypeStruct((m, n), out_dtype),
        scratch_shapes=[pltpu.VMEM((tm, k), BF16)],
        compiler_params=_cparams(2),
        name=name,
    )(x, g, w)


def _matmul_residual_kernel(a_ref, w_ref, r_ref, o_ref):
    o_ref[...] = r_ref[...] + jnp.dot(a_ref[...], w_ref[...], preferred_element_type=F32)


def _matmul_residual(a, w, res, *, tm, tn, name):
    m, k = a.shape
    n = w.shape[1]
    return pl.pallas_call(
        _matmul_residual_kernel,
        grid=(m // tm, n // tn),
        in_specs=[
            pl.BlockSpec((tm, k), lambda i, j: (i, 0)),
            pl.BlockSpec((k, tn), lambda i, j: (0, j)),
            pl.BlockSpec((tm, tn), lambda i, j: (i, j)),
        ],
        out_specs=pl.BlockSpec((tm, tn), lambda i, j: (i, j)),
        out_shape=jax.ShapeDtypeStruct((m, n), F32),
        compiler_params=_cparams(2),
        name=name,
    )(a, w, res)


def _xattn_kernel(n_seq, tq, q_ref, k_ref, v_ref, o_ref):
    scale = XATTN_HEAD_DIM ** -0.5
    for s in range(n_seq):
        rows = slice(s * tq, (s + 1) * tq)
        for h in range(XATTN_HEADS):
            cols = slice(h * XATTN_HEAD_DIM, (h + 1) * XATTN_HEAD_DIM)
            q = q_ref[rows, cols]
            k = k_ref[s, :, cols].astype(BF16)
            v = v_ref[s, :, cols].astype(BF16)
            sc = lax.dot_general(q, k, (((1,), (1,)), ((), ())),
                                 preferred_element_type=F32) * scale
            mx = jnp.max(sc, axis=-1, keepdims=True)
            e = jnp.exp(sc - mx)
            p = e / jnp.sum(e, axis=-1, keepdims=True)
            o = jnp.dot(p.astype(BF16), v, preferred_element_type=F32)
            o_ref[rows, cols] = o.astype(BF16)


def _xattn_prompt(q, k, v, *, seq_len, tq):
    b = k.shape[0]
    nt = seq_len // tq
    d = q.shape[1]
    return pl.pallas_call(
        functools.partial(_xattn_kernel, 1, tq),
        grid=(b, nt),
        in_specs=[
            pl.BlockSpec((tq, d), lambda i, j: (i * nt + j, 0)),
            pl.BlockSpec((1, N_MEM, d), lambda i, j: (i, 0, 0)),
            pl.BlockSpec((1, N_MEM, d), lambda i, j: (i, 0, 0)),
        ],
        out_specs=pl.BlockSpec((tq, d), lambda i, j: (i * nt + j, 0)),
        out_shape=jax.ShapeDtypeStruct((b * seq_len, d), BF16),
        compiler_params=_cparams(2),
        name="xattn_prompt",
    )(q, k, v)


def _xattn_sample(q, k, v, *, seq_len, n_seq, row_block_offset):
    b = k.shape[0]
    d = q.shape[1]
    rows = n_seq * seq_len
    return pl.pallas_call(
        functools.partial(_xattn_kernel, n_seq, seq_len),
        grid=(b // n_seq,),
        in_specs=[
            pl.BlockSpec((rows, d), lambda i: (row_block_offset + i, 0)),
            pl.BlockSpec((n_seq, N_MEM, d), lambda i: (i, 0, 0)),
            pl.BlockSpec((n_seq, N_MEM, d), lambda i: (i, 0, 0)),
        ],
        out_specs=pl.BlockSpec((rows, d), lambda i: (i, 0)),
        out_shape=jax.ShapeDtypeStruct((b * seq_len, d), BF16),
        compiler_params=_cparams(1),
        name="xattn_sample",
    )(q, k, v)


def _row_iota(shape):
    return lax.broadcasted_iota(jnp.int32, shape, 0)


def _lane_iota(shape):
    return lax.broadcasted_iota(jnp.int32, shape, 1)


def _seg_cumsum(x, seg_len):
    pos = _row_iota(x.shape) % seg_len
    s = 1
    while s < seg_len:
        x = x + jnp.where(pos >= s, pltpu.roll(x, s, axis=0), 0.0)
        s *= 2
    return x


def _seg_rev_excl_cumsum(x, seg_len):
    n = x.shape[0]
    pos = _row_iota(x.shape) % seg_len
    y = jnp.where(pos < seg_len - 1, pltpu.roll(x, n - 1, axis=0), 0.0)
    s = 1
    while s < seg_len:
        y = y + jnp.where(pos < seg_len - s, pltpu.roll(y, n - s, axis=0), 0.0)
        s *= 2
    return y


def _scan8(a, b):
    pos = _row_iota(a.shape) % SUBLANES
    for s in (1, 2, 4):
        m = pos >= s
        a_sh = pltpu.roll(a, s, axis=0)
        b_sh = pltpu.roll(b, s, axis=0)
        b = jnp.where(m, a * b_sh + b, b)
        a = jnp.where(m, a * a_sh, a)
    return a, b


def _conv_taps(x, w_ref, b_ref, cols, fix):
    acc = b_ref[:, cols] + x * w_ref[CONV_WIDTH - 1:CONV_WIDTH, cols]
    for s in range(1, CONV_WIDTH):
        xs = fix(s, pltpu.roll(x, s, axis=0))
        k = CONV_WIDTH - 1 - s
        acc = acc + xs * w_ref[k:k + 1, cols]
    return acc


def _bcast_col(v, c):
    return jnp.broadcast_to(v[:, c:c + 1], (v.shape[0], LANES))


def _mixer_front(sample, proj_ref, prev_fix_lru, prev_fix_ssd, h0_rows, hcar_ref,
                 lcw_ref, lcb_ref, wa_ref, wx_ref, ba_ref, bx_ref, lam_ref, lg_ref,
                 scw_ref, scb_ref, dtb_ref, alog_ref, dexp_ref,
                 mixed_ref, hs_scr, yl_scr, xc_scr, xw_scr, ecs_scr, y_scr, xt_scr, dec_scr, c_scr, b_scr):
    seg = SUBLANES if sample else ROWS
    n_seq = ROWS // seg

    ssq = jnp.zeros((ROWS, 1), F32)
    for hd in range(LRU_HEADS):
        cols = slice(hd * LANES, (hd + 1) * LANES)
        xl = proj_ref[:, COL_XLRU + hd * LANES:COL_XLRU + (hd + 1) * LANES]
        u = _conv_taps(xl, lcw_ref, lcb_ref, cols, functools.partial(prev_fix_lru, cols))
        ub = u.astype(BF16)
        r = jax.nn.sigmoid(jnp.dot(ub, wa_ref[hd], preferred_element_type=F32) + ba_ref[:, cols])
        i = jax.nn.sigmoid(jnp.dot(ub, wx_ref[hd], preferred_element_type=F32) + bx_ref[:, cols])
        log_a = (-LRU_C * r) * _softplus(-lam_ref[:, cols])
        a = jnp.exp(log_a)
        th = jnp.tanh(log_a)
        beta = jnp.sqrt((-2.0 * th) / (1.0 - th))
        bb = (beta * i) * u
        a_cum, b_cum = _scan8(a, bb)
        if sample:
            h = a_cum * h0_rows(cols) + b_cum
        else:
            carry = hcar_ref[:, cols]
            parts = []
            for t in range(ROWS // SUBLANES):
                rs = slice(t * SUBLANES, (t + 1) * SUBLANES)
                ht = a_cum[rs] * carry + b_cum[rs]
                parts.append(ht)
                carry = ht[SUBLANES - 1:SUBLANES]
            h = jnp.concatenate(parts, axis=0)
            hcar_ref[:, cols] = carry
        hs_scr[:, cols] = h
        g = jax.nn.gelu(proj_ref[:, COL_GLRU + hd * LANES:COL_GLRU + (hd + 1) * LANES])
        yl = h * g
        ssq = ssq + jnp.sum(yl * yl, axis=-1, keepdims=True)
        yl_scr[:, cols] = yl
    inv = lax.rsqrt(ssq * (1.0 / LRU_WIDTH) + EPS)
    for hd in range(LRU_HEADS):
        cols = slice(hd * LANES, (hd + 1) * LANES)
        mixed_ref[:, cols] = ((yl_scr[:, cols] * inv) * lg_ref[:, cols]).astype(BF16)

    for t in range(SSD_CONV_DIM // LANES):
        cols = slice(t * LANES, (t + 1) * LANES)
        xb = proj_ref[:, COL_XBC + t * LANES:COL_XBC + (t + 1) * LANES]
        v = _conv_taps(xb, scw_ref, scb_ref, cols, functools.partial(prev_fix_ssd, cols))
        xc_scr[:, cols] = _silu(v)

    dt = _softplus(proj_ref[:, COL_DT:COL_DT + LANES] + dtb_ref[...])
    a_neg = -jnp.exp(alog_ref[...])
    d_a = dt * a_neg
    cs = _seg_cumsum(d_a, seg)
    rcs = _seg_rev_excl_cumsum(d_a, seg)
    ecs = jnp.exp(cs)
    todt = jnp.exp(rcs) * dt
    dec_tot = jnp.exp(cs + rcs)
    cs_t = cs.T
    dt_t = dt.T
    dec_t = dec_tot.T
    for s in range(n_seq):
        dec_scr[s] = jnp.broadcast_to(dec_t[0:SSD_HEADS, s * seg:s * seg + 1], (SSD_HEADS, LANES))

    ri = _row_iota((ROWS, ROWS))
    ci = _lane_iota((ROWS, ROWS))
    causal = ri >= ci
    if sample:
        causal = jnp.logical_and(causal, (ri // seg) == (ci // seg))
    lo = _lane_iota((ROWS, LANES)) < SSD_HEADDIM

    b_off = SSD_WIDTH
    c_off = SSD_WIDTH + SSD_GROUPS * SSD_STATE
    for g in range(SSD_GROUPS):
        gcols = slice(g * SSD_STATE, (g + 1) * SSD_STATE)
        bg = xc_scr[:, b_off + g * SSD_STATE:b_off + (g + 1) * SSD_STATE].astype(BF16)
        cg = xc_scr[:, c_off + g * SSD_STATE:c_off + (g + 1) * SSD_STATE].astype(BF16)
        c_scr[:, gcols] = xc_scr[:, c_off + g * SSD_STATE:c_off + (g + 1) * SSD_STATE]
        b_scr[:, gcols] = bg
        cb = lax.dot_general(cg, bg, (((1,), (1,)), ((), ())), preferred_element_type=F32)
        heads_per_group = SSD_HEADS // SSD_GROUPS
        for jp in range(heads_per_group // 2):
            h0 = g * heads_per_group + 2 * jp
            h1 = h0 + 1
            cols = slice(h0 * SSD_HEADDIM, h0 * SSD_HEADDIM + LANES)
            x_pair = xc_scr[:, cols]
            ws = []
            for hh in (h0, h1):
                diff = _bcast_col(cs, hh) - cs_t[hh:hh + 1, :]
                dec = jnp.where(causal, jnp.exp(diff), 0.0)
                ws.append(((cb * dec) * dt_t[hh:hh + 1, :]).astype(BF16))
            lhs = jnp.concatenate(ws, axis=1)
            rhs = jnp.concatenate([jnp.where(lo, x_pair, 0.0), jnp.where(lo, 0.0, x_pair)],
                                  axis=0).astype(BF16)
            y_diag = jnp.dot(lhs, rhs, preferred_element_type=F32)
            y_scr[:, cols] = y_diag + x_pair * dexp_ref[:, cols]
            xw_scr[:, cols] = x_pair * jnp.where(lo, _bcast_col(todt, h0), _bcast_col(todt, h1))
            ecs_scr[:, cols] = jnp.where(lo, _bcast_col(ecs, h0), _bcast_col(ecs, h1))
    for t in range(SSD_WIDTH // LANES):
        xt_scr[t * LANES:(t + 1) * LANES, :] = xw_scr[:, t * LANES:(t + 1) * LANES].T.astype(BF16)


def _ssd_state_step(c_rows, b_mask, s_old, dec, xt_scr):
    y_parts = []
    s_parts = []
    heads_per_group = SSD_HEADS // SSD_GROUPS
    for g in range(SSD_GROUPS):
        gcols = slice(g * SSD_STATE, (g + 1) * SSD_STATE)
        grows = slice(g * GROUP_COLS, (g + 1) * GROUP_COLS)
        sg = s_old[grows]
        y_parts.append(lax.dot_general(c_rows[:, gcols], sg.astype(BF16), (((1,), (1,)), ((), ())),
                                       preferred_element_type=F32))
        upd = jnp.dot(xt_scr[grows, :], b_mask[:, gcols], preferred_element_type=F32)
        for hh in range(heads_per_group):
            hr = slice(hh * SSD_HEADDIM, (hh + 1) * SSD_HEADDIM)
            habs = g * heads_per_group + hh
            s_parts.append(sg[hr] * dec[habs:habs + 1, :] + upd[hr])
    return jnp.concatenate(y_parts, axis=1), jnp.concatenate(s_parts, axis=0)


def _ssd_finish(proj_ref, y_scr, sg_ref, mixed_ref):
    for g in range(SSD_GROUPS):
        ssq = jnp.zeros((ROWS, 1), F32)
        n_t = GROUP_COLS // LANES
        for t in range(n_t):
            c0 = g * GROUP_COLS + t * LANES
            yz = y_scr[:, c0:c0 + LANES] * _silu(proj_ref[:, COL_Z + c0:COL_Z + c0 + LANES])
            y_scr[:, c0:c0 + LANES] = yz
            ssq = ssq + jnp.sum(yz * yz, axis=-1, keepdims=True)
        inv = lax.rsqrt(ssq * (1.0 / GROUP_COLS) + EPS)
        for t in range(n_t):
            c0 = g * GROUP_COLS + t * LANES
            mixed_ref[:, LRU_WIDTH + c0:LRU_WIDTH + c0 + LANES] = (
                (y_scr[:, c0:c0 + LANES] * inv) * sg_ref[:, c0:c0 + LANES]).astype(BF16)


def _mixer_prompt_kernel(proj_ref,
                         lcw_ref, lcb_ref, wa_ref, wx_ref, ba_ref, bx_ref, lam_ref, lg_ref,
                         scw_ref, scb_ref, dtb_ref, alog_ref, dexp_ref, sg_ref,
                         mixed_ref, lconv_ref, lh_ref, sconv_ref, sh_ref,
                         hcar_ref, ltail_scr, stail_scr, s_scr,
                         hs_scr, yl_scr, xc_scr, xw_scr, ecs_scr, y_scr, xt_scr, dec_scr, c_scr, b_scr):
    c = pl.program_id(1)

    @pl.when(c == 0)
    def _():
        hcar_ref[...] = jnp.zeros_like(hcar_ref)
        ltail_scr[...] = jnp.zeros_like(ltail_scr)
        stail_scr[...] = jnp.zeros_like(stail_scr)
        s_scr[...] = jnp.zeros_like(s_scr)

    pos8 = _row_iota((SUBLANES, LANES))

    def make_fix(tail_scr):
        def fix(cols, s, rolled):
            first = jnp.where(pos8 < s, pltpu.roll(tail_scr[:, cols], s, axis=0), rolled[0:SUBLANES])
            return jnp.concatenate([first, rolled[SUBLANES:]], axis=0)
        return fix

    _mixer_front(False, proj_ref, make_fix(ltail_scr), make_fix(stail_scr), None, hcar_ref,
                 lcw_ref, lcb_ref, wa_ref, wx_ref, ba_ref, bx_ref, lam_ref, lg_ref,
                 scw_ref, scb_ref, dtb_ref, alog_ref, dexp_ref,
                 mixed_ref, hs_scr, yl_scr, xc_scr, xw_scr, ecs_scr, y_scr, xt_scr, dec_scr, c_scr, b_scr)

    ltail_scr[...] = proj_ref[ROWS - SUBLANES:ROWS, COL_XLRU:COL_XLRU + LRU_WIDTH]
    stail_scr[...] = proj_ref[ROWS - SUBLANES:ROWS, COL_XBC:COL_XBC + SSD_CONV_DIM]

    y_off, s_new = _ssd_state_step(c_scr[...].astype(BF16), b_scr[...], s_scr[...], dec_scr[0], xt_scr)
    s_scr[...] = s_new
    for t in range(SSD_WIDTH // LANES):
        cols = slice(t * LANES, (t + 1) * LANES)
        y_scr[:, cols] = y_scr[:, cols] + ecs_scr[:, cols] * y_off[:, cols]
    _ssd_finish(proj_ref, y_scr, sg_ref, mixed_ref)

    @pl.when(c == pl.num_programs(1) - 1)
    def _():
        lconv_ref[0] = proj_ref[ROWS - (CONV_WIDTH - 1):ROWS, COL_XLRU:COL_XLRU + LRU_WIDTH]
        sconv_ref[0] = proj_ref[ROWS - (CONV_WIDTH - 1):ROWS, COL_XBC:COL_XBC + SSD_CONV_DIM]
        lh_ref[0] = hcar_ref[...]
        sh_ref[0] = s_scr[...]


def _mixer_sample_kernel(n_inner, proj_ref, lprev_ref, sprev_ref, h0_ref, sin_ref,
                         lcw_ref, lcb_ref, wa_ref, wx_ref, ba_ref, bx_ref, lam_ref, lg_ref,
                         scw_ref, scb_ref, dtb_ref, alog_ref, dexp_ref, sg_ref,
                         mixed_ref, lconv_ref, lh_ref, sconv_ref, sh_ref,
                         hs_scr, yl_scr, xc_scr, xw_scr, ecs_scr, y_scr, xt_scr, dec_scr, c_scr, b_scr):
    i = pl.program_id(1)
    seg = SUBLANES
    n_seq = ROWS // seg
    seq_per_step = n_seq // n_inner

    @pl.when(i == 0)
    def _():
        pos = _row_iota((ROWS, LANES)) % seg

        def make_fix(prev_ref):
            def fix(cols, s, rolled):
                return jnp.where(pos < s, pltpu.roll(prev_ref[:, cols], ROWS - seg + s, axis=0), rolled)
            return fix

        def h0_rows(cols):
            return jnp.concatenate(
                [jnp.broadcast_to(h0_ref[q:q + 1, cols], (seg, LANES)) for q in range(n_seq)], axis=0)

        _mixer_front(True, proj_ref, make_fix(lprev_ref), make_fix(sprev_ref), h0_rows, None,
                     lcw_ref, lcb_ref, wa_ref, wx_ref, ba_ref, bx_ref, lam_ref, lg_ref,
                     scw_ref, scb_ref, dtb_ref, alog_ref, dexp_ref,
                     mixed_ref, hs_scr, yl_scr, xc_scr, xw_scr, ecs_scr, y_scr, xt_scr, dec_scr, c_scr, b_scr)
        for sq in range(n_seq):
            tail = slice((sq + 1) * seg - (CONV_WIDTH - 1), (sq + 1) * seg)
            lconv_ref[sq] = proj_ref[tail, COL_XLRU:COL_XLRU + LRU_WIDTH]
            sconv_ref[sq] = proj_ref[tail, COL_XBC:COL_XBC + SSD_CONV_DIM]
            lh_ref[sq:sq + 1, :] = hs_scr[(sq + 1) * seg - 1:(sq + 1) * seg, :]

    rgroup = _row_iota((ROWS, 2 * SSD_STATE)) // seg
    for jj in range(seq_per_step):
        q = i * seq_per_step + jj
        r0 = pl.multiple_of(q * seg, seg)
        c_rows = c_scr[pl.ds(r0, seg), :].astype(BF16)
        b_all = b_scr[...]
        b_mask = jnp.where(rgroup == q, b_all, jnp.zeros_like(b_all))
        y_off, s_new = _ssd_state_step(c_rows, b_mask, sin_ref[jj], dec_scr[q], xt_scr)
        sh_ref[jj] = s_new
        y_scr[pl.ds(r0, seg), :] = y_scr[pl.ds(r0, seg), :] + ecs_scr[pl.ds(r0, seg), :] * y_off

    @pl.when(i == n_inner - 1)
    def _():
        _ssd_finish(proj_ref, y_scr, sg_ref, mixed_ref)


def _mixer_scratch(n_dec):
    return [
        pltpu.VMEM((ROWS, LRU_WIDTH), F32),
        pltpu.VMEM((ROWS, LRU_WIDTH), F32),
        pltpu.VMEM((ROWS, SSD_CONV_DIM), F32),
        pltpu.VMEM((ROWS, SSD_WIDTH), F32),
        pltpu.VMEM((ROWS, SSD_WIDTH), F32),
        pltpu.VMEM((ROWS, SSD_WIDTH), F32),
        pltpu.VMEM((SSD_WIDTH, ROWS), BF16),
        pltpu.VMEM((n_dec, SSD_HEADS, LANES), F32),
        pltpu.VMEM((ROWS, 2 * SSD_STATE), F32),
        pltpu.VMEM((ROWS, 2 * SSD_STATE), BF16),
    ]


def _full_spec(shape, n_grid):
    zeros = (0,) * len(shape)
    if n_grid == 2:
        return pl.BlockSpec(shape, lambda i, j: zeros)
    return pl.BlockSpec(shape, lambda i: zeros)


def _mixer_weight_specs(n_grid):
    return [
        _full_spec((CONV_WIDTH, LRU_WIDTH), n_grid),
        _full_spec((1, LRU_WIDTH), n_grid),
        _full_spec((LRU_HEADS, LANES, LANES), n_grid),
        _full_spec((LRU_HEADS, LANES, LANES), n_grid),
        _full_spec((1, LRU_WIDTH), n_grid),
        _full_spec((1, LRU_WIDTH), n_grid),
        _full_spec((1, LRU_WIDTH), n_grid),
        _full_spec((1, LRU_WIDTH), n_grid),
        _full_spec((CONV_WIDTH, SSD_CONV_DIM), n_grid),
        _full_spec((1, SSD_CONV_DIM), n_grid),
        _full_spec((1, LANES), n_grid),
        _full_spec((1, LANES), n_grid),
        _full_spec((1, SSD_WIDTH), n_grid),
        _full_spec((1, SSD_WIDTH), n_grid),
    ]


def _mixer_prompt(proj, weights, *, batch, seq_len):
    nc = seq_len // ROWS
    return pl.pallas_call(
        _mixer_prompt_kernel,
        grid=(batch, nc),
        in_specs=[pl.BlockSpec((ROWS, D_IN_PAD), lambda b, c: (b * nc + c, 0))] + _mixer_weight_specs(2),
        out_specs=[
            pl.BlockSpec((ROWS, LRU_WIDTH + SSD_WIDTH), lambda b, c: (b * nc + c, 0)),
            pl.BlockSpec((1, CONV_WIDTH - 1, LRU_WIDTH), lambda b, c: (b, 0, 0)),
            pl.BlockSpec((1, 1, LRU_WIDTH), lambda b, c: (b, 0, 0)),
            pl.BlockSpec((1, CONV_WIDTH - 1, SSD_CONV_DIM), lambda b, c: (b, 0, 0)),
            pl.BlockSpec((1, SSD_WIDTH, SSD_STATE), lambda b, c: (b, 0, 0)),
        ],
        out_shape=[
            jax.ShapeDtypeStruct((batch * seq_len, LRU_WIDTH + SSD_WIDTH), BF16),
            jax.ShapeDtypeStruct((batch, CONV_WIDTH - 1, LRU_WIDTH), F32),
            jax.ShapeDtypeStruct((batch, 1, LRU_WIDTH), F32),
            jax.ShapeDtypeStruct((batch, CONV_WIDTH - 1, SSD_CONV_DIM), F32),
            jax.ShapeDtypeStruct((batch, SSD_WIDTH, SSD_STATE), F32),
        ],
        scratch_shapes=[
            pltpu.VMEM((1, LRU_WIDTH), F32),
            pltpu.VMEM((SUBLANES, LRU_WIDTH), F32),
            pltpu.VMEM((SUBLANES, SSD_CONV_DIM), F32),
            pltpu.VMEM((SSD_WIDTH, SSD_STATE), F32),
        ] + _mixer_scratch(1),
        compiler_params=_cparams(2),
        name="mixer_prompt",
    )(proj, *weights)


def _mixer_sample(proj, lprev, sprev, h0, s_in, weights, *, batch, seq_len, row_block_offset, n_inner=4):
    n_seq = ROWS // seq_len
    n_outer = batch // n_seq
    sps = n_seq // n_inner
    return pl.pallas_call(
        functools.partial(_mixer_sample_kernel, n_inner),
        grid=(n_outer, n_inner),
        in_specs=[
            pl.BlockSpec((ROWS, D_IN_PAD), lambda o, i: (row_block_offset + o, 0)),
            pl.BlockSpec((ROWS, LRU_WIDTH), lambda o, i: (o, 0)),
            pl.BlockSpec((ROWS, SSD_CONV_DIM), lambda o, i: (o, 0)),
            pl.BlockSpec((n_seq, LRU_WIDTH), lambda o, i: (o, 0)),
            pl.BlockSpec((sps, SSD_WIDTH, SSD_STATE), lambda o, i: (o * n_inner + i, 0, 0)),
        ] + _mixer_weight_specs(2),
        out_specs=[
            pl.BlockSpec((ROWS, LRU_WIDTH + SSD_WIDTH), lambda o, i: (o, 0)),
            pl.BlockSpec((n_seq, CONV_WIDTH - 1, LRU_WIDTH), lambda o, i: (o, 0, 0)),
            pl.BlockSpec((n_seq, LRU_WIDTH), lambda o, i: (o, 0)),
            pl.BlockSpec((n_seq, CONV_WIDTH - 1, SSD_CONV_DIM), lambda o, i: (o, 0, 0)),
            pl.BlockSpec((sps, SSD_WIDTH, SSD_STATE), lambda o, i: (o * n_inner + i, 0, 0)),
        ],
        out_shape=[
            jax.ShapeDtypeStruct((batch * seq_len, LRU_WIDTH + SSD_WIDTH), BF16),
            jax.ShapeDtypeStruct((batch, CONV_WIDTH - 1, LRU_WIDTH), F32),
            jax.ShapeDtypeStruct((batch, LRU_WIDTH), F32),
            jax.ShapeDtypeStruct((batch, CONV_WIDTH - 1, SSD_CONV_DIM), F32),
            jax.ShapeDtypeStruct((batch, SSD_WIDTH, SSD_STATE), F32),
        ],
        scratch_shapes=_mixer_scratch(n_seq),
        compiler_params=_cparams(2),
        name="mixer_sample",
    )(proj, lprev, sprev, h0, s_in, *weights)


def _row(v):
    return v.reshape(1, -1).astype(F32)


def _pad_lanes(v):
    v = v.reshape(1, -1).astype(F32)
    return jnp.pad(v, ((0, 0), (0, LANES - v.shape[1])))


def kernel(x_prompt, mem_prompt, x_sample, cache_mem_k, cache_mem_v, state_lru_conv, state_lru_h, state_ssd_conv, state_ssd_h, ffn1_norm_g, ffn1_w_gate, ffn1_w_up, ffn1_w_down, mix_norm_g, w_in, lru_conv_w, lru_conv_b, lru_w_a, lru_b_a, lru_w_x, lru_b_x, lru_lambda, lru_out_norm_g, ssd_conv_w, ssd_conv_b, ssd_dt_bias, ssd_a_log, ssd_d, ssd_out_norm_g, w_out, xattn_norm_g, mem_norm_g, xattn_w_q, xattn_w_k, xattn_w_v, xattn_w_o, ffn2_norm_g, ffn2_w_gate, ffn2_w_up, ffn2_w_down, final_norm_g):
    depth = ffn1_norm_g.shape[0]
    assert depth == 1
    bp, tp, d = x_prompt.shape
    bs, ts, _ = x_sample.shape
    mp = bp * tp
    ms = bs * ts
    assert tp % ROWS == 0 and ROWS % ts == 0 and ts == SUBLANES and mp % ROWS == 0
    l = 0

    x = jnp.concatenate([x_prompt.reshape(mp, d), x_sample.reshape(ms, d)], axis=0)

    w_in_p = jnp.pad(w_in[l], ((0, 0), (0, D_IN_PAD - D_IN))).astype(BF16)
    mixer_weights = (
        lru_conv_w[l], _row(lru_conv_b[l]), lru_w_a[l].astype(BF16), lru_w_x[l].astype(BF16),
        _row(lru_b_a[l]), _row(lru_b_x[l]), _row(lru_lambda[l]), _row(lru_out_norm_g[l]),
        ssd_conv_w[l], _row(ssd_conv_b[l]), _pad_lanes(ssd_dt_bias[l]), _pad_lanes(ssd_a_log[l]),
        _row(jnp.repeat(ssd_d[l], SSD_HEADDIM)), _row(ssd_out_norm_g[l]),
    )

    x1 = _ffn(x, _row(ffn1_norm_g[l]), ffn1_w_gate[l].astype(BF16), ffn1_w_up[l].astype(BF16),
              ffn1_w_down[l].astype(BF16), _row(final_norm_g), final_norm=False)
    proj = _norm_matmul(x1, _row(mix_norm_g[l]), w_in_p, tm=1024, tn=1024, out_dtype=F32, name="in_proj")

    mixed_p, p_lc, p_lh, p_sc, p_sh = _mixer_prompt(proj, mixer_weights, batch=bp, seq_len=tp)

    pad_rows = ((0, 0), (SUBLANES - (CONV_WIDTH - 1), 0), (0, 0))
    lprev = jnp.pad(state_lru_conv[l], pad_rows).reshape(ms, LRU_WIDTH)
    sprev = jnp.pad(state_ssd_conv[l], pad_rows).reshape(ms, SSD_CONV_DIM)
    mixed_s, s_lc, s_lh, s_sc, s_sh = _mixer_sample(
        proj, lprev, sprev, state_lru_h[l], state_ssd_h[l].reshape(bs, SSD_WIDTH, SSD_STATE),
        mixer_weights, batch=bs, seq_len=ts, row_block_offset=mp // ROWS)

    mixed = jnp.concatenate([mixed_p, mixed_s], axis=0)
    x2 = _matmul_residual(mixed, w_out[l].astype(BF16), x1, tm=1024, tn=512, name="out_proj")

    q = _norm_matmul(x2, _row(xattn_norm_g[l]), xattn_w_q[l].astype(BF16), tm=1024, tn=1024,
                     out_dtype=BF16, name="q_proj")
    mem = mem_prompt.reshape(bp * N_MEM, d)
    mk = _norm_matmul(mem, _row(mem_norm_g[l]), xattn_w_k[l].astype(BF16), tm=1024, tn=1024,
                      out_dtype=F32, name="mem_k")
    mv = _norm_matmul(mem, _row(mem_norm_g[l]), xattn_w_v[l].astype(BF16), tm=1024, tn=1024,
                      out_dtype=F32, name="mem_v")
    o_p = _xattn_prompt(q, mk.reshape(bp, N_MEM, d), mv.reshape(bp, N_MEM, d), seq_len=tp, tq=512)
    n_seq_x = 2
    o_s = _xattn_sample(q, cache_mem_k[l].reshape(bs, N_MEM, d), cache_mem_v[l].reshape(bs, N_MEM, d),
                        seq_len=ts, n_seq=n_seq_x, row_block_offset=mp // (n_seq_x * ts))
    o = jnp.concatenate([o_p, o_s], axis=0)
    x3 = _matmul_residual(o, xattn_w_o[l].astype(BF16), x2, tm=1024, tn=1024, name="o_proj")

    y = _ffn(x3, _row(ffn2_norm_g[l]), ffn2_w_gate[l].astype(BF16), ffn2_w_up[l].astype(BF16),
             ffn2_w_down[l].astype(BF16), _row(final_norm_g), final_norm=True)

    y_prompt = y[:mp].reshape(bp, tp, d)
    y_sample = y[mp:].reshape(bs, ts, d)
    hshape = (SSD_HEADS, SSD_HEADDIM, SSD_STATE)
    return (y_prompt, y_sample,
            p_lc[None], p_lh.reshape(1, bp, LRU_WIDTH), p_sc[None], p_sh.reshape((1, bp) + hshape),
            mk.reshape(1, bp, N_MEM, XATTN_HEADS, XATTN_HEAD_DIM),
            mv.reshape(1, bp, N_MEM, XATTN_HEADS, XATTN_HEAD_DIM),
            s_lc[None], s_lh[None], s_sc[None], s_sh.reshape((1, bs) + hshape))
```

```python
import functools

import jax
import jax.numpy as jnp
from jax import lax
from jax.experimental import pallas as pl
from jax.experimental.pallas import tpu as pltpu

F32 = jnp.float32
BF16 = jnp.bfloat16

D_MODEL = 2048
LRU_WIDTH = 2048
LRU_HEADS = 16
LRU_C = 8.0
CONV_WIDTH = 4
SSD_WIDTH = 2048
SSD_HEADDIM = 64
SSD_HEADS = 32
SSD_GROUPS = 2
SSD_STATE = 128
SSD_CONV_DIM = SSD_WIDTH + 2 * SSD_GROUPS * SSD_STATE
D_IN = 3 * 2048 + SSD_CONV_DIM + SSD_HEADS
N_MEM = 256
XATTN_HEADS = 4
XATTN_HEAD_DIM = 512
EPS = 1e-6

LANES = 128
SUBLANES = 8
VMEM_LIMIT_BYTES = 56 * 1024 * 1024

D_IN_PAD = 9216
COL_XLRU = 0
COL_GLRU = 2048
COL_Z = 4096
COL_XBC = 6144
COL_DT = 8704
ROWS = 128
GROUP_COLS = SSD_WIDTH // SSD_GROUPS


def _cparams(n_axes):
    return pltpu.CompilerParams(
        dimension_semantics=("arbitrary",) * n_axes,
        vmem_limit_bytes=VMEM_LIMIT_BYTES)


def _rmsnorm_rows(x, g):
    ms = jnp.mean(x * x, axis=-1, keepdims=True)
    return (x * lax.rsqrt(ms + EPS)) * g


def _softplus(x):
    return jnp.maximum(x, 0.0) + jnp.log1p(jnp.exp(-jnp.abs(x)))


def _silu(x):
    return x * jax.nn.sigmoid(x)


def _when_rows(i, n_a, fn_a, fn_b, extra=None):
    in_a = i < n_a
    in_b = i >= n_a
    if extra is not None:
        in_a = jnp.logical_and(in_a, extra)
        in_b = jnp.logical_and(in_b, extra)
    pl.when(in_a)(fn_a)
    pl.when(in_b)(fn_b)


def _ffn_kernel(n_a, split_in, split_out, final_norm, *refs):
    refs = list(refs)
    xa_ref = refs.pop(0)
    xb_ref = refs.pop(0) if split_in else xa_ref
    g_ref, wg_ref, wu_ref, wd_ref, gf_ref = refs[:5]
    oa_ref = refs[5]
    ob_ref = refs[6] if split_out else oa_ref
    h_scr, acc_scr = refs[-2:]
    i = pl.program_id(0)
    j = pl.program_id(1)

    def prologue(x_ref):
        def run():
            x = x_ref[...]
            h_scr[...] = _rmsnorm_rows(x, g_ref[...]).astype(BF16)
            acc_scr[...] = x
        return run

    if split_in:
        _when_rows(i, n_a, prologue(xa_ref), prologue(xb_ref), extra=(j == 0))
    else:
        pl.when(j == 0)(prologue(xa_ref))

    h = h_scr[...]
    gate = jnp.dot(h, wg_ref[...], preferred_element_type=F32)
    up = jnp.dot(h, wu_ref[...], preferred_element_type=F32)
    a = (0.5 * _silu(gate) * up).astype(BF16)
    acc_scr[...] += jnp.dot(a, wd_ref[...], preferred_element_type=F32)

    def epilogue(o_ref):
        def run():
            res = acc_scr[...]
            if final_norm:
                res = _rmsnorm_rows(res, gf_ref[...])
            o_ref[...] = res
        return run

    last = j == pl.num_programs(1) - 1
    if split_out:
        _when_rows(i, n_a, epilogue(oa_ref), epilogue(ob_ref), extra=last)
    else:
        pl.when(last)(epilogue(oa_ref))


def _ffn(xs, g, wg, wu, wd, gf, *, out_rows, final_norm, tm=512, tf=512):
    split_in = len(xs) == 2
    split_out = len(out_rows) == 2
    d = xs[0].shape[1]
    f = wg.shape[1]
    m = sum(x.shape[0] for x in xs)
    assert m == sum(out_rows) and all(x.shape[0] % tm == 0 for x in xs) and all(r % tm == 0 for r in out_rows)
    n_a = (xs[0].shape[0] if split_in else out_rows[0]) // tm
    if split_in and split_out:
        assert xs[0].shape[0] == out_rows[0]

    def first(i, j):
        return (jnp.minimum(i, n_a - 1), 0)

    def second(i, j):
        return (jnp.maximum(i - n_a, 0), 0)

    def whole(i, j):
        return (i, 0)

    x_specs = ([pl.BlockSpec((tm, d), first), pl.BlockSpec((tm, d), second)] if split_in
               else [pl.BlockSpec((tm, d), whole)])
    o_specs = ([pl.BlockSpec((tm, d), first), pl.BlockSpec((tm, d), second)] if split_out
               else [pl.BlockSpec((tm, d), whole)])
    return pl.pallas_call(
        functools.partial(_ffn_kernel, n_a, split_in, split_out, final_norm),
        grid=(m // tm, f // tf),
        in_specs=x_specs + [
            pl.BlockSpec((1, d), lambda i, j: (0, 0)),
            pl.BlockSpec((d, tf), lambda i, j: (0, j)),
            pl.BlockSpec((d, tf), lambda i, j: (0, j)),
            pl.BlockSpec((tf, d), lambda i, j: (j, 0)),
            pl.BlockSpec((1, d), lambda i, j: (0, 0)),
        ],
        out_specs=o_specs,
        out_shape=[jax.ShapeDtypeStruct((r, d), F32) for r in out_rows],
        scratch_shapes=[pltpu.VMEM((tm, d), BF16), pltpu.VMEM((tm, d), F32)],
        compiler_params=_cparams(2),
        name="ffn_final" if final_norm else "ffn",
    )(*xs, g, wg, wu, wd, gf)


def _norm_matmul_kernel(x_ref, g_ref, w_ref, o_ref, h_scr):
    @pl.when(pl.program_id(1) == 0)
    def _():
        h_scr[...] = _rmsnorm_rows(x_ref[...], g_ref[...]).astype(BF16)

    o_ref[...] = jnp.dot(h_scr[...], w_ref[...], preferred_element_type=F32).astype(o_ref.dtype)


def _norm_matmul(x, g, w, *, tm, tn, out_dtype, name):
    m, k = x.shape
    n = w.shape[1]
    return pl.pallas_call(
        _norm_matmul_kernel,
        grid=(m // tm, n // tn),
        in_specs=[
            pl.BlockSpec((tm, k), lambda i, j: (i, 0)),
            pl.BlockSpec((1, k), lambda i, j: (0, 0)),
            pl.BlockSpec((k, tn), lambda i, j: (0, j)),
        ],
        out_specs=pl.BlockSpec((tm, tn), lambda i, j: (i, j)),
        out_shape=jax.ShapeDtypeStruct((m, n), out_dtype),
        scratch_shapes=[pltpu.VMEM((tm, k), BF16)],
        compiler_params=_cparams(2),
        name=name,
    )(x, g, w)


def _matmul_residual_kernel(n_a, aa_ref, ab_ref, w_ref, r_ref, o_ref):
    def run(a_ref):
        def body():
            o_ref[...] = r_ref[...] + jnp.dot(a_ref[...], w_ref[...], preferred_element_type=F32)
        return body

    _when_rows(pl.program_id(0), n_a, run(aa_ref), run(ab_ref))


def _matmul_residual(a_first, a_second, w, res, *, tm, tn, name):
    k = a_first.shape[1]
    m = a_first.shape[0] + a_second.shape[0]
    n = w.shape[1]
    assert a_first.shape[0] % tm == 0 and a_second.shape[0] % tm == 0 and res.shape == (m, n)
    n_a = a_first.shape[0] // tm
    return pl.pallas_call(
        functools.partial(_matmul_residual_kernel, n_a),
        grid=(m // tm, n // tn),
        in_specs=[
            pl.BlockSpec((tm, k), lambda i, j: (jnp.minimum(i, n_a - 1), 0)),
            pl.BlockSpec((tm, k), lambda i, j: (jnp.maximum(i - n_a, 0), 0)),
            pl.BlockSpec((k, tn), lambda i, j: (0, j)),
            pl.BlockSpec((tm, tn), lambda i, j: (i, j)),
        ],
        out_specs=pl.BlockSpec((tm, tn), lambda i, j: (i, j)),
        out_shape=jax.ShapeDtypeStruct((m, n), F32),
        compiler_params=_cparams(2),
        name=name,
    )(a_first, a_second, w, res)


def _xattn_kernel(n_seq, tq, q_ref, k_ref, v_ref, o_ref):
    scale = XATTN_HEAD_DIM ** -0.5
    for s in range(n_seq):
        rows = slice(s * tq, (s + 1) * tq)
        for h in range(XATTN_HEADS):
            cols = slice(h * XATTN_HEAD_DIM, (h + 1) * XATTN_HEAD_DIM)
            q = q_ref[rows, cols]
            k = k_ref[s, :, cols].astype(BF16)
            v = v_ref[s, :, cols].astype(BF16)
            sc = lax.dot_general(q, k, (((1,), (1,)), ((), ())),
                                 preferred_element_type=F32) * scale
            mx = jnp.max(sc, axis=-1, keepdims=True)
            e = jnp.exp(sc - mx)
            p = e / jnp.sum(e, axis=-1, keepdims=True)
            o = jnp.dot(p.astype(BF16), v, preferred_element_type=F32)
            o_ref[rows, cols] = o.astype(BF16)


def _xattn_prompt(q, k, v, *, seq_len, tq):
    b = k.shape[0]
    nt = seq_len // tq
    d = q.shape[1]
    return pl.pallas_call(
        functools.partial(_xattn_kernel, 1, tq),
        grid=(b, nt),
        in_specs=[
            pl.BlockSpec((tq, d), lambda i, j: (i * nt + j, 0)),
            pl.BlockSpec((1, N_MEM, d), lambda i, j: (i, 0, 0)),
            pl.BlockSpec((1, N_MEM, d), lambda i, j: (i, 0, 0)),
        ],
        out_specs=pl.BlockSpec((tq, d), lambda i, j: (i * nt + j, 0)),
        out_shape=jax.ShapeDtypeStruct((b * seq_len, d), BF16),
        compiler_params=_cparams(2),
        name="xattn_prompt",
    )(q, k, v)


def _xattn_cache_kernel(n_seq, tq, q_ref, k_ref, v_ref, o_ref):
    scale = XATTN_HEAD_DIM ** -0.5
    n_rows = XATTN_HEADS * tq
    n_cols = N_MEM * XATTN_HEADS
    own = (_lane_iota((n_rows, n_cols)) % XATTN_HEADS) == (_row_iota((n_rows, n_cols)) // tq)
    qf = q_ref[...].astype(F32)
    outs = []
    for s in range(n_seq):
        qs = qf[s * tq:(s + 1) * tq]
        q4 = jnp.concatenate(
            [qs[:, h * XATTN_HEAD_DIM:(h + 1) * XATTN_HEAD_DIM] for h in range(XATTN_HEADS)],
            axis=0).astype(BF16)
        kf = k_ref[s].reshape(n_cols, XATTN_HEAD_DIM).astype(BF16)
        vf = v_ref[s].reshape(n_cols, XATTN_HEAD_DIM).astype(BF16)
        sc = lax.dot_general(q4, kf, (((1,), (1,)), ((), ())), preferred_element_type=F32) * scale
        sc = jnp.where(own, sc, -1e30)
        mx = jnp.max(sc, axis=-1, keepdims=True)
        e = jnp.exp(sc - mx)
        p = e / jnp.sum(e, axis=-1, keepdims=True)
        o4 = jnp.dot(p.astype(BF16), vf, preferred_element_type=F32)
        outs.append(jnp.concatenate([o4[h * tq:(h + 1) * tq] for h in range(XATTN_HEADS)], axis=1))
    o_ref[...] = jnp.concatenate(outs, axis=0).astype(BF16)


def _xattn_sample(q, k, v, *, seq_len, n_seq, row_block_offset):
    b = k.shape[0]
    d = q.shape[1]
    rows = n_seq * seq_len
    kv_block = (n_seq, N_MEM, XATTN_HEADS, XATTN_HEAD_DIM)
    return pl.pallas_call(
        functools.partial(_xattn_cache_kernel, n_seq, seq_len),
        grid=(b // n_seq,),
        in_specs=[
            pl.BlockSpec((rows, d), lambda i: (row_block_offset + i, 0)),
            pl.BlockSpec(kv_block, lambda i: (i, 0, 0, 0)),
            pl.BlockSpec(kv_block, lambda i: (i, 0, 0, 0)),
        ],
        out_specs=pl.BlockSpec((rows, d), lambda i: (i, 0)),
        out_shape=jax.ShapeDtypeStruct((b * seq_len, d), BF16),
        compiler_params=_cparams(1),
        name="xattn_sample",
    )(q, k, v)


def _row_iota(shape):
    return lax.broadcasted_iota(jnp.int32, shape, 0)


def _lane_iota(shape):
    return lax.broadcasted_iota(jnp.int32, shape, 1)


def _seg_cumsum(x, seg_len):
    pos = _row_iota(x.shape) % seg_len
    s = 1
    while s < seg_len:
        x = x + jnp.where(pos >= s, pltpu.roll(x, s, axis=0), 0.0)
        s *= 2
    return x


def _seg_rev_excl_cumsum(x, seg_len):
    n = x.shape[0]
    pos = _row_iota(x.shape) % seg_len
    y = jnp.where(pos < seg_len - 1, pltpu.roll(x, n - 1, axis=0), 0.0)
    s = 1
    while s < seg_len:
        y = y + jnp.where(pos < seg_len - s, pltpu.roll(y, n - s, axis=0), 0.0)
        s *= 2
    return y


def _scan8(a, b):
    shape = a.shape
    tiled = (shape[0] // SUBLANES, SUBLANES, shape[1])
    a = a.reshape(tiled)
    b = b.reshape(tiled)
    pos = lax.broadcasted_iota(jnp.int32, tiled, 1)
    for s in (1, 2, 4):
        m = pos >= s
        a_sh = pltpu.roll(a, s, axis=1)
        b_sh = pltpu.roll(b, s, axis=1)
        b = jnp.where(m, a * b_sh + b, b)
        a = jnp.where(m, a * a_sh, a)
    return a.reshape(shape), b.reshape(shape)


def _conv_taps(x, w_ref, b_ref, cols, fix):
    acc = b_ref[:, cols] + x * w_ref[CONV_WIDTH - 1:CONV_WIDTH, cols]
    for s in range(1, CONV_WIDTH):
        xs = fix(s, pltpu.roll(x, s, axis=0))
        k = CONV_WIDTH - 1 - s
        acc = acc + xs * w_ref[k:k + 1, cols]
    return acc


def _bcast_col(v, c):
    return jnp.broadcast_to(v[:, c:c + 1], (v.shape[0], LANES))


def _mixer_front(sample, proj_ref, prev_fix_lru, prev_fix_ssd, h0_rows, hcar_ref,
                 lcw_ref, lcb_ref, wa_ref, wx_ref, ba_ref, bx_ref, lam_ref, lg_ref,
                 scw_ref, scb_ref, dtb_ref, alog_ref, dexp_ref,
                 mixed_ref, hs_scr, yl_scr, xc_scr, xw_scr, ecs_scr, y_scr, xt_scr, dec_scr, c_scr, b_scr):
    seg = SUBLANES if sample else ROWS
    n_seq = ROWS // seg

    ssq = jnp.zeros((ROWS, 1), F32)
    for hd in range(LRU_HEADS):
        cols = slice(hd * LANES, (hd + 1) * LANES)
        xl = proj_ref[:, COL_XLRU + hd * LANES:COL_XLRU + (hd + 1) * LANES]
        u = _conv_taps(xl, lcw_ref, lcb_ref, cols, functools.partial(prev_fix_lru, cols))
        ub = u.astype(BF16)
        r = jax.nn.sigmoid(jnp.dot(ub, wa_ref[hd], preferred_element_type=F32) + ba_ref[:, cols])
        i = jax.nn.sigmoid(jnp.dot(ub, wx_ref[hd], preferred_element_type=F32) + bx_ref[:, cols])
        log_a = (-LRU_C * r) * _softplus(-lam_ref[:, cols])
        a = jnp.exp(log_a)
        th = jnp.tanh(log_a)
        beta = jnp.sqrt((-2.0 * th) / (1.0 - th))
        bb = (beta * i) * u
        a_cum, b_cum = _scan8(a, bb)
        if sample:
            h = a_cum * h0_rows(cols) + b_cum
        else:
            carry = hcar_ref[:, cols]
            parts = []
            for t in range(ROWS // SUBLANES):
                rs = slice(t * SUBLANES, (t + 1) * SUBLANES)
                ht = a_cum[rs] * carry + b_cum[rs]
                parts.append(ht)
                carry = ht[SUBLANES - 1:SUBLANES]
            h = jnp.concatenate(parts, axis=0)
            hcar_ref[:, cols] = carry
        hs_scr[:, cols] = h
        g = jax.nn.gelu(proj_ref[:, COL_GLRU + hd * LANES:COL_GLRU + (hd + 1) * LANES])
        yl = h * g
        ssq = ssq + jnp.sum(yl * yl, axis=-1, keepdims=True)
        yl_scr[:, cols] = yl
    inv = lax.rsqrt(ssq * (1.0 / LRU_WIDTH) + EPS)
    for hd in range(LRU_HEADS):
        cols = slice(hd * LANES, (hd + 1) * LANES)
        mixed_ref[:, cols] = ((yl_scr[:, cols] * inv) * lg_ref[:, cols]).astype(BF16)

    for t in range(SSD_CONV_DIM // LANES):
        cols = slice(t * LANES, (t + 1) * LANES)
        xb = proj_ref[:, COL_XBC + t * LANES:COL_XBC + (t + 1) * LANES]
        v = _conv_taps(xb, scw_ref, scb_ref, cols, functools.partial(prev_fix_ssd, cols))
        xc_scr[:, cols] = _silu(v)

    dt = _softplus(proj_ref[:, COL_DT:COL_DT + LANES] + dtb_ref[...])
    a_neg = -jnp.exp(alog_ref[...])
    d_a = dt * a_neg
    cs = _seg_cumsum(d_a, seg)
    rcs = _seg_rev_excl_cumsum(d_a, seg)
    ecs = jnp.exp(cs)
    todt = jnp.exp(rcs) * dt
    dec_tot = jnp.exp(cs + rcs)
    cs_t = cs.T
    dt_t = dt.T
    dec_t = dec_tot.T
    for s in range(n_seq):
        dec_scr[s] = jnp.broadcast_to(dec_t[0:SSD_HEADS, s * seg:s * seg + 1], (SSD_HEADS, LANES))

    ri = _row_iota((ROWS, ROWS))
    ci = _lane_iota((ROWS, ROWS))
    causal = ri >= ci
    if sample:
        causal = jnp.logical_and(causal, (ri // seg) == (ci // seg))
    lo = _lane_iota((ROWS, LANES)) < SSD_HEADDIM

    b_off = SSD_WIDTH
    c_off = SSD_WIDTH + SSD_GROUPS * SSD_STATE
    for g in range(SSD_GROUPS):
        gcols = slice(g * SSD_STATE, (g + 1) * SSD_STATE)
        bg = xc_scr[:, b_off + g * SSD_STATE:b_off + (g + 1) * SSD_STATE].astype(BF16)
        cg = xc_scr[:, c_off + g * SSD_STATE:c_off + (g + 1) * SSD_STATE].astype(BF16)
        c_scr[:, gcols] = xc_scr[:, c_off + g * SSD_STATE:c_off + (g + 1) * SSD_STATE]
        b_scr[:, gcols] = bg
        cb = lax.dot_general(cg, bg, (((1,), (1,)), ((), ())), preferred_element_type=F32)
        heads_per_group = SSD_HEADS // SSD_GROUPS
        for jp in range(heads_per_group // 2):
            h0 = g * heads_per_group + 2 * jp
            h1 = h0 + 1
            cols = slice(h0 * SSD_HEADDIM, h0 * SSD_HEADDIM + LANES)
            x_pair = xc_scr[:, cols]
            ws = []
            for hh in (h0, h1):
                diff = _bcast_col(cs, hh) - cs_t[hh:hh + 1, :]
                dec = jnp.where(causal, jnp.exp(diff), 0.0)
                ws.append(((cb * dec) * dt_t[hh:hh + 1, :]).astype(BF16))
            lhs = jnp.concatenate(ws, axis=1)
            rhs = jnp.concatenate([jnp.where(lo, x_pair, 0.0), jnp.where(lo, 0.0, x_pair)],
                                  axis=0).astype(BF16)
            y_diag = jnp.dot(lhs, rhs, preferred_element_type=F32)
            y_scr[:, cols] = y_diag + x_pair * dexp_ref[:, cols]
            xw_scr[:, cols] = x_pair * jnp.where(lo, _bcast_col(todt, h0), _bcast_col(todt, h1))
            ecs_scr[:, cols] = jnp.where(lo, _bcast_col(ecs, h0), _bcast_col(ecs, h1))
    for t in range(SSD_WIDTH // LANES):
        xt_scr[t * LANES:(t + 1) * LANES, :] = xw_scr[:, t * LANES:(t + 1) * LANES].T.astype(BF16)


def _ssd_state_step(c_rows, b_mask, s_old, dec, xt_scr):
    y_parts = []
    s_parts = []
    heads_per_group = SSD_HEADS // SSD_GROUPS
    for g in range(SSD_GROUPS):
        gcols = slice(g * SSD_STATE, (g + 1) * SSD_STATE)
        grows = slice(g * GROUP_COLS, (g + 1) * GROUP_COLS)
        sg = s_old[grows]
        y_parts.append(lax.dot_general(c_rows[:, gcols], sg.astype(BF16), (((1,), (1,)), ((), ())),
                                       preferred_element_type=F32))
        upd = jnp.dot(xt_scr[grows, :], b_mask[:, gcols], preferred_element_type=F32)
        for hh in range(heads_per_group):
            hr = slice(hh * SSD_HEADDIM, (hh + 1) * SSD_HEADDIM)
            habs = g * heads_per_group + hh
            s_parts.append(sg[hr] * dec[habs:habs + 1, :] + upd[hr])
    return jnp.concatenate(y_parts, axis=1), jnp.concatenate(s_parts, axis=0)


def _ssd_finish(proj_ref, y_scr, sg_ref, mixed_ref):
    for g in range(SSD_GROUPS):
        ssq = jnp.zeros((ROWS, 1), F32)
        n_t = GROUP_COLS // LANES
        for t in range(n_t):
            c0 = g * GROUP_COLS + t * LANES
            yz = y_scr[:, c0:c0 + LANES] * _silu(proj_ref[:, COL_Z + c0:COL_Z + c0 + LANES])
            y_scr[:, c0:c0 + LANES] = yz
            ssq = ssq + jnp.sum(yz * yz, axis=-1, keepdims=True)
        inv = lax.rsqrt(ssq * (1.0 / GROUP_COLS) + EPS)
        for t in range(n_t):
            c0 = g * GROUP_COLS + t * LANES
            mixed_ref[:, LRU_WIDTH + c0:LRU_WIDTH + c0 + LANES] = (
                (y_scr[:, c0:c0 + LANES] * inv) * sg_ref[:, c0:c0 + LANES]).astype(BF16)


def _mixer_prompt_kernel(proj_ref,
                         lcw_ref, lcb_ref, wa_ref, wx_ref, ba_ref, bx_ref, lam_ref, lg_ref,
                         scw_ref, scb_ref, dtb_ref, alog_ref, dexp_ref, sg_ref,
                         mixed_ref, lconv_ref, lh_ref, sconv_ref, sh_ref,
                         hcar_ref, ltail_scr, stail_scr, s_scr,
                         hs_scr, yl_scr, xc_scr, xw_scr, ecs_scr, y_scr, xt_scr, dec_scr, c_scr, b_scr):
    c = pl.program_id(1)

    @pl.when(c == 0)
    def _():
        hcar_ref[...] = jnp.zeros_like(hcar_ref)
        ltail_scr[...] = jnp.zeros_like(ltail_scr)
        stail_scr[...] = jnp.zeros_like(stail_scr)
        s_scr[...] = jnp.zeros_like(s_scr)

    pos8 = _row_iota((SUBLANES, LANES))

    def make_fix(tail_scr):
        def fix(cols, s, rolled):
            first = jnp.where(pos8 < s, pltpu.roll(tail_scr[:, cols], s, axis=0), rolled[0:SUBLANES])
            return jnp.concatenate([first, rolled[SUBLANES:]], axis=0)
        return fix

    _mixer_front(False, proj_ref, make_fix(ltail_scr), make_fix(stail_scr), None, hcar_ref,
                 lcw_ref, lcb_ref, wa_ref, wx_ref, ba_ref, bx_ref, lam_ref, lg_ref,
                 scw_ref, scb_ref, dtb_ref, alog_ref, dexp_ref,
                 mixed_ref, hs_scr, yl_scr, xc_scr, xw_scr, ecs_scr, y_scr, xt_scr, dec_scr, c_scr, b_scr)

    ltail_scr[...] = proj_ref[ROWS - SUBLANES:ROWS, COL_XLRU:COL_XLRU + LRU_WIDTH]
    stail_scr[...] = proj_ref[ROWS - SUBLANES:ROWS, COL_XBC:COL_XBC + SSD_CONV_DIM]

    y_off, s_new = _ssd_state_step(c_scr[...].astype(BF16), b_scr[...], s_scr[...], dec_scr[0], xt_scr)
    s_scr[...] = s_new
    for t in range(SSD_WIDTH // LANES):
        cols = slice(t * LANES, (t + 1) * LANES)
        y_scr[:, cols] = y_scr[:, cols] + ecs_scr[:, cols] * y_off[:, cols]
    _ssd_finish(proj_ref, y_scr, sg_ref, mixed_ref)

    @pl.when(c == pl.num_programs(1) - 1)
    def _():
        lconv_ref[0] = proj_ref[ROWS - (CONV_WIDTH - 1):ROWS, COL_XLRU:COL_XLRU + LRU_WIDTH]
        sconv_ref[0] = proj_ref[ROWS - (CONV_WIDTH - 1):ROWS, COL_XBC:COL_XBC + SSD_CONV_DIM]
        lh_ref[0] = hcar_ref[...]
        sh_ref[0] = s_scr[...]


def _mixer_sample_kernel(n_inner, proj_ref, lprev_ref, sprev_ref, h0_ref, sin_ref,
                         lcw_ref, lcb_ref, wa_ref, wx_ref, ba_ref, bx_ref, lam_ref, lg_ref,
                         scw_ref, scb_ref, dtb_ref, alog_ref, dexp_ref, sg_ref,
                         mixed_ref, lconv_ref, lh_ref, sconv_ref, sh_ref,
                         hs_scr, yl_scr, xc_scr, xw_scr, ecs_scr, y_scr, xt_scr, dec_scr, c_scr, b_scr):
    i = pl.program_id(1)
    seg = SUBLANES
    n_seq = ROWS // seg
    seq_per_step = n_seq // n_inner

    @pl.when(i == 0)
    def _():
        pos = _row_iota((ROWS, LANES)) % seg

        def make_fix(prev_ref):
            def fix(cols, s, rolled):
                return jnp.where(pos < s, pltpu.roll(prev_ref[:, cols], ROWS - seg + s, axis=0), rolled)
            return fix

        def h0_rows(cols):
            return jnp.concatenate(
                [jnp.broadcast_to(h0_ref[q:q + 1, cols], (seg, LANES)) for q in range(n_seq)], axis=0)

        _mixer_front(True, proj_ref, make_fix(lprev_ref), make_fix(sprev_ref), h0_rows, None,
                     lcw_ref, lcb_ref, wa_ref, wx_ref, ba_ref, bx_ref, lam_ref, lg_ref,
                     scw_ref, scb_ref, dtb_ref, alog_ref, dexp_ref,
                     mixed_ref, hs_scr, yl_scr, xc_scr, xw_scr, ecs_scr, y_scr, xt_scr, dec_scr, c_scr, b_scr)
        for sq in range(n_seq):
            tail = slice((sq + 1) * seg - (CONV_WIDTH - 1), (sq + 1) * seg)
            lconv_ref[sq] = proj_ref[tail, COL_XLRU:COL_XLRU + LRU_WIDTH]
            sconv_ref[sq] = proj_ref[tail, COL_XBC:COL_XBC + SSD_CONV_DIM]
            lh_ref[sq:sq + 1, :] = hs_scr[(sq + 1) * seg - 1:(sq + 1) * seg, :]

    rgroup = _row_iota((ROWS, 2 * SSD_STATE)) // seg
    for jj in range(seq_per_step):
        q = i * seq_per_step + jj
        r0 = pl.multiple_of(q * seg, seg)
        c_rows = c_scr[pl.ds(r0, seg), :].astype(BF16)
        b_all = b_scr[...]
        b_mask = jnp.where(rgroup == q, b_all, jnp.zeros_like(b_all))
        y_off, s_new = _ssd_state_step(c_rows, b_mask, sin_ref[jj], dec_scr[q], xt_scr)
        sh_ref[jj] = s_new
        y_scr[pl.ds(r0, seg), :] = y_scr[pl.ds(r0, seg), :] + ecs_scr[pl.ds(r0, seg), :] * y_off

    @pl.when(i == n_inner - 1)
    def _():
        _ssd_finish(proj_ref, y_scr, sg_ref, mixed_ref)


def _mixer_scratch(n_dec):
    return [
        pltpu.VMEM((ROWS, LRU_WIDTH), F32),
        pltpu.VMEM((ROWS, LRU_WIDTH), F32),
        pltpu.VMEM((ROWS, SSD_CONV_DIM), F32),
        pltpu.VMEM((ROWS, SSD_WIDTH), F32),
        pltpu.VMEM((ROWS, SSD_WIDTH), F32),
        pltpu.VMEM((ROWS, SSD_WIDTH), F32),
        pltpu.VMEM((SSD_WIDTH, ROWS), BF16),
        pltpu.VMEM((n_dec, SSD_HEADS, LANES), F32),
        pltpu.VMEM((ROWS, 2 * SSD_STATE), F32),
        pltpu.VMEM((ROWS, 2 * SSD_STATE), BF16),
    ]


def _full_spec(shape, n_grid):
    zeros = (0,) * len(shape)
    if n_grid == 2:
        return pl.BlockSpec(shape, lambda i, j: zeros)
    return pl.BlockSpec(shape, lambda i: zeros)


def _mixer_weight_specs(n_grid):
    return [
        _full_spec((CONV_WIDTH, LRU_WIDTH), n_grid),
        _full_spec((1, LRU_WIDTH), n_grid),
        _full_spec((LRU_HEADS, LANES, LANES), n_grid),
        _full_spec((LRU_HEADS, LANES, LANES), n_grid),
        _full_spec((1, LRU_WIDTH), n_grid),
        _full_spec((1, LRU_WIDTH), n_grid),
        _full_spec((1, LRU_WIDTH), n_grid),
        _full_spec((1, LRU_WIDTH), n_grid),
        _full_spec((CONV_WIDTH, SSD_CONV_DIM), n_grid),
        _full_spec((1, SSD_CONV_DIM), n_grid),
        _full_spec((1, LANES), n_grid),
        _full_spec((1, LANES), n_grid),
        _full_spec((1, SSD_WIDTH), n_grid),
        _full_spec((1, SSD_WIDTH), n_grid),
    ]


def _mixer_prompt(proj, weights, *, batch, seq_len):
    nc = seq_len // ROWS
    return pl.pallas_call(
        _mixer_prompt_kernel,
        grid=(batch, nc),
        in_specs=[pl.BlockSpec((ROWS, D_IN_PAD), lambda b, c: (b * nc + c, 0))] + _mixer_weight_specs(2),
        out_specs=[
            pl.BlockSpec((ROWS, LRU_WIDTH + SSD_WIDTH), lambda b, c: (b * nc + c, 0)),
            pl.BlockSpec((1, CONV_WIDTH - 1, LRU_WIDTH), lambda b, c: (b, 0, 0)),
            pl.BlockSpec((1, 1, LRU_WIDTH), lambda b, c: (b, 0, 0)),
            pl.BlockSpec((1, CONV_WIDTH - 1, SSD_CONV_DIM), lambda b, c: (b, 0, 0)),
            pl.BlockSpec((1, SSD_WIDTH, SSD_STATE), lambda b, c: (b, 0, 0)),
        ],
        out_shape=[
            jax.ShapeDtypeStruct((batch * seq_len, LRU_WIDTH + SSD_WIDTH), BF16),
            jax.ShapeDtypeStruct((batch, CONV_WIDTH - 1, LRU_WIDTH), F32),
            jax.ShapeDtypeStruct((batch, 1, LRU_WIDTH), F32),
            jax.ShapeDtypeStruct((batch, CONV_WIDTH - 1, SSD_CONV_DIM), F32),
            jax.ShapeDtypeStruct((batch, SSD_WIDTH, SSD_STATE), F32),
        ],
        scratch_shapes=[
            pltpu.VMEM((1, LRU_WIDTH), F32),
            pltpu.VMEM((SUBLANES, LRU_WIDTH), F32),
            pltpu.VMEM((SUBLANES, SSD_CONV_DIM), F32),
            pltpu.VMEM((SSD_WIDTH, SSD_STATE), F32),
        ] + _mixer_scratch(1),
        compiler_params=_cparams(2),
        name="mixer_prompt",
    )(proj, *weights)


def _mixer_sample(proj, lprev, sprev, h0, s_in, weights, *, batch, seq_len, row_block_offset, n_inner=4):
    n_seq = ROWS // seq_len
    n_outer = batch // n_seq
    sps = n_seq // n_inner
    return pl.pallas_call(
        functools.partial(_mixer_sample_kernel, n_inner),
        grid=(n_outer, n_inner),
        in_specs=[
            pl.BlockSpec((ROWS, D_IN_PAD), lambda o, i: (row_block_offset + o, 0)),
            pl.BlockSpec((ROWS, LRU_WIDTH), lambda o, i: (o, 0)),
            pl.BlockSpec((ROWS, SSD_CONV_DIM), lambda o, i: (o, 0)),
            pl.BlockSpec((n_seq, LRU_WIDTH), lambda o, i: (o, 0)),
            pl.BlockSpec((sps, SSD_WIDTH, SSD_STATE), lambda o, i: (o * n_inner + i, 0, 0)),
        ] + _mixer_weight_specs(2),
        out_specs=[
            pl.BlockSpec((ROWS, LRU_WIDTH + SSD_WIDTH), lambda o, i: (o, 0)),
            pl.BlockSpec((n_seq, CONV_WIDTH - 1, LRU_WIDTH), lambda o, i: (o, 0, 0)),
            pl.BlockSpec((n_seq, LRU_WIDTH), lambda o, i: (o, 0)),
            pl.BlockSpec((n_seq, CONV_WIDTH - 1, SSD_CONV_DIM), lambda o, i: (o, 0, 0)),
            pl.BlockSpec((sps, SSD_WIDTH, SSD_STATE), lambda o, i: (o * n_inner + i, 0, 0)),
        ],
        out_shape=[
            jax.ShapeDtypeStruct((batch * seq_len, LRU_WIDTH + SSD_WIDTH), BF16),
            jax.ShapeDtypeStruct((batch, CONV_WIDTH - 1, LRU_WIDTH), F32),
            jax.ShapeDtypeStruct((batch, LRU_WIDTH), F32),
            jax.ShapeDtypeStruct((batch, CONV_WIDTH - 1, SSD_CONV_DIM), F32),
            jax.ShapeDtypeStruct((batch, SSD_WIDTH, SSD_STATE), F32),
        ],
        scratch_shapes=_mixer_scratch(n_seq),
        compiler_params=_cparams(2),
        name="mixer_sample",
    )(proj, lprev, sprev, h0, s_in, *weights)


def _row(v):
    return v.reshape(1, -1).astype(F32)


def _pad_lanes(v):
    v = v.reshape(1, -1).astype(F32)
    return jnp.pad(v, ((0, 0), (0, LANES - v.shape[1])))


def kernel(x_prompt, mem_prompt, x_sample, cache_mem_k, cache_mem_v, state_lru_conv, state_lru_h, state_ssd_conv, state_ssd_h, ffn1_norm_g, ffn1_w_gate, ffn1_w_up, ffn1_w_down, mix_norm_g, w_in, lru_conv_w, lru_conv_b, lru_w_a, lru_b_a, lru_w_x, lru_b_x, lru_lambda, lru_out_norm_g, ssd_conv_w, ssd_conv_b, ssd_dt_bias, ssd_a_log, ssd_d, ssd_out_norm_g, w_out, xattn_norm_g, mem_norm_g, xattn_w_q, xattn_w_k, xattn_w_v, xattn_w_o, ffn2_norm_g, ffn2_w_gate, ffn2_w_up, ffn2_w_down, final_norm_g):
    depth = ffn1_norm_g.shape[0]
    assert depth == 1
    bp, tp, d = x_prompt.shape
    bs, ts, _ = x_sample.shape
    mp = bp * tp
    ms = bs * ts
    assert tp % ROWS == 0 and ROWS % ts == 0 and ts == SUBLANES and mp % ROWS == 0
    l = 0

    w_in_p = jnp.pad(w_in[l], ((0, 0), (0, D_IN_PAD - D_IN))).astype(BF16)
    mixer_weights = (
        lru_conv_w[l], _row(lru_conv_b[l]), lru_w_a[l].astype(BF16), lru_w_x[l].astype(BF16),
        _row(lru_b_a[l]), _row(lru_b_x[l]), _row(lru_lambda[l]), _row(lru_out_norm_g[l]),
        ssd_conv_w[l], _row(ssd_conv_b[l]), _pad_lanes(ssd_dt_bias[l]), _pad_lanes(ssd_a_log[l]),
        _row(jnp.repeat(ssd_d[l], SSD_HEADDIM)), _row(ssd_out_norm_g[l]),
    )

    (x1,) = _ffn((x_prompt.reshape(mp, d), x_sample.reshape(ms, d)), _row(ffn1_norm_g[l]),
                 ffn1_w_gate[l].astype(BF16), ffn1_w_up[l].astype(BF16), ffn1_w_down[l].astype(BF16),
                 _row(final_norm_g), out_rows=(mp + ms,), final_norm=False)
    proj = _norm_matmul(x1, _row(mix_norm_g[l]), w_in_p, tm=1024, tn=1024, out_dtype=F32, name="in_proj")

    mixed_p, p_lc, p_lh, p_sc, p_sh = _mixer_prompt(proj, mixer_weights, batch=bp, seq_len=tp)

    pad_rows = ((0, 0), (SUBLANES - (CONV_WIDTH - 1), 0), (0, 0))
    lprev = jnp.pad(state_lru_conv[l], pad_rows).reshape(ms, LRU_WIDTH)
    sprev = jnp.pad(state_ssd_conv[l], pad_rows).reshape(ms, SSD_CONV_DIM)
    mixed_s, s_lc, s_lh, s_sc, s_sh = _mixer_sample(
        proj, lprev, sprev, state_lru_h[l], state_ssd_h[l].reshape(bs, SSD_WIDTH, SSD_STATE),
        mixer_weights, batch=bs, seq_len=ts, row_block_offset=mp // ROWS)

    x2 = _matmul_residual(mixed_p, mixed_s, w_out[l].astype(BF16), x1, tm=1024, tn=512, name="out_proj")

    q = _norm_matmul(x2, _row(xattn_norm_g[l]), xattn_w_q[l].astype(BF16), tm=1024, tn=1024,
                     out_dtype=BF16, name="q_proj")
    mem = mem_prompt.reshape(bp * N_MEM, d)
    mk = _norm_matmul(mem, _row(mem_norm_g[l]), xattn_w_k[l].astype(BF16), tm=1024, tn=1024,
                      out_dtype=F32, name="mem_k")
    mv = _norm_matmul(mem, _row(mem_norm_g[l]), xattn_w_v[l].astype(BF16), tm=1024, tn=1024,
                      out_dtype=F32, name="mem_v")
    o_p = _xattn_prompt(q, mk.reshape(bp, N_MEM, d), mv.reshape(bp, N_MEM, d), seq_len=tp, tq=512)
    n_seq_x = 4
    o_s = _xattn_sample(q, cache_mem_k[l], cache_mem_v[l],
                        seq_len=ts, n_seq=n_seq_x, row_block_offset=mp // (n_seq_x * ts))
    x3 = _matmul_residual(o_p, o_s, xattn_w_o[l].astype(BF16), x2, tm=1024, tn=1024, name="o_proj")

    y_p, y_s = _ffn((x3,), _row(ffn2_norm_g[l]), ffn2_w_gate[l].astype(BF16), ffn2_w_up[l].astype(BF16),
                    ffn2_w_down[l].astype(BF16), _row(final_norm_g), out_rows=(mp, ms), final_norm=True)

    y_prompt = y_p.reshape(bp, tp, d)
    y_sample = y_s.reshape(bs, ts, d)
    hshape = (SSD_HEADS, SSD_HEADDIM, SSD_STATE)
    return (y_prompt, y_sample,
            p_lc[None], p_lh.reshape(1, bp, LRU_WIDTH), p_sc[None], p_sh.reshape((1, bp) + hshape),
            mk.reshape(1, bp, N_MEM, XATTN_HEADS, XATTN_HEAD_DIM),
            mv.reshape(1, bp, N_MEM, XATTN_HEADS, XATTN_HEAD_DIM),
            s_lc[None], s_lh[None], s_sc[None], s_sh.reshape((1, bs) + hshape))
```

```python
import functools

import jax
import jax.numpy as jnp
from jax import lax
from jax.experimental import pallas as pl
from jax.experimental.pallas import tpu as pltpu

F32 = jnp.float32
BF16 = jnp.bfloat16

D_MODEL = 2048
LRU_WIDTH = 2048
LRU_HEADS = 16
LRU_C = 8.0
CONV_WIDTH = 4
SSD_WIDTH = 2048
SSD_HEADDIM = 64
SSD_HEADS = 32
SSD_GROUPS = 2
SSD_STATE = 128
SSD_CONV_DIM = SSD_WIDTH + 2 * SSD_GROUPS * SSD_STATE
D_IN = 3 * 2048 + SSD_CONV_DIM + SSD_HEADS
N_MEM = 256
XATTN_HEADS = 4
XATTN_HEAD_DIM = 512
EPS = 1e-6

LANES = 128
SUBLANES = 8
VMEM_LIMIT_BYTES = 56 * 1024 * 1024

D_A = 3 * 2048
W_A_TILE = 512
COL_XLRU = 0
COL_GLRU = 2048
COL_Z = 4096
D_B = SSD_CONV_DIM + SSD_HEADS
D_B_PAD = 3072
COL_XBC = 0
COL_DT = SSD_CONV_DIM
ROWS = 128
GROUP_COLS = SSD_WIDTH // SSD_GROUPS


def _cparams(n_axes):
    return pltpu.CompilerParams(
        dimension_semantics=("arbitrary",) * n_axes,
        vmem_limit_bytes=VMEM_LIMIT_BYTES)


def _rmsnorm_rows(x, g):
    ms = jnp.mean(x * x, axis=-1, keepdims=True)
    return (x * lax.rsqrt(ms + EPS)) * g


def _softplus(x):
    return jnp.maximum(x, 0.0) + jnp.log1p(jnp.exp(-jnp.abs(x)))


def _silu(x):
    return x * jax.nn.sigmoid(x)


def _when_rows(i, n_a, fn_a, fn_b, extra=None):
    in_a = i < n_a
    in_b = i >= n_a
    if extra is not None:
        in_a = jnp.logical_and(in_a, extra)
        in_b = jnp.logical_and(in_b, extra)
    pl.when(in_a)(fn_a)
    pl.when(in_b)(fn_b)


def _ffn_kernel(n_a, split_in, split_out, final_norm, *refs):
    refs = list(refs)
    xa_ref = refs.pop(0)
    xb_ref = refs.pop(0) if split_in else xa_ref
    g_ref, wg_ref, wu_ref, wd_ref, gf_ref = refs[:5]
    oa_ref = refs[5]
    ob_ref = refs[6] if split_out else oa_ref
    h_scr, acc_scr = refs[-2:]
    i = pl.program_id(0)
    j = pl.program_id(1)

    def prologue(x_ref):
        def run():
            x = x_ref[...]
            h_scr[...] = _rmsnorm_rows(x, g_ref[...]).astype(BF16)
            acc_scr[...] = x
        return run

    if split_in:
        _when_rows(i, n_a, prologue(xa_ref), prologue(xb_ref), extra=(j == 0))
    else:
        pl.when(j == 0)(prologue(xa_ref))

    h = h_scr[...]
    gate = jnp.dot(h, wg_ref[...], preferred_element_type=F32)
    up = jnp.dot(h, wu_ref[...], preferred_element_type=F32)
    a = (0.5 * _silu(gate) * up).astype(BF16)
    acc_scr[...] += jnp.dot(a, wd_ref[...], preferred_element_type=F32)

    def epilogue(o_ref):
        def run():
            res = acc_scr[...]
            if final_norm:
                res = _rmsnorm_rows(res, gf_ref[...])
            o_ref[...] = res
        return run

    last = j == pl.num_programs(1) - 1
    if split_out:
        _when_rows(i, n_a, epilogue(oa_ref), epilogue(ob_ref), extra=last)
    else:
        pl.when(last)(epilogue(oa_ref))


def _ffn(xs, g, wg, wu, wd, gf, *, out_rows, final_norm, tm=512, tf=512):
    split_in = len(xs) == 2
    split_out = len(out_rows) == 2
    d = xs[0].shape[1]
    f = wg.shape[1]
    m = sum(x.shape[0] for x in xs)
    assert m == sum(out_rows) and all(x.shape[0] % tm == 0 for x in xs) and all(r % tm == 0 for r in out_rows)
    n_a = (xs[0].shape[0] if split_in else out_rows[0]) // tm
    if split_in and split_out:
        assert xs[0].shape[0] == out_rows[0]

    def first(i, j):
        return (jnp.minimum(i, n_a - 1), 0)

    def second(i, j):
        return (jnp.maximum(i - n_a, 0), 0)

    def whole(i, j):
        return (i, 0)

    x_specs = ([pl.BlockSpec((tm, d), first), pl.BlockSpec((tm, d), second)] if split_in
               else [pl.BlockSpec((tm, d), whole)])
    o_specs = ([pl.BlockSpec((tm, d), first), pl.BlockSpec((tm, d), second)] if split_out
               else [pl.BlockSpec((tm, d), whole)])
    return pl.pallas_call(
        functools.partial(_ffn_kernel, n_a, split_in, split_out, final_norm),
        grid=(m // tm, f // tf),
        in_specs=x_specs + [
            pl.BlockSpec((1, d), lambda i, j: (0, 0)),
            pl.BlockSpec((d, tf), lambda i, j: (0, j)),
            pl.BlockSpec((d, tf), lambda i, j: (0, j)),
            pl.BlockSpec((tf, d), lambda i, j: (j, 0)),
            pl.BlockSpec((1, d), lambda i, j: (0, 0)),
        ],
        out_specs=o_specs,
        out_shape=[jax.ShapeDtypeStruct((r, d), F32) for r in out_rows],
        scratch_shapes=[pltpu.VMEM((tm, d), BF16), pltpu.VMEM((tm, d), F32)],
        compiler_params=_cparams(2),
        name="ffn_final" if final_norm else "ffn",
    )(*xs, g, wg, wu, wd, gf)


def _norm_matmul_kernel(x_ref, g_ref, w_ref, o_ref, h_scr):
    @pl.when(pl.program_id(1) == 0)
    def _():
        h_scr[...] = _rmsnorm_rows(x_ref[...], g_ref[...]).astype(BF16)

    o_ref[...] = jnp.dot(h_scr[...], w_ref[...].astype(BF16), preferred_element_type=F32).astype(o_ref.dtype)


def _norm_matmul(x, g, w, *, tm, tn, out_dtype, name, row_offset=0, n_cols=None):
    k = x.shape[1]
    m = x.shape[0] - row_offset
    n = w.shape[1] if n_cols is None else n_cols
    assert m % tm == 0 and row_offset % tm == 0 and n % tn == 0
    first = row_offset // tm
    return pl.pallas_call(
        _norm_matmul_kernel,
        grid=(m // tm, n // tn),
        in_specs=[
            pl.BlockSpec((tm, k), lambda i, j: (first + i, 0)),
            pl.BlockSpec((1, k), lambda i, j: (0, 0)),
            pl.BlockSpec((k, tn), lambda i, j: (0, j)),
        ],
        out_specs=pl.BlockSpec((tm, tn), lambda i, j: (i, j)),
        out_shape=jax.ShapeDtypeStruct((m, n), out_dtype),
        scratch_shapes=[pltpu.VMEM((tm, k), BF16)],
        compiler_params=_cparams(2),
        name=name,
    )(x, g, w)


def _matmul_residual_kernel(n_a, aa_ref, ab_ref, w_ref, r_ref, o_ref):
    def run(a_ref):
        def body():
            o_ref[...] = r_ref[...] + jnp.dot(a_ref[...], w_ref[...].astype(BF16),
                                              preferred_element_type=F32)
        return body

    _when_rows(pl.program_id(0), n_a, run(aa_ref), run(ab_ref))


def _matmul_residual(a_first, a_second, w, res, *, tm, tn, name):
    k = a_first.shape[1]
    m = a_first.shape[0] + a_second.shape[0]
    n = w.shape[1]
    assert a_first.shape[0] % tm == 0 and a_second.shape[0] % tm == 0 and res.shape == (m, n)
    n_a = a_first.shape[0] // tm
    return pl.pallas_call(
        functools.partial(_matmul_residual_kernel, n_a),
        grid=(m // tm, n // tn),
        in_specs=[
            pl.BlockSpec((tm, k), lambda i, j: (jnp.minimum(i, n_a - 1), 0)),
            pl.BlockSpec((tm, k), lambda i, j: (jnp.maximum(i - n_a, 0), 0)),
            pl.BlockSpec((k, tn), lambda i, j: (0, j)),
            pl.BlockSpec((tm, tn), lambda i, j: (i, j)),
        ],
        out_specs=pl.BlockSpec((tm, tn), lambda i, j: (i, j)),
        out_shape=jax.ShapeDtypeStruct((m, n), F32),
        compiler_params=_cparams(2),
        name=name,
    )(a_first, a_second, w, res)


def _xattn_kernel(n_seq, tq, q_ref, k_ref, v_ref, o_ref):
    scale = XATTN_HEAD_DIM ** -0.5
    for s in range(n_seq):
        rows = slice(s * tq, (s + 1) * tq)
        for h in range(XATTN_HEADS):
            cols = slice(h * XATTN_HEAD_DIM, (h + 1) * XATTN_HEAD_DIM)
            q = q_ref[rows, cols]
            k = k_ref[s, :, cols].astype(BF16)
            v = v_ref[s, :, cols].astype(BF16)
            sc = lax.dot_general(q, k, (((1,), (1,)), ((), ())),
                                 preferred_element_type=F32) * scale
            mx = jnp.max(sc, axis=-1, keepdims=True)
            e = jnp.exp(sc - mx)
            p = e / jnp.sum(e, axis=-1, keepdims=True)
            o = jnp.dot(p.astype(BF16), v, preferred_element_type=F32)
            o_ref[rows, cols] = o.astype(BF16)


def _xattn_prompt(q, k, v, *, seq_len, tq):
    b = k.shape[0]
    nt = seq_len // tq
    d = q.shape[1]
    return pl.pallas_call(
        functools.partial(_xattn_kernel, 1, tq),
        grid=(b, nt),
        in_specs=[
            pl.BlockSpec((tq, d), lambda i, j: (i * nt + j, 0)),
            pl.BlockSpec((1, N_MEM, d), lambda i, j: (i, 0, 0)),
            pl.BlockSpec((1, N_MEM, d), lambda i, j: (i, 0, 0)),
        ],
        out_specs=pl.BlockSpec((tq, d), lambda i, j: (i * nt + j, 0)),
        out_shape=jax.ShapeDtypeStruct((b * seq_len, d), BF16),
        compiler_params=_cparams(2),
        name="xattn_prompt",
    )(q, k, v)


def _xattn_cache_kernel(n_seq, tq, q_ref, k_ref, v_ref, o_ref):
    scale = XATTN_HEAD_DIM ** -0.5
    n_rows = XATTN_HEADS * tq
    n_cols = N_MEM * XATTN_HEADS
    own = (_lane_iota((n_rows, n_cols)) % XATTN_HEADS) == (_row_iota((n_rows, n_cols)) // tq)
    qf = q_ref[...].astype(F32)
    outs = []
    for s in range(n_seq):
        qs = qf[s * tq:(s + 1) * tq]
        q4 = jnp.concatenate(
            [qs[:, h * XATTN_HEAD_DIM:(h + 1) * XATTN_HEAD_DIM] for h in range(XATTN_HEADS)],
            axis=0).astype(BF16)
        kf = k_ref[s].reshape(n_cols, XATTN_HEAD_DIM).astype(BF16)
        vf = v_ref[s].reshape(n_cols, XATTN_HEAD_DIM).astype(BF16)
        sc = lax.dot_general(q4, kf, (((1,), (1,)), ((), ())), preferred_element_type=F32) * scale
        sc = jnp.where(own, sc, -1e30)
        mx = jnp.max(sc, axis=-1, keepdims=True)
        e = jnp.exp(sc - mx)
        p = e / jnp.sum(e, axis=-1, keepdims=True)
        o4 = jnp.dot(p.astype(BF16), vf, preferred_element_type=F32)
        outs.append(jnp.concatenate([o4[h * tq:(h + 1) * tq] for h in range(XATTN_HEADS)], axis=1))
    o_ref[...] = jnp.concatenate(outs, axis=0).astype(BF16)


def _xattn_sample(q, k, v, *, seq_len, n_seq, row_block_offset):
    b = k.shape[0]
    d = q.shape[1]
    rows = n_seq * seq_len
    kv_block = (n_seq, N_MEM, XATTN_HEADS, XATTN_HEAD_DIM)
    return pl.pallas_call(
        functools.partial(_xattn_cache_kernel, n_seq, seq_len),
        grid=(b // n_seq,),
        in_specs=[
            pl.BlockSpec((rows, d), lambda i: (row_block_offset + i, 0)),
            pl.BlockSpec(kv_block, lambda i: (i, 0, 0, 0)),
            pl.BlockSpec(kv_block, lambda i: (i, 0, 0, 0)),
        ],
        out_specs=pl.BlockSpec((rows, d), lambda i: (i, 0)),
        out_shape=jax.ShapeDtypeStruct((b * seq_len, d), BF16),
        compiler_params=_cparams(1),
        name="xattn_sample",
    )(q, k, v)


def _row_iota(shape):
    return lax.broadcasted_iota(jnp.int32, shape, 0)


def _lane_iota(shape):
    return lax.broadcasted_iota(jnp.int32, shape, 1)


def _seg_cumsum(x, seg_len):
    pos = _row_iota(x.shape) % seg_len
    s = 1
    while s < seg_len:
        x = x + jnp.where(pos >= s, pltpu.roll(x, s, axis=0), 0.0)
        s *= 2
    return x


def _seg_rev_excl_cumsum(x, seg_len):
    n = x.shape[0]
    pos = _row_iota(x.shape) % seg_len
    y = jnp.where(pos < seg_len - 1, pltpu.roll(x, n - 1, axis=0), 0.0)
    s = 1
    while s < seg_len:
        y = y + jnp.where(pos < seg_len - s, pltpu.roll(y, n - s, axis=0), 0.0)
        s *= 2
    return y


def _scan8(a, b):
    shape = a.shape
    tiled = (shape[0] // SUBLANES, SUBLANES, shape[1])
    a = a.reshape(tiled)
    b = b.reshape(tiled)
    pos = lax.broadcasted_iota(jnp.int32, tiled, 1)
    for s in (1, 2, 4):
        m = pos >= s
        a_sh = pltpu.roll(a, s, axis=1)
        b_sh = pltpu.roll(b, s, axis=1)
        b = jnp.where(m, a * b_sh + b, b)
        a = jnp.where(m, a * a_sh, a)
    return a.reshape(shape), b.reshape(shape)


def _conv_taps(x, w_ref, b_ref, cols, fix):
    acc = b_ref[:, cols] + x * w_ref[CONV_WIDTH - 1:CONV_WIDTH, cols]
    for s in range(1, CONV_WIDTH):
        xs = fix(s, pltpu.roll(x, s, axis=0))
        k = CONV_WIDTH - 1 - s
        acc = acc + xs * w_ref[k:k + 1, cols]
    return acc


def _bcast_col(v, c):
    return jnp.broadcast_to(v[:, c:c + 1], (v.shape[0], LANES))


def _lru_part(sample, pa_ref, prev_fix_lru, h0_rows, hcar_ref,
              lcw_ref, lcb_ref, wa_ref, wx_ref, ba_ref, bx_ref, lam_ref, lg_ref,
              mixed_ref, hs_scr, yl_scr):
    ssq = jnp.zeros((ROWS, 1), F32)
    for hd in range(LRU_HEADS):
        cols = slice(hd * LANES, (hd + 1) * LANES)
        xl = pa_ref[:, COL_XLRU + hd * LANES:COL_XLRU + (hd + 1) * LANES]
        u = _conv_taps(xl, lcw_ref, lcb_ref, cols, functools.partial(prev_fix_lru, cols))
        ub = u.astype(BF16)
        r = jax.nn.sigmoid(jnp.dot(ub, wa_ref[hd], preferred_element_type=F32) + ba_ref[:, cols])
        i = jax.nn.sigmoid(jnp.dot(ub, wx_ref[hd], preferred_element_type=F32) + bx_ref[:, cols])
        log_a = (-LRU_C * r) * _softplus(-lam_ref[:, cols])
        a = jnp.exp(log_a)
        th = jnp.tanh(log_a)
        beta = jnp.sqrt((-2.0 * th) / (1.0 - th))
        bb = (beta * i) * u
        a_cum, b_cum = _scan8(a, bb)
        if sample:
            h = a_cum * h0_rows(cols) + b_cum
        else:
            carry = hcar_ref[:, cols]
            parts = []
            for t in range(ROWS // SUBLANES):
                rs = slice(t * SUBLANES, (t + 1) * SUBLANES)
                ht = a_cum[rs] * carry + b_cum[rs]
                parts.append(ht)
                carry = ht[SUBLANES - 1:SUBLANES]
            h = jnp.concatenate(parts, axis=0)
            hcar_ref[:, cols] = carry
        hs_scr[:, cols] = h
        g = jax.nn.gelu(pa_ref[:, COL_GLRU + hd * LANES:COL_GLRU + (hd + 1) * LANES])
        yl = h * g
        ssq = ssq + jnp.sum(yl * yl, axis=-1, keepdims=True)
        yl_scr[:, cols] = yl
    inv = lax.rsqrt(ssq * (1.0 / LRU_WIDTH) + EPS)
    for hd in range(LRU_HEADS):
        cols = slice(hd * LANES, (hd + 1) * LANES)
        mixed_ref[:, cols] = ((yl_scr[:, cols] * inv) * lg_ref[:, cols]).astype(BF16)


def _ssd_front(sample, pb_ref, prev_fix_ssd, scw_ref, scb_ref, dtb_ref, alog_ref, dexp_ref,
               xc_scr, xw_scr, ecs_scr, y_scr, xt_scr, dec_scr, c_scr, b_scr, side_work=()):
    seg = SUBLANES if sample else ROWS
    n_seq = ROWS // seg
    n_conv_tiles = SSD_CONV_DIM // LANES
    n_pairs = SSD_HEADS // 2
    pending = list(side_work)
    stride = max(1, (n_conv_tiles + n_pairs) // max(1, len(pending)))
    point = [0]

    def side_point():
        if pending and point[0] % stride == 0:
            pending.pop(0)()
        point[0] += 1

    for t in range(n_conv_tiles):
        side_point()
        cols = slice(t * LANES, (t + 1) * LANES)
        xb = pb_ref[:, COL_XBC + t * LANES:COL_XBC + (t + 1) * LANES]
        v = _conv_taps(xb, scw_ref, scb_ref, cols, functools.partial(prev_fix_ssd, cols))
        xc_scr[:, cols] = _silu(v)

    dt = _softplus(pb_ref[:, COL_DT:COL_DT + LANES] + dtb_ref[...])
    a_neg = -jnp.exp(alog_ref[...])
    d_a = dt * a_neg
    cs = _seg_cumsum(d_a, seg)
    rcs = _seg_rev_excl_cumsum(d_a, seg)
    ecs = jnp.exp(cs)
    todt = jnp.exp(rcs) * dt
    dec_tot = jnp.exp(cs + rcs)
    cs_t = cs.T
    dt_t = dt.T
    dec_t = dec_tot.T
    for s in range(n_seq):
        dec_scr[s] = jnp.broadcast_to(dec_t[0:SSD_HEADS, s * seg:s * seg + 1], (SSD_HEADS, LANES))

    ri = _row_iota((ROWS, ROWS))
    ci = _lane_iota((ROWS, ROWS))
    causal = ri >= ci
    if sample:
        causal = jnp.logical_and(causal, (ri // seg) == (ci // seg))
    lo = _lane_iota((ROWS, LANES)) < SSD_HEADDIM

    b_off = SSD_WIDTH
    c_off = SSD_WIDTH + SSD_GROUPS * SSD_STATE
    for g in range(SSD_GROUPS):
        gcols = slice(g * SSD_STATE, (g + 1) * SSD_STATE)
        bg = xc_scr[:, b_off + g * SSD_STATE:b_off + (g + 1) * SSD_STATE].astype(BF16)
        cg = xc_scr[:, c_off + g * SSD_STATE:c_off + (g + 1) * SSD_STATE].astype(BF16)
        c_scr[:, gcols] = xc_scr[:, c_off + g * SSD_STATE:c_off + (g + 1) * SSD_STATE]
        b_scr[:, gcols] = bg
        cb = lax.dot_general(cg, bg, (((1,), (1,)), ((), ())), preferred_element_type=F32)
        heads_per_group = SSD_HEADS // SSD_GROUPS
        for jp in range(heads_per_group // 2):
            side_point()
            h0 = g * heads_per_group + 2 * jp
            h1 = h0 + 1
            cols = slice(h0 * SSD_HEADDIM, h0 * SSD_HEADDIM + LANES)
            x_pair = xc_scr[:, cols]
            ws = []
            for hh in (h0, h1):
                diff = _bcast_col(cs, hh) - cs_t[hh:hh + 1, :]
                dec = jnp.where(causal, jnp.exp(diff), 0.0)
                ws.append(((cb * dec) * dt_t[hh:hh + 1, :]).astype(BF16))
            lhs = jnp.concatenate(ws, axis=1)
            rhs = jnp.concatenate([jnp.where(lo, x_pair, 0.0), jnp.where(lo, 0.0, x_pair)],
                                  axis=0).astype(BF16)
            y_diag = jnp.dot(lhs, rhs, preferred_element_type=F32)
            y_scr[:, cols] = y_diag + x_pair * dexp_ref[:, cols]
            xw_scr[:, cols] = x_pair * jnp.where(lo, _bcast_col(todt, h0), _bcast_col(todt, h1))
            ecs_scr[:, cols] = jnp.where(lo, _bcast_col(ecs, h0), _bcast_col(ecs, h1))
    while pending:
        pending.pop(0)()
    for t in range(SSD_WIDTH // LANES):
        xt_scr[t * LANES:(t + 1) * LANES, :] = xw_scr[:, t * LANES:(t + 1) * LANES].T.astype(BF16)


def _ssd_state_step(c_rows, b_mask, s_old, dec, xt_scr):
    y_parts = []
    s_parts = []
    heads_per_group = SSD_HEADS // SSD_GROUPS
    for g in range(SSD_GROUPS):
        gcols = slice(g * SSD_STATE, (g + 1) * SSD_STATE)
        grows = slice(g * GROUP_COLS, (g + 1) * GROUP_COLS)
        sg = s_old[grows]
        y_parts.append(lax.dot_general(c_rows[:, gcols], sg.astype(BF16), (((1,), (1,)), ((), ())),
                                       preferred_element_type=F32))
        upd = jnp.dot(xt_scr[grows, :], b_mask[:, gcols], preferred_element_type=F32)
        for hh in range(heads_per_group):
            hr = slice(hh * SSD_HEADDIM, (hh + 1) * SSD_HEADDIM)
            habs = g * heads_per_group + hh
            s_parts.append(sg[hr] * dec[habs:habs + 1, :] + upd[hr])
    return jnp.concatenate(y_parts, axis=1), jnp.concatenate(s_parts, axis=0)


def _ssd_finish(pa_ref, y_scr, sg_ref, mixed_ref):
    for g in range(SSD_GROUPS):
        ssq = jnp.zeros((ROWS, 1), F32)
        n_t = GROUP_COLS // LANES
        for t in range(n_t):
            c0 = g * GROUP_COLS + t * LANES
            yz = y_scr[:, c0:c0 + LANES] * _silu(pa_ref[:, COL_Z + c0:COL_Z + c0 + LANES])
            y_scr[:, c0:c0 + LANES] = yz
            ssq = ssq + jnp.sum(yz * yz, axis=-1, keepdims=True)
        inv = lax.rsqrt(ssq * (1.0 / GROUP_COLS) + EPS)
        for t in range(n_t):
            c0 = g * GROUP_COLS + t * LANES
            mixed_ref[:, LRU_WIDTH + c0:LRU_WIDTH + c0 + LANES] = (
                (y_scr[:, c0:c0 + LANES] * inv) * sg_ref[:, c0:c0 + LANES]).astype(BF16)


def _mixer_prompt_kernel(x_ref, xn_ref, mg_ref, wina_ref, pb_ref,
                         lcw_ref, lcb_ref, wa_ref, wx_ref, ba_ref, bx_ref, lam_ref, lg_ref,
                         scw_ref, scb_ref, dtb_ref, alog_ref, dexp_ref, sg_ref,
                         mixed_ref, lconv_ref, lh_ref, sconv_ref, sh_ref,
                         pa_scr, hn_scr, hcar_ref, ltail_scr, stail_scr, s_scr,
                         hs_scr, yl_scr, xc_scr, xw_scr, ecs_scr, y_scr, xt_scr, dec_scr, c_scr, b_scr):
    c = pl.program_id(1)
    step = pl.program_id(0) * pl.num_programs(1) + c
    slot = step % 2
    n_slabs = D_A // W_A_TILE

    @pl.when(c == 0)
    def _():
        hcar_ref[...] = jnp.zeros_like(hcar_ref)
        ltail_scr[...] = jnp.zeros_like(ltail_scr)
        stail_scr[...] = jnp.zeros_like(stail_scr)
        s_scr[...] = jnp.zeros_like(s_scr)

    @pl.when(step == 0)
    def _():
        hn = _rmsnorm_rows(x_ref[...], mg_ref[...]).astype(BF16)
        for j in range(n_slabs):
            pa_scr[0, :, j * W_A_TILE:(j + 1) * W_A_TILE] = jnp.dot(hn, wina_ref[j], preferred_element_type=F32)

    hn_scr[...] = _rmsnorm_rows(xn_ref[...], mg_ref[...]).astype(BF16)
    pa_next = pa_scr.at[1 - slot]
    pa_cur = pa_scr.at[slot]

    def project_slab(j):
        def run():
            pa_next[:, j * W_A_TILE:(j + 1) * W_A_TILE] = jnp.dot(
                hn_scr[...], wina_ref[j], preferred_element_type=F32)
        return run

    side_work = [project_slab(j) for j in range(n_slabs)]

    pos8 = _row_iota((SUBLANES, LANES))

    def make_fix(tail_scr):
        def fix(cols, s, rolled):
            first = jnp.where(pos8 < s, pltpu.roll(tail_scr[:, cols], s, axis=0), rolled[0:SUBLANES])
            return jnp.concatenate([first, rolled[SUBLANES:]], axis=0)
        return fix

    _ssd_front(False, pb_ref, make_fix(stail_scr), scw_ref, scb_ref, dtb_ref, alog_ref, dexp_ref,
               xc_scr, xw_scr, ecs_scr, y_scr, xt_scr, dec_scr, c_scr, b_scr, side_work=side_work)
    _lru_part(False, pa_cur, make_fix(ltail_scr), None, hcar_ref,
              lcw_ref, lcb_ref, wa_ref, wx_ref, ba_ref, bx_ref, lam_ref, lg_ref,
              mixed_ref, hs_scr, yl_scr)

    ltail_scr[...] = pa_cur[ROWS - SUBLANES:ROWS, COL_XLRU:COL_XLRU + LRU_WIDTH]
    stail_scr[...] = pb_ref[ROWS - SUBLANES:ROWS, COL_XBC:COL_XBC + SSD_CONV_DIM]

    y_off, s_new = _ssd_state_step(c_scr[...].astype(BF16), b_scr[...], s_scr[...], dec_scr[0], xt_scr)
    s_scr[...] = s_new
    for t in range(SSD_WIDTH // LANES):
        cols = slice(t * LANES, (t + 1) * LANES)
        y_scr[:, cols] = y_scr[:, cols] + ecs_scr[:, cols] * y_off[:, cols]
    _ssd_finish(pa_cur, y_scr, sg_ref, mixed_ref)

    @pl.when(c == pl.num_programs(1) - 1)
    def _():
        lconv_ref[0] = pa_cur[ROWS - (CONV_WIDTH - 1):ROWS, COL_XLRU:COL_XLRU + LRU_WIDTH]
        sconv_ref[0] = pb_ref[ROWS - (CONV_WIDTH - 1):ROWS, COL_XBC:COL_XBC + SSD_CONV_DIM]
        lh_ref[0] = hcar_ref[...]
        sh_ref[0] = s_scr[...]


def _mixer_sample_kernel(n_inner, pa_ref, pb_ref, lprev_ref, sprev_ref, h0_ref, sin_ref,
                         lcw_ref, lcb_ref, wa_ref, wx_ref, ba_ref, bx_ref, lam_ref, lg_ref,
                         scw_ref, scb_ref, dtb_ref, alog_ref, dexp_ref, sg_ref,
                         mixed_ref, lconv_ref, lh_ref, sconv_ref, sh_ref,
                         hs_scr, yl_scr, xc_scr, xw_scr, ecs_scr, y_scr, xt_scr, dec_scr, c_scr, b_scr):
    i = pl.program_id(1)
    seg = SUBLANES
    n_seq = ROWS // seg
    seq_per_step = n_seq // n_inner

    @pl.when(i == 0)
    def _():
        pos = _row_iota((ROWS, LANES)) % seg

        def make_fix(prev_ref):
            def fix(cols, s, rolled):
                return jnp.where(pos < s, pltpu.roll(prev_ref[:, cols], ROWS - seg + s, axis=0), rolled)
            return fix

        def h0_rows(cols):
            return jnp.concatenate(
                [jnp.broadcast_to(h0_ref[q:q + 1, cols], (seg, LANES)) for q in range(n_seq)], axis=0)

        _ssd_front(True, pb_ref, make_fix(sprev_ref), scw_ref, scb_ref, dtb_ref, alog_ref, dexp_ref,
                   xc_scr, xw_scr, ecs_scr, y_scr, xt_scr, dec_scr, c_scr, b_scr)
        _lru_part(True, pa_ref, make_fix(lprev_ref), h0_rows, None,
                  lcw_ref, lcb_ref, wa_ref, wx_ref, ba_ref, bx_ref, lam_ref, lg_ref,
                  mixed_ref, hs_scr, yl_scr)
        for sq in range(n_seq):
            tail = slice((sq + 1) * seg - (CONV_WIDTH - 1), (sq + 1) * seg)
            lconv_ref[sq] = pa_ref[tail, COL_XLRU:COL_XLRU + LRU_WIDTH]
            sconv_ref[sq] = pb_ref[tail, COL_XBC:COL_XBC + SSD_CONV_DIM]
            lh_ref[sq:sq + 1, :] = hs_scr[(sq + 1) * seg - 1:(sq + 1) * seg, :]

    rgroup = _row_iota((ROWS, 2 * SSD_STATE)) // seg
    for jj in range(seq_per_step):
        q = i * seq_per_step + jj
        r0 = pl.multiple_of(q * seg, seg)
        c_rows = c_scr[pl.ds(r0, seg), :].astype(BF16)
        b_all = b_scr[...]
        b_mask = jnp.where(rgroup == q, b_all, jnp.zeros_like(b_all))
        y_off, s_new = _ssd_state_step(c_rows, b_mask, sin_ref[jj], dec_scr[q], xt_scr)
        sh_ref[jj] = s_new
        y_scr[pl.ds(r0, seg), :] = y_scr[pl.ds(r0, seg), :] + ecs_scr[pl.ds(r0, seg), :] * y_off

    @pl.when(i == n_inner - 1)
    def _():
        _ssd_finish(pa_ref, y_scr, sg_ref, mixed_ref)


def _mixer_scratch(n_dec):
    return [
        pltpu.VMEM((ROWS, LRU_WIDTH), F32),
        pltpu.VMEM((ROWS, LRU_WIDTH), F32),
        pltpu.VMEM((ROWS, SSD_CONV_DIM), F32),
        pltpu.VMEM((ROWS, SSD_WIDTH), F32),
        pltpu.VMEM((ROWS, SSD_WIDTH), F32),
        pltpu.VMEM((ROWS, SSD_WIDTH), F32),
        pltpu.VMEM((SSD_WIDTH, ROWS), BF16),
        pltpu.VMEM((n_dec, SSD_HEADS, LANES), F32),
        pltpu.VMEM((ROWS, 2 * SSD_STATE), F32),
        pltpu.VMEM((ROWS, 2 * SSD_STATE), BF16),
    ]


def _full_spec(shape, n_grid):
    zeros = (0,) * len(shape)
    if n_grid == 2:
        return pl.BlockSpec(shape, lambda i, j: zeros)
    return pl.BlockSpec(shape, lambda i: zeros)


def _mixer_weight_specs(n_grid):
    return [
        _full_spec((CONV_WIDTH, LRU_WIDTH), n_grid),
        _full_spec((1, LRU_WIDTH), n_grid),
        _full_spec((LRU_HEADS, LANES, LANES), n_grid),
        _full_spec((LRU_HEADS, LANES, LANES), n_grid),
        _full_spec((1, LRU_WIDTH), n_grid),
        _full_spec((1, LRU_WIDTH), n_grid),
        _full_spec((1, LRU_WIDTH), n_grid),
        _full_spec((1, LRU_WIDTH), n_grid),
        _full_spec((CONV_WIDTH, SSD_CONV_DIM), n_grid),
        _full_spec((1, SSD_CONV_DIM), n_grid),
        _full_spec((1, LANES), n_grid),
        _full_spec((1, LANES), n_grid),
        _full_spec((1, SSD_WIDTH), n_grid),
        _full_spec((1, SSD_WIDTH), n_grid),
    ]


def _mixer_prompt(x1, mix_g, w_in_a, proj_b, weights, *, batch, seq_len):
    nc = seq_len // ROWS
    return pl.pallas_call(
        _mixer_prompt_kernel,
        grid=(batch, nc),
        in_specs=[
            pl.BlockSpec((ROWS, D_MODEL), lambda b, c: (b * nc + c, 0)),
            pl.BlockSpec((ROWS, D_MODEL), lambda b, c: (jnp.minimum(b * nc + c + 1, batch * nc - 1), 0)),
            pl.BlockSpec((1, D_MODEL), lambda b, c: (0, 0)),
            pl.BlockSpec((D_A // W_A_TILE, D_MODEL, W_A_TILE), lambda b, c: (0, 0, 0),
                         pipeline_mode=pl.Buffered(1)),
            pl.BlockSpec((ROWS, D_B_PAD), lambda b, c: (b * nc + c, 0)),
        ] + _mixer_weight_specs(2),
        out_specs=[
            pl.BlockSpec((ROWS, LRU_WIDTH + SSD_WIDTH), lambda b, c: (b * nc + c, 0)),
            pl.BlockSpec((1, CONV_WIDTH - 1, LRU_WIDTH), lambda b, c: (b, 0, 0)),
            pl.BlockSpec((1, 1, LRU_WIDTH), lambda b, c: (b, 0, 0)),
            pl.BlockSpec((1, CONV_WIDTH - 1, SSD_CONV_DIM), lambda b, c: (b, 0, 0)),
            pl.BlockSpec((1, SSD_WIDTH, SSD_STATE), lambda b, c: (b, 0, 0)),
        ],
        out_shape=[
            jax.ShapeDtypeStruct((batch * seq_len, LRU_WIDTH + SSD_WIDTH), BF16),
            jax.ShapeDtypeStruct((batch, CONV_WIDTH - 1, LRU_WIDTH), F32),
            jax.ShapeDtypeStruct((batch, 1, LRU_WIDTH), F32),
            jax.ShapeDtypeStruct((batch, CONV_WIDTH - 1, SSD_CONV_DIM), F32),
            jax.ShapeDtypeStruct((batch, SSD_WIDTH, SSD_STATE), F32),
        ],
        scratch_shapes=[
            pltpu.VMEM((2, ROWS, D_A), F32),
            pltpu.VMEM((ROWS, D_MODEL), BF16),
            pltpu.VMEM((1, LRU_WIDTH), F32),
            pltpu.VMEM((SUBLANES, LRU_WIDTH), F32),
            pltpu.VMEM((SUBLANES, SSD_CONV_DIM), F32),
            pltpu.VMEM((SSD_WIDTH, SSD_STATE), F32),
        ] + _mixer_scratch(1),
        compiler_params=_cparams(2),
        name="mixer_prompt",
    )(x1, x1, mix_g, w_in_a, proj_b, *weights)


def _mixer_sample(proj_a, proj_b, lprev, sprev, h0, s_in, weights, *, batch, seq_len, row_block_offset,
                  n_inner=4):
    n_seq = ROWS // seq_len
    n_outer = batch // n_seq
    sps = n_seq // n_inner
    return pl.pallas_call(
        functools.partial(_mixer_sample_kernel, n_inner),
        grid=(n_outer, n_inner),
        in_specs=[
            pl.BlockSpec((ROWS, D_A), lambda o, i: (o, 0)),
            pl.BlockSpec((ROWS, D_B_PAD), lambda o, i: (row_block_offset + o, 0)),
            pl.BlockSpec((ROWS, LRU_WIDTH), lambda o, i: (o, 0)),
            pl.BlockSpec((ROWS, SSD_CONV_DIM), lambda o, i: (o, 0)),
            pl.BlockSpec((n_seq, LRU_WIDTH), lambda o, i: (o, 0)),
            pl.BlockSpec((sps, SSD_WIDTH, SSD_STATE), lambda o, i: (o * n_inner + i, 0, 0)),
        ] + _mixer_weight_specs(2),
        out_specs=[
            pl.BlockSpec((ROWS, LRU_WIDTH + SSD_WIDTH), lambda o, i: (o, 0)),
            pl.BlockSpec((n_seq, CONV_WIDTH - 1, LRU_WIDTH), lambda o, i: (o, 0, 0)),
            pl.BlockSpec((n_seq, LRU_WIDTH), lambda o, i: (o, 0)),
            pl.BlockSpec((n_seq, CONV_WIDTH - 1, SSD_CONV_DIM), lambda o, i: (o, 0, 0)),
            pl.BlockSpec((sps, SSD_WIDTH, SSD_STATE), lambda o, i: (o * n_inner + i, 0, 0)),
        ],
        out_shape=[
            jax.ShapeDtypeStruct((batch * seq_len, LRU_WIDTH + SSD_WIDTH), BF16),
            jax.ShapeDtypeStruct((batch, CONV_WIDTH - 1, LRU_WIDTH), F32),
            jax.ShapeDtypeStruct((batch, LRU_WIDTH), F32),
            jax.ShapeDtypeStruct((batch, CONV_WIDTH - 1, SSD_CONV_DIM), F32),
            jax.ShapeDtypeStruct((batch, SSD_WIDTH, SSD_STATE), F32),
        ],
        scratch_shapes=_mixer_scratch(n_seq),
        compiler_params=_cparams(2),
        name="mixer_sample",
    )(proj_a, proj_b, lprev, sprev, h0, s_in, *weights)


def _row(v):
    return v.reshape(1, -1).astype(F32)


def _pad_lanes(v):
    v = v.reshape(1, -1).astype(F32)
    return jnp.pad(v, ((0, 0), (0, LANES - v.shape[1])))


def kernel(x_prompt, mem_prompt, x_sample, cache_mem_k, cache_mem_v, state_lru_conv, state_lru_h, state_ssd_conv, state_ssd_h, ffn1_norm_g, ffn1_w_gate, ffn1_w_up, ffn1_w_down, mix_norm_g, w_in, lru_conv_w, lru_conv_b, lru_w_a, lru_b_a, lru_w_x, lru_b_x, lru_lambda, lru_out_norm_g, ssd_conv_w, ssd_conv_b, ssd_dt_bias, ssd_a_log, ssd_d, ssd_out_norm_g, w_out, xattn_norm_g, mem_norm_g, xattn_w_q, xattn_w_k, xattn_w_v, xattn_w_o, ffn2_norm_g, ffn2_w_gate, ffn2_w_up, ffn2_w_down, final_norm_g):
    depth = ffn1_norm_g.shape[0]
    assert depth == 1
    bp, tp, d = x_prompt.shape
    bs, ts, _ = x_sample.shape
    mp = bp * tp
    ms = bs * ts
    assert tp % ROWS == 0 and ROWS % ts == 0 and ts == SUBLANES and mp % ROWS == 0
    l = 0

    w_in_a = w_in[l][:, :D_A].astype(BF16).reshape(d, D_A // W_A_TILE, W_A_TILE).transpose(1, 0, 2)
    w_in_b = jnp.pad(w_in[l][:, D_A:].astype(BF16), ((0, 0), (0, D_B_PAD - D_B)))
    mixer_weights = (
        lru_conv_w[l], _row(lru_conv_b[l]), lru_w_a[l].astype(BF16), lru_w_x[l].astype(BF16),
        _row(lru_b_a[l]), _row(lru_b_x[l]), _row(lru_lambda[l]), _row(lru_out_norm_g[l]),
        ssd_conv_w[l], _row(ssd_conv_b[l]), _pad_lanes(ssd_dt_bias[l]), _pad_lanes(ssd_a_log[l]),
        _row(jnp.repeat(ssd_d[l], SSD_HEADDIM)), _row(ssd_out_norm_g[l]),
    )

    (x1,) = _ffn((x_prompt.reshape(mp, d), x_sample.reshape(ms, d)), _row(ffn1_norm_g[l]),
                 ffn1_w_gate[l].astype(BF16), ffn1_w_up[l].astype(BF16), ffn1_w_down[l].astype(BF16),
                 _row(final_norm_g), out_rows=(mp + ms,), final_norm=False)
    mix_g = _row(mix_norm_g[l])
    proj_b = _norm_matmul(x1, mix_g, w_in_b, tm=1024, tn=1024, out_dtype=F32, name="in_proj_b")
    proj_a_s = _norm_matmul(x1, mix_g, w_in[l], tm=1024, tn=1024, out_dtype=F32, name="in_proj_a_sample",
                            row_offset=mp, n_cols=D_A)

    mixed_p, p_lc, p_lh, p_sc, p_sh = _mixer_prompt(x1, mix_g, w_in_a, proj_b, mixer_weights,
                                                    batch=bp, seq_len=tp)

    pad_rows = ((0, 0), (SUBLANES - (CONV_WIDTH - 1), 0), (0, 0))
    lprev = jnp.pad(state_lru_conv[l], pad_rows).reshape(ms, LRU_WIDTH)
    sprev = jnp.pad(state_ssd_conv[l], pad_rows).reshape(ms, SSD_CONV_DIM)
    mixed_s, s_lc, s_lh, s_sc, s_sh = _mixer_sample(
        proj_a_s, proj_b, lprev, sprev, state_lru_h[l], state_ssd_h[l].reshape(bs, SSD_WIDTH, SSD_STATE),
        mixer_weights, batch=bs, seq_len=ts, row_block_offset=mp // ROWS)

    x2 = _matmul_residual(mixed_p, mixed_s, w_out[l], x1, tm=1024, tn=512, name="out_proj")

    q = _norm_matmul(x2, _row(xattn_norm_g[l]), xattn_w_q[l], tm=1024, tn=1024,
                     out_dtype=BF16, name="q_proj")
    mem = mem_prompt.reshape(bp * N_MEM, d)
    mk = _norm_matmul(mem, _row(mem_norm_g[l]), xattn_w_k[l], tm=1024, tn=1024,
                      out_dtype=F32, name="mem_k")
    mv = _norm_matmul(mem, _row(mem_norm_g[l]), xattn_w_v[l], tm=1024, tn=1024,
                      out_dtype=F32, name="mem_v")
    o_p = _xattn_prompt(q, mk.reshape(bp, N_MEM, d), mv.reshape(bp, N_MEM, d), seq_len=tp, tq=512)
    n_seq_x = 4
    o_s = _xattn_sample(q, cache_mem_k[l], cache_mem_v[l],
                        seq_len=ts, n_seq=n_seq_x, row_block_offset=mp // (n_seq_x * ts))
    x3 = _matmul_residual(o_p, o_s, xattn_w_o[l], x2, tm=1024, tn=1024, name="o_proj")

    y_p, y_s = _ffn((x3,), _row(ffn2_norm_g[l]), ffn2_w_gate[l].astype(BF16), ffn2_w_up[l].astype(BF16),
                    ffn2_w_down[l].astype(BF16), _row(final_norm_g), out_rows=(mp, ms), final_norm=True)

    y_prompt = y_p.reshape(bp, tp, d)
    y_sample = y_s.reshape(bs, ts, d)
    hshape = (SSD_HEADS, SSD_HEADDIM, SSD_STATE)
    return (y_prompt, y_sample,
            p_lc[None], p_lh.reshape(1, bp, LRU_WIDTH), p_sc[None], p_sh.reshape((1, bp) + hshape),
            mk.reshape(1, bp, N_MEM, XATTN_HEADS, XATTN_HEAD_DIM),
            mv.reshape(1, bp, N_MEM, XATTN_HEADS, XATTN_HEAD_DIM),
            s_lc[None], s_lh[None], s_sc[None], s_sh.reshape((1, bs) + hshape))
```

```python
import functools

import jax
import jax.numpy as jnp
from jax import lax
from jax.experimental import pallas as pl
from jax.experimental.pallas import tpu as pltpu

F32 = jnp.float32
BF16 = jnp.bfloat16

D_MODEL = 2048
LRU_WIDTH = 2048
LRU_HEADS = 16
LRU_C = 8.0
CONV_WIDTH = 4
SSD_WIDTH = 2048
SSD_HEADDIM = 64
SSD_HEADS = 32
SSD_GROUPS = 2
SSD_STATE = 128
SSD_CONV_DIM = SSD_WIDTH + 2 * SSD_GROUPS * SSD_STATE
D_IN = 3 * 2048 + SSD_CONV_DIM + SSD_HEADS
N_MEM = 256
XATTN_HEADS = 4
XATTN_HEAD_DIM = 512
EPS = 1e-6

LANES = 128
SUBLANES = 8
VMEM_LIMIT_BYTES = 56 * 1024 * 1024

D_A = 3 * 2048
W_A_TILE = 512
COL_XLRU = 0
COL_GLRU = 2048
COL_Z = 4096
D_B = SSD_CONV_DIM + SSD_HEADS
D_B_PAD = 3072
COL_XBC = 0
COL_DT = SSD_CONV_DIM
ROWS = 128
GROUP_COLS = SSD_WIDTH // SSD_GROUPS


def _cparams(n_axes):
    return pltpu.CompilerParams(
        dimension_semantics=("arbitrary",) * n_axes,
        vmem_limit_bytes=VMEM_LIMIT_BYTES)


def _rmsnorm_rows(x, g):
    ms = jnp.mean(x * x, axis=-1, keepdims=True)
    return (x * lax.rsqrt(ms + EPS)) * g


def _softplus(x):
    return jnp.maximum(x, 0.0) + jnp.log1p(jnp.exp(-jnp.abs(x)))


def _silu(x):
    return x * jax.nn.sigmoid(x)


def _when_rows(i, n_a, fn_a, fn_b, extra=None):
    in_a = i < n_a
    in_b = i >= n_a
    if extra is not None:
        in_a = jnp.logical_and(in_a, extra)
        in_b = jnp.logical_and(in_b, extra)
    pl.when(in_a)(fn_a)
    pl.when(in_b)(fn_b)


def _ffn_kernel(n_a, split_in, split_out, final_norm, *refs):
    refs = list(refs)
    xa_ref = refs.pop(0)
    xb_ref = refs.pop(0) if split_in else xa_ref
    g_ref, wg_ref, wu_ref, wd_ref, gf_ref = refs[:5]
    oa_ref = refs[5]
    ob_ref = refs[6] if split_out else oa_ref
    h_scr, acc_scr = refs[-2:]
    i = pl.program_id(0)
    j = pl.program_id(1)

    def prologue(x_ref):
        def run():
            x = x_ref[...]
            h_scr[...] = _rmsnorm_rows(x, g_ref[...]).astype(BF16)
            acc_scr[...] = x
        return run

    if split_in:
        _when_rows(i, n_a, prologue(xa_ref), prologue(xb_ref), extra=(j == 0))
    else:
        pl.when(j == 0)(prologue(xa_ref))

    h = h_scr[...]
    gate = jnp.dot(h, wg_ref[...], preferred_element_type=F32)
    up = jnp.dot(h, wu_ref[...], preferred_element_type=F32)
    a = (0.5 * _silu(gate) * up).astype(BF16)
    acc_scr[...] += jnp.dot(a, wd_ref[...], preferred_element_type=F32)

    def epilogue(o_ref):
        def run():
            res = acc_scr[...]
            if final_norm:
                res = _rmsnorm_rows(res, gf_ref[...])
            o_ref[...] = res
        return run

    last = j == pl.num_programs(1) - 1
    if split_out:
        _when_rows(i, n_a, epilogue(oa_ref), epilogue(ob_ref), extra=last)
    else:
        pl.when(last)(epilogue(oa_ref))


def _ffn(xs, g, wg, wu, wd, gf, *, out_rows, final_norm, tm=512, tf=512):
    split_in = len(xs) == 2
    split_out = len(out_rows) == 2
    d = xs[0].shape[1]
    f = wg.shape[1]
    m = sum(x.shape[0] for x in xs)
    assert m == sum(out_rows) and all(x.shape[0] % tm == 0 for x in xs) and all(r % tm == 0 for r in out_rows)
    n_a = (xs[0].shape[0] if split_in else out_rows[0]) // tm
    if split_in and split_out:
        assert xs[0].shape[0] == out_rows[0]

    def first(i, j):
        return (jnp.minimum(i, n_a - 1), 0)

    def second(i, j):
        return (jnp.maximum(i - n_a, 0), 0)

    def whole(i, j):
        return (i, 0)

    x_specs = ([pl.BlockSpec((tm, d), first), pl.BlockSpec((tm, d), second)] if split_in
               else [pl.BlockSpec((tm, d), whole)])
    o_specs = ([pl.BlockSpec((tm, d), first), pl.BlockSpec((tm, d), second)] if split_out
               else [pl.BlockSpec((tm, d), whole)])
    return pl.pallas_call(
        functools.partial(_ffn_kernel, n_a, split_in, split_out, final_norm),
        grid=(m // tm, f // tf),
        in_specs=x_specs + [
            pl.BlockSpec((1, d), lambda i, j: (0, 0)),
            pl.BlockSpec((d, tf), lambda i, j: (0, j)),
            pl.BlockSpec((d, tf), lambda i, j: (0, j)),
            pl.BlockSpec((tf, d), lambda i, j: (j, 0)),
            pl.BlockSpec((1, d), lambda i, j: (0, 0)),
        ],
        out_specs=o_specs,
        out_shape=[jax.ShapeDtypeStruct((r, d), F32) for r in out_rows],
        scratch_shapes=[pltpu.VMEM((tm, d), BF16), pltpu.VMEM((tm, d), F32)],
        compiler_params=_cparams(2),
        name="ffn_final" if final_norm else "ffn",
    )(*xs, g, wg, wu, wd, gf)


def _norm_matmul_kernel(x_ref, g_ref, w_ref, o_ref, h_scr):
    @pl.when(pl.program_id(1) == 0)
    def _():
        h_scr[...] = _rmsnorm_rows(x_ref[...], g_ref[...]).astype(BF16)

    o_ref[...] = jnp.dot(h_scr[...], w_ref[...], preferred_element_type=F32).astype(o_ref.dtype)


def _norm_matmul(x, g, w, *, tm, tn, out_dtype, name, row_offset=0):
    k = x.shape[1]
    m = x.shape[0] - row_offset
    n = w.shape[1]
    assert m % tm == 0 and row_offset % tm == 0 and n % tn == 0
    first = row_offset // tm
    return pl.pallas_call(
        _norm_matmul_kernel,
        grid=(m // tm, n // tn),
        in_specs=[
            pl.BlockSpec((tm, k), lambda i, j: (first + i, 0)),
            pl.BlockSpec((1, k), lambda i, j: (0, 0)),
            pl.BlockSpec((k, tn), lambda i, j: (0, j)),
        ],
        out_specs=pl.BlockSpec((tm, tn), lambda i, j: (i, j)),
        out_shape=jax.ShapeDtypeStruct((m, n), out_dtype),
        scratch_shapes=[pltpu.VMEM((tm, k), BF16)],
        compiler_params=_cparams(2),
        name=name,
    )(x, g, w)


def _matmul_residual_kernel(n_a, aa_ref, ab_ref, w_ref, r_ref, o_ref):
    def run(a_ref):
        def body():
            o_ref[...] = r_ref[...] + jnp.dot(a_ref[...], w_ref[...], preferred_element_type=F32)
        return body

    _when_rows(pl.program_id(0), n_a, run(aa_ref), run(ab_ref))


def _matmul_residual(a_first, a_second, w, res, *, tm, tn, name):
    k = a_first.shape[1]
    m = a_first.shape[0] + a_second.shape[0]
    n = w.shape[1]
    assert a_first.shape[0] % tm == 0 and a_second.shape[0] % tm == 0 and res.shape == (m, n)
    n_a = a_first.shape[0] // tm
    return pl.pallas_call(
        functools.partial(_matmul_residual_kernel, n_a),
        grid=(m // tm, n // tn),
        in_specs=[
            pl.BlockSpec((tm, k), lambda i, j: (jnp.minimum(i, n_a - 1), 0)),
            pl.BlockSpec((tm, k), lambda i, j: (jnp.maximum(i - n_a, 0), 0)),
            pl.BlockSpec((k, tn), lambda i, j: (0, j)),
            pl.BlockSpec((tm, tn), lambda i, j: (i, j)),
        ],
        out_specs=pl.BlockSpec((tm, tn), lambda i, j: (i, j)),
        out_shape=jax.ShapeDtypeStruct((m, n), F32),
        compiler_params=_cparams(2),
        name=name,
    )(a_first, a_second, w, res)


def _xattn_kernel(n_seq, tq, q_ref, k_ref, v_ref, o_ref):
    scale = XATTN_HEAD_DIM ** -0.5
    for s in range(n_seq):
        rows = slice(s * tq, (s + 1) * tq)
        for h in range(XATTN_HEADS):
            cols = slice(h * XATTN_HEAD_DIM, (h + 1) * XATTN_HEAD_DIM)
            q = q_ref[rows, cols]
            k = k_ref[s, :, cols].astype(BF16)
            v = v_ref[s, :, cols].astype(BF16)
            sc = lax.dot_general(q, k, (((1,), (1,)), ((), ())),
                                 preferred_element_type=F32) * scale
            mx = jnp.max(sc, axis=-1, keepdims=True)
            e = jnp.exp(sc - mx)
            p = e / jnp.sum(e, axis=-1, keepdims=True)
            o = jnp.dot(p.astype(BF16), v, preferred_element_type=F32)
            o_ref[rows, cols] = o.astype(BF16)


def _xattn_prompt(q, k, v, *, seq_len, tq):
    b = k.shape[0]
    nt = seq_len // tq
    d = q.shape[1]
    return pl.pallas_call(
        functools.partial(_xattn_kernel, 1, tq),
        grid=(b, nt),
        in_specs=[
            pl.BlockSpec((tq, d), lambda i, j: (i * nt + j, 0)),
            pl.BlockSpec((1, N_MEM, d), lambda i, j: (i, 0, 0)),
            pl.BlockSpec((1, N_MEM, d), lambda i, j: (i, 0, 0)),
        ],
        out_specs=pl.BlockSpec((tq, d), lambda i, j: (i * nt + j, 0)),
        out_shape=jax.ShapeDtypeStruct((b * seq_len, d), BF16),
        compiler_params=_cparams(2),
        name="xattn_prompt",
    )(q, k, v)


def _xattn_cache_kernel(n_seq, tq, q_ref, k_ref, v_ref, o_ref):
    scale = XATTN_HEAD_DIM ** -0.5
    n_rows = XATTN_HEADS * tq
    n_cols = N_MEM * XATTN_HEADS
    own = (_lane_iota((n_rows, n_cols)) % XATTN_HEADS) == (_row_iota((n_rows, n_cols)) // tq)
    qf = q_ref[...].astype(F32)
    outs = []
    for s in range(n_seq):
        qs = qf[s * tq:(s + 1) * tq]
        q4 = jnp.concatenate(
            [qs[:, h * XATTN_HEAD_DIM:(h + 1) * XATTN_HEAD_DIM] for h in range(XATTN_HEADS)],
            axis=0).astype(BF16)
        kf = k_ref[s].reshape(n_cols, XATTN_HEAD_DIM).astype(BF16)
        vf = v_ref[s].reshape(n_cols, XATTN_HEAD_DIM).astype(BF16)
        sc = lax.dot_general(q4, kf, (((1,), (1,)), ((), ())), preferred_element_type=F32) * scale
        sc = jnp.where(own, sc, -1e30)
        mx = jnp.max(sc, axis=-1, keepdims=True)
        e = jnp.exp(sc - mx)
        p = e / jnp.sum(e, axis=-1, keepdims=True)
        o4 = jnp.dot(p.astype(BF16), vf, preferred_element_type=F32)
        outs.append(jnp.concatenate([o4[h * tq:(h + 1) * tq] for h in range(XATTN_HEADS)], axis=1))
    o_ref[...] = jnp.concatenate(outs, axis=0).astype(BF16)


def _xattn_sample(q, k, v, *, seq_len, n_seq, row_block_offset):
    b = k.shape[0]
    d = q.shape[1]
    rows = n_seq * seq_len
    kv_block = (n_seq, N_MEM, XATTN_HEADS, XATTN_HEAD_DIM)
    return pl.pallas_call(
        functools.partial(_xattn_cache_kernel, n_seq, seq_len),
        grid=(b // n_seq,),
        in_specs=[
            pl.BlockSpec((rows, d), lambda i: (row_block_offset + i, 0)),
            pl.BlockSpec(kv_block, lambda i: (i, 0, 0, 0)),
            pl.BlockSpec(kv_block, lambda i: (i, 0, 0, 0)),
        ],
        out_specs=pl.BlockSpec((rows, d), lambda i: (i, 0)),
        out_shape=jax.ShapeDtypeStruct((b * seq_len, d), BF16),
        compiler_params=_cparams(1),
        name="xattn_sample",
    )(q, k, v)


def _row_iota(shape):
    return lax.broadcasted_iota(jnp.int32, shape, 0)


def _lane_iota(shape):
    return lax.broadcasted_iota(jnp.int32, shape, 1)


def _seg_cumsum(x, seg_len):
    pos = _row_iota(x.shape) % seg_len
    s = 1
    while s < seg_len:
        x = x + jnp.where(pos >= s, pltpu.roll(x, s, axis=0), 0.0)
        s *= 2
    return x


def _seg_rev_excl_cumsum(x, seg_len):
    n = x.shape[0]
    pos = _row_iota(x.shape) % seg_len
    y = jnp.where(pos < seg_len - 1, pltpu.roll(x, n - 1, axis=0), 0.0)
    s = 1
    while s < seg_len:
        y = y + jnp.where(pos < seg_len - s, pltpu.roll(y, n - s, axis=0), 0.0)
        s *= 2
    return y


def _scan8(a, b):
    shape = a.shape
    tiled = (shape[0] // SUBLANES, SUBLANES, shape[1])
    a = a.reshape(tiled)
    b = b.reshape(tiled)
    pos = lax.broadcasted_iota(jnp.int32, tiled, 1)
    for s in (1, 2, 4):
        m = pos >= s
        a_sh = pltpu.roll(a, s, axis=1)
        b_sh = pltpu.roll(b, s, axis=1)
        b = jnp.where(m, a * b_sh + b, b)
        a = jnp.where(m, a * a_sh, a)
    return a.reshape(shape), b.reshape(shape)


def _conv_taps(x, w_ref, b_ref, cols, fix):
    acc = b_ref[:, cols] + x * w_ref[CONV_WIDTH - 1:CONV_WIDTH, cols]
    for s in range(1, CONV_WIDTH):
        xs = fix(s, pltpu.roll(x, s, axis=0))
        k = CONV_WIDTH - 1 - s
        acc = acc + xs * w_ref[k:k + 1, cols]
    return acc


def _bcast_col(v, c):
    return jnp.broadcast_to(v[:, c:c + 1], (v.shape[0], LANES))


def _lru_part(sample, pa_ref, prev_fix_lru, h0_rows, hcar_ref,
              lcw_ref, lcb_ref, wa_ref, wx_ref, ba_ref, bx_ref, lam_ref, lg_ref,
              mixed_ref, hs_scr, yl_scr):
    ssq = jnp.zeros((ROWS, 1), F32)
    for hd in range(LRU_HEADS):
        cols = slice(hd * LANES, (hd + 1) * LANES)
        xl = pa_ref[:, COL_XLRU + hd * LANES:COL_XLRU + (hd + 1) * LANES]
        u = _conv_taps(xl, lcw_ref, lcb_ref, cols, functools.partial(prev_fix_lru, cols))
        ub = u.astype(BF16)
        r = jax.nn.sigmoid(jnp.dot(ub, wa_ref[hd], preferred_element_type=F32) + ba_ref[:, cols])
        i = jax.nn.sigmoid(jnp.dot(ub, wx_ref[hd], preferred_element_type=F32) + bx_ref[:, cols])
        log_a = (-LRU_C * r) * _softplus(-lam_ref[:, cols])
        a = jnp.exp(log_a)
        th = jnp.tanh(log_a)
        beta = jnp.sqrt((-2.0 * th) / (1.0 - th))
        bb = (beta * i) * u
        a_cum, b_cum = _scan8(a, bb)
        if sample:
            h = a_cum * h0_rows(cols) + b_cum
        else:
            carry = hcar_ref[:, cols]
            parts = []
            for t in range(ROWS // SUBLANES):
                rs = slice(t * SUBLANES, (t + 1) * SUBLANES)
                ht = a_cum[rs] * carry + b_cum[rs]
                parts.append(ht)
                carry = ht[SUBLANES - 1:SUBLANES]
            h = jnp.concatenate(parts, axis=0)
            hcar_ref[:, cols] = carry
        hs_scr[:, cols] = h
        g = jax.nn.gelu(pa_ref[:, COL_GLRU + hd * LANES:COL_GLRU + (hd + 1) * LANES])
        yl = h * g
        ssq = ssq + jnp.sum(yl * yl, axis=-1, keepdims=True)
        yl_scr[:, cols] = yl
    inv = lax.rsqrt(ssq * (1.0 / LRU_WIDTH) + EPS)
    for hd in range(LRU_HEADS):
        cols = slice(hd * LANES, (hd + 1) * LANES)
        mixed_ref[:, cols] = ((yl_scr[:, cols] * inv) * lg_ref[:, cols]).astype(BF16)


def _ssd_front(sample, pb_ref, prev_fix_ssd, scw_ref, scb_ref, dtb_ref, alog_ref, dexp_ref,
               xc_scr, xw_scr, ecs_scr, y_scr, xt_scr, dec_scr, c_scr, b_scr, side_work=()):
    seg = SUBLANES if sample else ROWS
    n_seq = ROWS // seg
    n_conv_tiles = SSD_CONV_DIM // LANES
    n_pairs = SSD_HEADS // 2
    pending = list(side_work)
    stride = max(1, (n_conv_tiles + n_pairs) // max(1, len(pending)))
    point = [0]

    def side_point():
        if pending and point[0] % stride == 0:
            pending.pop(0)()
        point[0] += 1

    for t in range(n_conv_tiles):
        side_point()
        cols = slice(t * LANES, (t + 1) * LANES)
        xb = pb_ref[:, COL_XBC + t * LANES:COL_XBC + (t + 1) * LANES]
        v = _conv_taps(xb, scw_ref, scb_ref, cols, functools.partial(prev_fix_ssd, cols))
        xc_scr[:, cols] = _silu(v)

    dt = _softplus(pb_ref[:, COL_DT:COL_DT + LANES] + dtb_ref[...])
    a_neg = -jnp.exp(alog_ref[...])
    d_a = dt * a_neg
    cs = _seg_cumsum(d_a, seg)
    rcs = _seg_rev_excl_cumsum(d_a, seg)
    ecs = jnp.exp(cs)
    todt = jnp.exp(rcs) * dt
    dec_tot = jnp.exp(cs + rcs)
    cs_t = cs.T
    dt_t = dt.T
    dec_t = dec_tot.T
    for s in range(n_seq):
        dec_scr[s] = jnp.broadcast_to(dec_t[0:SSD_HEADS, s * seg:s * seg + 1], (SSD_HEADS, LANES))

    ri = _row_iota((ROWS, ROWS))
    ci = _lane_iota((ROWS, ROWS))
    causal = ri >= ci
    if sample:
        causal = jnp.logical_and(causal, (ri // seg) == (ci // seg))
    lo = _lane_iota((ROWS, LANES)) < SSD_HEADDIM

    b_off = SSD_WIDTH
    c_off = SSD_WIDTH + SSD_GROUPS * SSD_STATE
    for g in range(SSD_GROUPS):
        gcols = slice(g * SSD_STATE, (g + 1) * SSD_STATE)
        bg = xc_scr[:, b_off + g * SSD_STATE:b_off + (g + 1) * SSD_STATE].astype(BF16)
        cg = xc_scr[:, c_off + g * SSD_STATE:c_off + (g + 1) * SSD_STATE].astype(BF16)
        c_scr[:, gcols] = xc_scr[:, c_off + g * SSD_STATE:c_off + (g + 1) * SSD_STATE]
        b_scr[:, gcols] = bg
        cb = lax.dot_general(cg, bg, (((1,), (1,)), ((), ())), preferred_element_type=F32)
        heads_per_group = SSD_HEADS // SSD_GROUPS
        for jp in range(heads_per_group // 2):
            side_point()
            h0 = g * heads_per_group + 2 * jp
            h1 = h0 + 1
            cols = slice(h0 * SSD_HEADDIM, h0 * SSD_HEADDIM + LANES)
            x_pair = xc_scr[:, cols]
            ws = []
            for hh in (h0, h1):
                diff = _bcast_col(cs, hh) - cs_t[hh:hh + 1, :]
                dec = jnp.where(causal, jnp.exp(diff), 0.0)
                ws.append(((cb * dec) * dt_t[hh:hh + 1, :]).astype(BF16))
            lhs = jnp.concatenate(ws, axis=1)
            rhs = jnp.concatenate([jnp.where(lo, x_pair, 0.0), jnp.where(lo, 0.0, x_pair)],
                                  axis=0).astype(BF16)
            y_diag = jnp.dot(lhs, rhs, preferred_element_type=F32)
            y_scr[:, cols] = y_diag + x_pair * dexp_ref[:, cols]
            xw_scr[:, cols] = x_pair * jnp.where(lo, _bcast_col(todt, h0), _bcast_col(todt, h1))
            ecs_scr[:, cols] = jnp.where(lo, _bcast_col(ecs, h0), _bcast_col(ecs, h1))
    while pending:
        pending.pop(0)()
    for t in range(SSD_WIDTH // LANES):
        xt_scr[t * LANES:(t + 1) * LANES, :] = xw_scr[:, t * LANES:(t + 1) * LANES].T.astype(BF16)


def _ssd_state_step(c_rows, b_mask, s_old, dec, xt_scr):
    y_parts = []
    s_parts = []
    heads_per_group = SSD_HEADS // SSD_GROUPS
    for g in range(SSD_GROUPS):
        gcols = slice(g * SSD_STATE, (g + 1) * SSD_STATE)
        grows = slice(g * GROUP_COLS, (g + 1) * GROUP_COLS)
        sg = s_old[grows]
        y_parts.append(lax.dot_general(c_rows[:, gcols], sg.astype(BF16), (((1,), (1,)), ((), ())),
                                       preferred_element_type=F32))
        upd = jnp.dot(xt_scr[grows, :], b_mask[:, gcols], preferred_element_type=F32)
        for hh in range(heads_per_group):
            hr = slice(hh * SSD_HEADDIM, (hh + 1) * SSD_HEADDIM)
            habs = g * heads_per_group + hh
            s_parts.append(sg[hr] * dec[habs:habs + 1, :] + upd[hr])
    return jnp.concatenate(y_parts, axis=1), jnp.concatenate(s_parts, axis=0)


def _ssd_finish(pa_ref, y_scr, sg_ref, mixed_ref):
    for g in range(SSD_GROUPS):
        ssq = jnp.zeros((ROWS, 1), F32)
        n_t = GROUP_COLS // LANES
        for t in range(n_t):
            c0 = g * GROUP_COLS + t * LANES
            yz = y_scr[:, c0:c0 + LANES] * _silu(pa_ref[:, COL_Z + c0:COL_Z + c0 + LANES])
            y_scr[:, c0:c0 + LANES] = yz
            ssq = ssq + jnp.sum(yz * yz, axis=-1, keepdims=True)
        inv = lax.rsqrt(ssq * (1.0 / GROUP_COLS) + EPS)
        for t in range(n_t):
            c0 = g * GROUP_COLS + t * LANES
            mixed_ref[:, LRU_WIDTH + c0:LRU_WIDTH + c0 + LANES] = (
                (y_scr[:, c0:c0 + LANES] * inv) * sg_ref[:, c0:c0 + LANES]).astype(BF16)


def _mixer_prompt_kernel(x_ref, xn_ref, mg_ref, *rest):
    n_slabs = D_A // W_A_TILE
    wina_refs = rest[:n_slabs]
    (pb_ref,
     lcw_ref, lcb_ref, wa_ref, wx_ref, ba_ref, bx_ref, lam_ref, lg_ref,
     scw_ref, scb_ref, dtb_ref, alog_ref, dexp_ref, sg_ref,
     mixed_ref, lconv_ref, lh_ref, sconv_ref, sh_ref,
     pa_scr, hn_scr, hcar_ref, ltail_scr, stail_scr, s_scr,
     hs_scr, yl_scr, xc_scr, xw_scr, ecs_scr, y_scr, xt_scr, dec_scr, c_scr, b_scr) = rest[n_slabs:]
    c = pl.program_id(1)
    step = pl.program_id(0) * pl.num_programs(1) + c
    slot = step % 2

    @pl.when(c == 0)
    def _():
        hcar_ref[...] = jnp.zeros_like(hcar_ref)
        ltail_scr[...] = jnp.zeros_like(ltail_scr)
        stail_scr[...] = jnp.zeros_like(stail_scr)
        s_scr[...] = jnp.zeros_like(s_scr)

    @pl.when(step == 0)
    def _():
        hn = _rmsnorm_rows(x_ref[...], mg_ref[...]).astype(BF16)
        for j in range(n_slabs):
            pa_scr[0, :, j * W_A_TILE:(j + 1) * W_A_TILE] = jnp.dot(
                hn, wina_refs[j][...], preferred_element_type=F32)

    hn_scr[...] = _rmsnorm_rows(xn_ref[...], mg_ref[...]).astype(BF16)
    pa_next = pa_scr.at[1 - slot]
    pa_cur = pa_scr.at[slot]

    def project_slab(j):
        def run():
            pa_next[:, j * W_A_TILE:(j + 1) * W_A_TILE] = jnp.dot(
                hn_scr[...], wina_refs[j][...], preferred_element_type=F32)
        return run

    side_work = [project_slab(j) for j in range(n_slabs)]

    pos8 = _row_iota((SUBLANES, LANES))

    def make_fix(tail_scr):
        def fix(cols, s, rolled):
            first = jnp.where(pos8 < s, pltpu.roll(tail_scr[:, cols], s, axis=0), rolled[0:SUBLANES])
            return jnp.concatenate([first, rolled[SUBLANES:]], axis=0)
        return fix

    _ssd_front(False, pb_ref, make_fix(stail_scr), scw_ref, scb_ref, dtb_ref, alog_ref, dexp_ref,
               xc_scr, xw_scr, ecs_scr, y_scr, xt_scr, dec_scr, c_scr, b_scr, side_work=side_work)
    _lru_part(False, pa_cur, make_fix(ltail_scr), None, hcar_ref,
              lcw_ref, lcb_ref, wa_ref, wx_ref, ba_ref, bx_ref, lam_ref, lg_ref,
              mixed_ref, hs_scr, yl_scr)

    ltail_scr[...] = pa_cur[ROWS - SUBLANES:ROWS, COL_XLRU:COL_XLRU + LRU_WIDTH]
    stail_scr[...] = pb_ref[ROWS - SUBLANES:ROWS, COL_XBC:COL_XBC + SSD_CONV_DIM]

    y_off, s_new = _ssd_state_step(c_scr[...].astype(BF16), b_scr[...], s_scr[...], dec_scr[0], xt_scr)
    s_scr[...] = s_new
    for t in range(SSD_WIDTH // LANES):
        cols = slice(t * LANES, (t + 1) * LANES)
        y_scr[:, cols] = y_scr[:, cols] + ecs_scr[:, cols] * y_off[:, cols]
    _ssd_finish(pa_cur, y_scr, sg_ref, mixed_ref)

    @pl.when(c == pl.num_programs(1) - 1)
    def _():
        lconv_ref[0] = pa_cur[ROWS - (CONV_WIDTH - 1):ROWS, COL_XLRU:COL_XLRU + LRU_WIDTH]
        sconv_ref[0] = pb_ref[ROWS - (CONV_WIDTH - 1):ROWS, COL_XBC:COL_XBC + SSD_CONV_DIM]
        lh_ref[0] = hcar_ref[...]
        sh_ref[0] = s_scr[...]


def _mixer_sample_kernel(n_inner, pa_ref, pb_ref, lprev_ref, sprev_ref, h0_ref, sin_ref,
                         lcw_ref, lcb_ref, wa_ref, wx_ref, ba_ref, bx_ref, lam_ref, lg_ref,
                         scw_ref, scb_ref, dtb_ref, alog_ref, dexp_ref, sg_ref,
                         mixed_ref, lconv_ref, lh_ref, sconv_ref, sh_ref,
                         hs_scr, yl_scr, xc_scr, xw_scr, ecs_scr, y_scr, xt_scr, dec_scr, c_scr, b_scr):
    i = pl.program_id(1)
    seg = SUBLANES
    n_seq = ROWS // seg
    seq_per_step = n_seq // n_inner

    @pl.when(i == 0)
    def _():
        pos = _row_iota((ROWS, LANES)) % seg

        def make_fix(prev_ref):
            def fix(cols, s, rolled):
                return jnp.where(pos < s, pltpu.roll(prev_ref[:, cols], ROWS - seg + s, axis=0), rolled)
            return fix

        def h0_rows(cols):
            return jnp.concatenate(
                [jnp.broadcast_to(h0_ref[q:q + 1, cols], (seg, LANES)) for q in range(n_seq)], axis=0)

        _ssd_front(True, pb_ref, make_fix(sprev_ref), scw_ref, scb_ref, dtb_ref, alog_ref, dexp_ref,
                   xc_scr, xw_scr, ecs_scr, y_scr, xt_scr, dec_scr, c_scr, b_scr)
        _lru_part(True, pa_ref, make_fix(lprev_ref), h0_rows, None,
                  lcw_ref, lcb_ref, wa_ref, wx_ref, ba_ref, bx_ref, lam_ref, lg_ref,
                  mixed_ref, hs_scr, yl_scr)
        for sq in range(n_seq):
            tail = slice((sq + 1) * seg - (CONV_WIDTH - 1), (sq + 1) * seg)
            lconv_ref[sq] = pa_ref[tail, COL_XLRU:COL_XLRU + LRU_WIDTH]
            sconv_ref[sq] = pb_ref[tail, COL_XBC:COL_XBC + SSD_CONV_DIM]
            lh_ref[sq:sq + 1, :] = hs_scr[(sq + 1) * seg - 1:(sq + 1) * seg, :]

    rgroup = _row_iota((ROWS, 2 * SSD_STATE)) // seg
    for jj in range(seq_per_step):
        q = i * seq_per_step + jj
        r0 = pl.multiple_of(q * seg, seg)
        c_rows = c_scr[pl.ds(r0, seg), :].astype(BF16)
        b_all = b_scr[...]
        b_mask = jnp.where(rgroup == q, b_all, jnp.zeros_like(b_all))
        y_off, s_new = _ssd_state_step(c_rows, b_mask, sin_ref[jj], dec_scr[q], xt_scr)
        sh_ref[jj] = s_new
        y_scr[pl.ds(r0, seg), :] = y_scr[pl.ds(r0, seg), :] + ecs_scr[pl.ds(r0, seg), :] * y_off

    @pl.when(i == n_inner - 1)
    def _():
        _ssd_finish(pa_ref, y_scr, sg_ref, mixed_ref)


def _mixer_scratch(n_dec):
    return [
        pltpu.VMEM((ROWS, LRU_WIDTH), F32),
        pltpu.VMEM((ROWS, LRU_WIDTH), F32),
        pltpu.VMEM((ROWS, SSD_CONV_DIM), F32),
        pltpu.VMEM((ROWS, SSD_WIDTH), F32),
        pltpu.VMEM((ROWS, SSD_WIDTH), F32),
        pltpu.VMEM((ROWS, SSD_WIDTH), F32),
        pltpu.VMEM((SSD_WIDTH, ROWS), BF16),
        pltpu.VMEM((n_dec, SSD_HEADS, LANES), F32),
        pltpu.VMEM((ROWS, 2 * SSD_STATE), F32),
        pltpu.VMEM((ROWS, 2 * SSD_STATE), BF16),
    ]


def _full_spec(shape, n_grid):
    zeros = (0,) * len(shape)
    if n_grid == 2:
        return pl.BlockSpec(shape, lambda i, j: zeros)
    return pl.BlockSpec(shape, lambda i: zeros)


def _mixer_weight_specs(n_grid):
    return [
        _full_spec((CONV_WIDTH, LRU_WIDTH), n_grid),
        _full_spec((1, LRU_WIDTH), n_grid),
        _full_spec((LRU_HEADS, LANES, LANES), n_grid),
        _full_spec((LRU_HEADS, LANES, LANES), n_grid),
        _full_spec((1, LRU_WIDTH), n_grid),
        _full_spec((1, LRU_WIDTH), n_grid),
        _full_spec((1, LRU_WIDTH), n_grid),
        _full_spec((1, LRU_WIDTH), n_grid),
        _full_spec((CONV_WIDTH, SSD_CONV_DIM), n_grid),
        _full_spec((1, SSD_CONV_DIM), n_grid),
        _full_spec((1, LANES), n_grid),
        _full_spec((1, LANES), n_grid),
        _full_spec((1, SSD_WIDTH), n_grid),
        _full_spec((1, SSD_WIDTH), n_grid),
    ]


def _mixer_prompt(x1, mix_g, w_in_a, proj_b, weights, *, batch, seq_len):
    nc = seq_len // ROWS
    n_slabs = D_A // W_A_TILE

    def slab_spec(j):
        return pl.BlockSpec((D_MODEL, W_A_TILE), lambda b, c: (0, j), pipeline_mode=pl.Buffered(1))

    return pl.pallas_call(
        _mixer_prompt_kernel,
        grid=(batch, nc),
        in_specs=[
            pl.BlockSpec((ROWS, D_MODEL), lambda b, c: (b * nc + c, 0)),
            pl.BlockSpec((ROWS, D_MODEL), lambda b, c: (jnp.minimum(b * nc + c + 1, batch * nc - 1), 0)),
            pl.BlockSpec((1, D_MODEL), lambda b, c: (0, 0)),
        ] + [slab_spec(j) for j in range(n_slabs)] + [
            pl.BlockSpec((ROWS, D_B_PAD), lambda b, c: (b * nc + c, 0)),
        ] + _mixer_weight_specs(2),
        out_specs=[
            pl.BlockSpec((ROWS, LRU_WIDTH + SSD_WIDTH), lambda b, c: (b * nc + c, 0)),
            pl.BlockSpec((1, CONV_WIDTH - 1, LRU_WIDTH), lambda b, c: (b, 0, 0)),
            pl.BlockSpec((1, 1, LRU_WIDTH), lambda b, c: (b, 0, 0)),
            pl.BlockSpec((1, CONV_WIDTH - 1, SSD_CONV_DIM), lambda b, c: (b, 0, 0)),
            pl.BlockSpec((1, SSD_WIDTH, SSD_STATE), lambda b, c: (b, 0, 0)),
        ],
        out_shape=[
            jax.ShapeDtypeStruct((batch * seq_len, LRU_WIDTH + SSD_WIDTH), BF16),
            jax.ShapeDtypeStruct((batch, CONV_WIDTH - 1, LRU_WIDTH), F32),
            jax.ShapeDtypeStruct((batch, 1, LRU_WIDTH), F32),
            jax.ShapeDtypeStruct((batch, CONV_WIDTH - 1, SSD_CONV_DIM), F32),
            jax.ShapeDtypeStruct((batch, SSD_WIDTH, SSD_STATE), F32),
        ],
        scratch_shapes=[
            pltpu.VMEM((2, ROWS, D_A), F32),
            pltpu.VMEM((ROWS, D_MODEL), BF16),
            pltpu.VMEM((1, LRU_WIDTH), F32),
            pltpu.VMEM((SUBLANES, LRU_WIDTH), F32),
            pltpu.VMEM((SUBLANES, SSD_CONV_DIM), F32),
            pltpu.VMEM((SSD_WIDTH, SSD_STATE), F32),
        ] + _mixer_scratch(1),
        compiler_params=_cparams(2),
        name="mixer_prompt",
    )(x1, x1, mix_g, *([w_in_a] * n_slabs), proj_b, *weights)


def _mixer_sample(proj_a, proj_b, lprev, sprev, h0, s_in, weights, *, batch, seq_len, row_block_offset,
                  n_inner=4):
    n_seq = ROWS // seq_len
    n_outer = batch // n_seq
    sps = n_seq // n_inner
    return pl.pallas_call(
        functools.partial(_mixer_sample_kernel, n_inner),
        grid=(n_outer, n_inner),
        in_specs=[
            pl.BlockSpec((ROWS, D_A), lambda o, i: (o, 0)),
            pl.BlockSpec((ROWS, D_B_PAD), lambda o, i: (row_block_offset + o, 0)),
            pl.BlockSpec((ROWS, LRU_WIDTH), lambda o, i: (o, 0)),
            pl.BlockSpec((ROWS, SSD_CONV_DIM), lambda o, i: (o, 0)),
            pl.BlockSpec((n_seq, LRU_WIDTH), lambda o, i: (o, 0)),
            pl.BlockSpec((sps, SSD_WIDTH, SSD_STATE), lambda o, i: (o * n_inner + i, 0, 0)),
        ] + _mixer_weight_specs(2),
        out_specs=[
            pl.BlockSpec((ROWS, LRU_WIDTH + SSD_WIDTH), lambda o, i: (o, 0)),
            pl.BlockSpec((n_seq, CONV_WIDTH - 1, LRU_WIDTH), lambda o, i: (o, 0, 0)),
            pl.BlockSpec((n_seq, LRU_WIDTH), lambda o, i: (o, 0)),
            pl.BlockSpec((n_seq, CONV_WIDTH - 1, SSD_CONV_DIM), lambda o, i: (o, 0, 0)),
            pl.BlockSpec((sps, SSD_WIDTH, SSD_STATE), lambda o, i: (o * n_inner + i, 0, 0)),
        ],
        out_shape=[
            jax.ShapeDtypeStruct((batch * seq_len, LRU_WIDTH + SSD_WIDTH), BF16),
            jax.ShapeDtypeStruct((batch, CONV_WIDTH - 1, LRU_WIDTH), F32),
            jax.ShapeDtypeStruct((batch, LRU_WIDTH), F32),
            jax.ShapeDtypeStruct((batch, CONV_WIDTH - 1, SSD_CONV_DIM), F32),
            jax.ShapeDtypeStruct((batch, SSD_WIDTH, SSD_STATE), F32),
        ],
        scratch_shapes=_mixer_scratch(n_seq),
        compiler_params=_cparams(2),
        name="mixer_sample",
    )(proj_a, proj_b, lprev, sprev, h0, s_in, *weights)


def _row(v):
    return v.reshape(1, -1).astype(F32)


def _pad_lanes(v):
    v = v.reshape(1, -1).astype(F32)
    return jnp.pad(v, ((0, 0), (0, LANES - v.shape[1])))


def kernel(x_prompt, mem_prompt, x_sample, cache_mem_k, cache_mem_v, state_lru_conv, state_lru_h, state_ssd_conv, state_ssd_h, ffn1_norm_g, ffn1_w_gate, ffn1_w_up, ffn1_w_down, mix_norm_g, w_in, lru_conv_w, lru_conv_b, lru_w_a, lru_b_a, lru_w_x, lru_b_x, lru_lambda, lru_out_norm_g, ssd_conv_w, ssd_conv_b, ssd_dt_bias, ssd_a_log, ssd_d, ssd_out_norm_g, w_out, xattn_norm_g, mem_norm_g, xattn_w_q, xattn_w_k, xattn_w_v, xattn_w_o, ffn2_norm_g, ffn2_w_gate, ffn2_w_up, ffn2_w_down, final_norm_g):
    depth = ffn1_norm_g.shape[0]
    assert depth == 1
    bp, tp, d = x_prompt.shape
    bs, ts, _ = x_sample.shape
    mp = bp * tp
    ms = bs * ts
    assert tp % ROWS == 0 and ROWS % ts == 0 and ts == SUBLANES and mp % ROWS == 0
    l = 0

    w_in_a = w_in[l][:, :D_A].astype(BF16)
    w_in_b = jnp.pad(w_in[l][:, D_A:].astype(BF16), ((0, 0), (0, D_B_PAD - D_B)))
    mixer_weights = (
        lru_conv_w[l], _row(lru_conv_b[l]), lru_w_a[l].astype(BF16), lru_w_x[l].astype(BF16),
        _row(lru_b_a[l]), _row(lru_b_x[l]), _row(lru_lambda[l]), _row(lru_out_norm_g[l]),
        ssd_conv_w[l], _row(ssd_conv_b[l]), _pad_lanes(ssd_dt_bias[l]), _pad_lanes(ssd_a_log[l]),
        _row(jnp.repeat(ssd_d[l], SSD_HEADDIM)), _row(ssd_out_norm_g[l]),
    )

    (x1,) = _ffn((x_prompt.reshape(mp, d), x_sample.reshape(ms, d)), _row(ffn1_norm_g[l]),
                 ffn1_w_gate[l].astype(BF16), ffn1_w_up[l].astype(BF16), ffn1_w_down[l].astype(BF16),
                 _row(final_norm_g), out_rows=(mp + ms,), final_norm=False)
    mix_g = _row(mix_norm_g[l])
    proj_b = _norm_matmul(x1, mix_g, w_in_b, tm=1024, tn=1024, out_dtype=F32, name="in_proj_b")
    proj_a_s = _norm_matmul(x1, mix_g, w_in_a, tm=1024, tn=1024, out_dtype=F32, name="in_proj_a_sample",
                            row_offset=mp)

    mixed_p, p_lc, p_lh, p_sc, p_sh = _mixer_prompt(x1, mix_g, w_in_a, proj_b, mixer_weights,
                                                    batch=bp, seq_len=tp)

    pad_rows = ((0, 0), (SUBLANES - (CONV_WIDTH - 1), 0), (0, 0))
    lprev = jnp.pad(state_lru_conv[l], pad_rows).reshape(ms, LRU_WIDTH)
    sprev = jnp.pad(state_ssd_conv[l], pad_rows).reshape(ms, SSD_CONV_DIM)
    mixed_s, s_lc, s_lh, s_sc, s_sh = _mixer_sample(
        proj_a_s, proj_b, lprev, sprev, state_lru_h[l], state_ssd_h[l].reshape(bs, SSD_WIDTH, SSD_STATE),
        mixer_weights, batch=bs, seq_len=ts, row_block_offset=mp // ROWS)

    x2 = _matmul_residual(mixed_p, mixed_s, w_out[l].astype(BF16), x1, tm=1024, tn=512, name="out_proj")

    q = _norm_matmul(x2, _row(xattn_norm_g[l]), xattn_w_q[l].astype(BF16), tm=1024, tn=1024,
                     out_dtype=BF16, name="q_proj")
    mem = mem_prompt.reshape(bp * N_MEM, d)
    mk = _norm_matmul(mem, _row(mem_norm_g[l]), xattn_w_k[l].astype(BF16), tm=1024, tn=1024,
                      out_dtype=F32, name="mem_k")
    mv = _norm_matmul(mem, _row(mem_norm_g[l]), xattn_w_v[l].astype(BF16), tm=1024, tn=1024,
                      out_dtype=F32, name="mem_v")
    o_p = _xattn_prompt(q, mk.reshape(bp, N_MEM, d), mv.reshape(bp, N_MEM, d), seq_len=tp, tq=512)
    n_seq_x = 4
    o_s = _xattn_sample(q, cache_mem_k[l], cache_mem_v[l],
                        seq_len=ts, n_seq=n_seq_x, row_block_offset=mp // (n_seq_x * ts))
    x3 = _matmul_residual(o_p, o_s, xattn_w_o[l].astype(BF16), x2, tm=1024, tn=1024, name="o_proj")

    y_p, y_s = _ffn((x3,), _row(ffn2_norm_g[l]), ffn2_w_gate[l].astype(BF16), ffn2_w_up[l].astype(BF16),
                    ffn2_w_down[l].astype(BF16), _row(final_norm_g), out_rows=(mp, ms), final_norm=True)

    y_prompt = y_p.reshape(bp, tp, d)
    y_sample = y_s.reshape(bs, ts, d)
    hshape = (SSD_HEADS, SSD_HEADDIM, SSD_STATE)
    return (y_prompt, y_sample,
            p_lc[None], p_lh.reshape(1, bp, LRU_WIDTH), p_sc[None], p_sh.reshape((1, bp) + hshape),
            mk.reshape(1, bp, N_MEM, XATTN_HEADS, XATTN_HEAD_DIM),
            mv.reshape(1, bp, N_MEM, XATTN_HEADS, XATTN_HEAD_DIM),
            s_lc[None], s_lh[None], s_sc[None], s_sh.reshape((1, bs) + hshape))
```

```python
import functools

import jax
import jax.numpy as jnp
from jax import lax
from jax.experimental import pallas as pl
from jax.experimental.pallas import tpu as pltpu

F32 = jnp.float32
BF16 = jnp.bfloat16

D_MODEL = 2048
LRU_WIDTH = 2048
LRU_HEADS = 16
LRU_C = 8.0
CONV_WIDTH = 4
SSD_WIDTH = 2048
SSD_HEADDIM = 64
SSD_HEADS = 32
SSD_GROUPS = 2
SSD_STATE = 128
SSD_CONV_DIM = SSD_WIDTH + 2 * SSD_GROUPS * SSD_STATE
D_IN = 3 * 2048 + SSD_CONV_DIM + SSD_HEADS
N_MEM = 256
XATTN_HEADS = 4
XATTN_HEAD_DIM = 512
EPS = 1e-6

LANES = 128
SUBLANES = 8
VMEM_LIMIT_BYTES = 56 * 1024 * 1024

D_A = 3 * 2048
W_A_TILE = 512
COL_XLRU = 0
COL_GLRU = 2048
COL_Z = 4096
D_B = SSD_CONV_DIM + SSD_HEADS
D_B_PAD = 3072
COL_XBC = 0
COL_DT = SSD_CONV_DIM
ROWS = 128
GROUP_COLS = SSD_WIDTH // SSD_GROUPS

FFN_TM, FFN_TF = 512, 512
PROJ_TM, PROJ_TN = 1024, 1024
OUT_PROJ_TN = 512
XATTN_TQ = 512
XATTN_CACHE_SEQS = 4
SAMPLE_STATE_STEPS = 4


def _cparams(n_axes):
    return pltpu.CompilerParams(
        dimension_semantics=("arbitrary",) * n_axes,
        vmem_limit_bytes=VMEM_LIMIT_BYTES)


def _rmsnorm_rows(x, g):
    ms = jnp.mean(x * x, axis=-1, keepdims=True)
    return (x * lax.rsqrt(ms + EPS)) * g


def _softplus(x):
    return jnp.maximum(x, 0.0) + jnp.log1p(jnp.exp(-jnp.abs(x)))


def _silu(x):
    return x * jax.nn.sigmoid(x)


def _when_rows(i, n_a, fn_a, fn_b, extra=None):
    in_a = i < n_a
    in_b = i >= n_a
    if extra is not None:
        in_a = jnp.logical_and(in_a, extra)
        in_b = jnp.logical_and(in_b, extra)
    pl.when(in_a)(fn_a)
    pl.when(in_b)(fn_b)


def _ffn_kernel(n_a, split_in, split_out, final_norm, *refs):
    refs = list(refs)
    xa_ref = refs.pop(0)
    xb_ref = refs.pop(0) if split_in else xa_ref
    g_ref, wg_ref, wu_ref, wd_ref, gf_ref = refs[:5]
    oa_ref = refs[5]
    ob_ref = refs[6] if split_out else oa_ref
    h_scr, acc_scr = refs[-2:]
    i = pl.program_id(0)
    j = pl.program_id(1)

    def prologue(x_ref):
        def run():
            x = x_ref[...]
            h_scr[...] = _rmsnorm_rows(x, g_ref[...]).astype(BF16)
            acc_scr[...] = x
        return run

    if split_in:
        _when_rows(i, n_a, prologue(xa_ref), prologue(xb_ref), extra=(j == 0))
    else:
        pl.when(j == 0)(prologue(xa_ref))

    h = h_scr[...]
    gate = jnp.dot(h, wg_ref[...], preferred_element_type=F32)
    up = jnp.dot(h, wu_ref[...], preferred_element_type=F32)
    a = (0.5 * _silu(gate) * up).astype(BF16)
    acc_scr[...] += jnp.dot(a, wd_ref[...], preferred_element_type=F32)

    def epilogue(o_ref):
        def run():
            res = acc_scr[...]
            if final_norm:
                res = _rmsnorm_rows(res, gf_ref[...])
            o_ref[...] = res
        return run

    last = j == pl.num_programs(1) - 1
    if split_out:
        _when_rows(i, n_a, epilogue(oa_ref), epilogue(ob_ref), extra=last)
    else:
        pl.when(last)(epilogue(oa_ref))


def _ffn(xs, g, wg, wu, wd, gf, *, out_rows, final_norm):
    tm, tf = FFN_TM, FFN_TF
    split_in = len(xs) == 2
    split_out = len(out_rows) == 2
    d = xs[0].shape[1]
    f = wg.shape[1]
    m = sum(x.shape[0] for x in xs)
    assert m == sum(out_rows) and all(x.shape[0] % tm == 0 for x in xs) and all(r % tm == 0 for r in out_rows)
    n_a = (xs[0].shape[0] if split_in else out_rows[0]) // tm
    if split_in and split_out:
        assert xs[0].shape[0] == out_rows[0]

    def first(i, j):
        return (jnp.minimum(i, n_a - 1), 0)

    def second(i, j):
        return (jnp.maximum(i - n_a, 0), 0)

    def whole(i, j):
        return (i, 0)

    x_specs = ([pl.BlockSpec((tm, d), first), pl.BlockSpec((tm, d), second)] if split_in
               else [pl.BlockSpec((tm, d), whole)])
    o_specs = ([pl.BlockSpec((tm, d), first), pl.BlockSpec((tm, d), second)] if split_out
               else [pl.BlockSpec((tm, d), whole)])
    return pl.pallas_call(
        functools.partial(_ffn_kernel, n_a, split_in, split_out, final_norm),
        grid=(m // tm, f // tf),
        in_specs=x_specs + [
            pl.BlockSpec((1, d), lambda i, j: (0, 0)),
            pl.BlockSpec((d, tf), lambda i, j: (0, j)),
            pl.BlockSpec((d, tf), lambda i, j: (0, j)),
            pl.BlockSpec((tf, d), lambda i, j: (j, 0)),
            pl.BlockSpec((1, d), lambda i, j: (0, 0)),
        ],
        out_specs=o_specs,
        out_shape=[jax.ShapeDtypeStruct((r, d), F32) for r in out_rows],
        scratch_shapes=[pltpu.VMEM((tm, d), BF16), pltpu.VMEM((tm, d), F32)],
        compiler_params=_cparams(2),
        name="ffn_final" if final_norm else "ffn",
    )(*xs, g, wg, wu, wd, gf)


def _norm_matmul_kernel(x_ref, g_ref, w_ref, o_ref, h_scr):
    @pl.when(pl.program_id(1) == 0)
    def _():
        h_scr[...] = _rmsnorm_rows(x_ref[...], g_ref[...]).astype(BF16)

    o_ref[...] = jnp.dot(h_scr[...], w_ref[...], preferred_element_type=F32).astype(o_ref.dtype)


def _norm_matmul(x, g, w, *, out_dtype, name, row_offset=0, n_cols=None):
    tm, tn = PROJ_TM, PROJ_TN
    k = x.shape[1]
    m = x.shape[0] - row_offset
    n = w.shape[1] if n_cols is None else n_cols
    assert m % tm == 0 and row_offset % tm == 0 and n % tn == 0
    first = row_offset // tm
    return pl.pallas_call(
        _norm_matmul_kernel,
        grid=(m // tm, n // tn),
        in_specs=[
            pl.BlockSpec((tm, k), lambda i, j: (first + i, 0)),
            pl.BlockSpec((1, k), lambda i, j: (0, 0)),
            pl.BlockSpec((k, tn), lambda i, j: (0, j)),
        ],
        out_specs=pl.BlockSpec((tm, tn), lambda i, j: (i, j)),
        out_shape=jax.ShapeDtypeStruct((m, n), out_dtype),
        scratch_shapes=[pltpu.VMEM((tm, k), BF16)],
        compiler_params=_cparams(2),
        name=name,
    )(x, g, w)


def _matmul_residual_kernel(n_a, aa_ref, ab_ref, w_ref, r_ref, o_ref):
    def run(a_ref):
        def body():
            o_ref[...] = r_ref[...] + jnp.dot(a_ref[...], w_ref[...], preferred_element_type=F32)
        return body

    _when_rows(pl.program_id(0), n_a, run(aa_ref), run(ab_ref))


def _matmul_residual(a_first, a_second, w, res, *, tn, name):
    tm = PROJ_TM
    k = a_first.shape[1]
    m = a_first.shape[0] + a_second.shape[0]
    n = w.shape[1]
    assert a_first.shape[0] % tm == 0 and a_second.shape[0] % tm == 0 and res.shape == (m, n)
    n_a = a_first.shape[0] // tm
    return pl.pallas_call(
        functools.partial(_matmul_residual_kernel, n_a),
        grid=(m // tm, n // tn),
        in_specs=[
            pl.BlockSpec((tm, k), lambda i, j: (jnp.minimum(i, n_a - 1), 0)),
            pl.BlockSpec((tm, k), lambda i, j: (jnp.maximum(i - n_a, 0), 0)),
            pl.BlockSpec((k, tn), lambda i, j: (0, j)),
            pl.BlockSpec((tm, tn), lambda i, j: (i, j)),
        ],
        out_specs=pl.BlockSpec((tm, tn), lambda i, j: (i, j)),
        out_shape=jax.ShapeDtypeStruct((m, n), F32),
        compiler_params=_cparams(2),
        name=name,
    )(a_first, a_second, w, res)


def _xattn_kernel(n_seq, tq, q_ref, k_ref, v_ref, o_ref):
    scale = XATTN_HEAD_DIM ** -0.5
    for s in range(n_seq):
        rows = slice(s * tq, (s + 1) * tq)
        for h in range(XATTN_HEADS):
            cols = slice(h * XATTN_HEAD_DIM, (h + 1) * XATTN_HEAD_DIM)
            q = q_ref[rows, cols]
            k = k_ref[s, :, cols].astype(BF16)
            v = v_ref[s, :, cols].astype(BF16)
            sc = lax.dot_general(q, k, (((1,), (1,)), ((), ())),
                                 preferred_element_type=F32) * scale
            mx = jnp.max(sc, axis=-1, keepdims=True)
            e = jnp.exp(sc - mx)
            p = e / jnp.sum(e, axis=-1, keepdims=True)
            o = jnp.dot(p.astype(BF16), v, preferred_element_type=F32)
            o_ref[rows, cols] = o.astype(BF16)


def _xattn_prompt(q, k, v, *, seq_len, tq):
    b = k.shape[0]
    nt = seq_len // tq
    d = q.shape[1]
    return pl.pallas_call(
        functools.partial(_xattn_kernel, 1, tq),
        grid=(b, nt),
        in_specs=[
            pl.BlockSpec((tq, d), lambda i, j: (i * nt + j, 0)),
            pl.BlockSpec((1, N_MEM, d), lambda i, j: (i, 0, 0)),
            pl.BlockSpec((1, N_MEM, d), lambda i, j: (i, 0, 0)),
        ],
        out_specs=pl.BlockSpec((tq, d), lambda i, j: (i * nt + j, 0)),
        out_shape=jax.ShapeDtypeStruct((b * seq_len, d), BF16),
        compiler_params=_cparams(2),
        name="xattn_prompt",
    )(q, k, v)


def _xattn_cache_kernel(n_seq, tq, q_ref, k_ref, v_ref, o_ref):
    scale = XATTN_HEAD_DIM ** -0.5
    n_rows = XATTN_HEADS * tq
    n_cols = N_MEM * XATTN_HEADS
    own = (_lane_iota((n_rows, n_cols)) % XATTN_HEADS) == (_row_iota((n_rows, n_cols)) // tq)
    qf = q_ref[...].astype(F32)
    outs = []
    for s in range(n_seq):
        qs = qf[s * tq:(s + 1) * tq]
        q4 = jnp.concatenate(
            [qs[:, h * XATTN_HEAD_DIM:(h + 1) * XATTN_HEAD_DIM] for h in range(XATTN_HEADS)],
            axis=0).astype(BF16)
        kf = k_ref[s].reshape(n_cols, XATTN_HEAD_DIM).astype(BF16)
        vf = v_ref[s].reshape(n_cols, XATTN_HEAD_DIM).astype(BF16)
        sc = lax.dot_general(q4, kf, (((1,), (1,)), ((), ())), preferred_element_type=F32) * scale
        sc = jnp.where(own, sc, -1e30)
        mx = jnp.max(sc, axis=-1, keepdims=True)
        e = jnp.exp(sc - mx)
        p = e / jnp.sum(e, axis=-1, keepdims=True)
        o4 = jnp.dot(p.astype(BF16), vf, preferred_element_type=F32)
        outs.append(jnp.concatenate([o4[h * tq:(h + 1) * tq] for h in range(XATTN_HEADS)], axis=1))
    o_ref[...] = jnp.concatenate(outs, axis=0).astype(BF16)


def _xattn_sample(q, k, v, *, seq_len, n_seq, row_block_offset):
    b = k.shape[0]
    d = q.shape[1]
    rows = n_seq * seq_len
    kv_block = (n_seq, N_MEM, XATTN_HEADS, XATTN_HEAD_DIM)
    return pl.pallas_call(
        functools.partial(_xattn_cache_kernel, n_seq, seq_len),
        grid=(b // n_seq,),
        in_specs=[
            pl.BlockSpec((rows, d), lambda i: (row_block_offset + i, 0)),
            pl.BlockSpec(kv_block, lambda i: (i, 0, 0, 0)),
            pl.BlockSpec(kv_block, lambda i: (i, 0, 0, 0)),
        ],
        out_specs=pl.BlockSpec((rows, d), lambda i: (i, 0)),
        out_shape=jax.ShapeDtypeStruct((b * seq_len, d), BF16),
        compiler_params=_cparams(1),
        name="xattn_sample",
    )(q, k, v)


def _row_iota(shape):
    return lax.broadcasted_iota(jnp.int32, shape, 0)


def _lane_iota(shape):
    return lax.broadcasted_iota(jnp.int32, shape, 1)


def _seg_cumsum(x, seg_len):
    pos = _row_iota(x.shape) % seg_len
    s = 1
    while s < seg_len:
        x = x + jnp.where(pos >= s, pltpu.roll(x, s, axis=0), 0.0)
        s *= 2
    return x


def _seg_rev_excl_cumsum(x, seg_len):
    n = x.shape[0]
    pos = _row_iota(x.shape) % seg_len
    y = jnp.where(pos < seg_len - 1, pltpu.roll(x, n - 1, axis=0), 0.0)
    s = 1
    while s < seg_len:
        y = y + jnp.where(pos < seg_len - s, pltpu.roll(y, n - s, axis=0), 0.0)
        s *= 2
    return y


def _scan8(a, b):
    shape = a.shape
    tiled = (shape[0] // SUBLANES, SUBLANES, shape[1])
    a = a.reshape(tiled)
    b = b.reshape(tiled)
    pos = lax.broadcasted_iota(jnp.int32, tiled, 1)
    for s in (1, 2, 4):
        m = pos >= s
        a_sh = pltpu.roll(a, s, axis=1)
        b_sh = pltpu.roll(b, s, axis=1)
        b = jnp.where(m, a * b_sh + b, b)
        a = jnp.where(m, a * a_sh, a)
    return a.reshape(shape), b.reshape(shape)


def _conv_taps(x, w_ref, b_ref, cols, fix):
    acc = b_ref[:, cols] + x * w_ref[CONV_WIDTH - 1:CONV_WIDTH, cols]
    for s in range(1, CONV_WIDTH):
        xs = fix(s, pltpu.roll(x, s, axis=0))
        k = CONV_WIDTH - 1 - s
        acc = acc + xs * w_ref[k:k + 1, cols]
    return acc


def _bcast_col(v, c):
    return jnp.broadcast_to(v[:, c:c + 1], (v.shape[0], LANES))


def _lru_part(sample, pa_ref, prev_fix_lru, h0_rows, hcar_ref,
              lcw_ref, lcb_ref, wa_ref, wx_ref, ba_ref, bx_ref, lam_ref, lg_ref,
              mixed_ref, hs_scr, yl_scr):
    ssq = jnp.zeros((ROWS, 1), F32)
    for hd in range(LRU_HEADS):
        cols = slice(hd * LANES, (hd + 1) * LANES)
        xl = pa_ref[:, COL_XLRU + hd * LANES:COL_XLRU + (hd + 1) * LANES]
        u = _conv_taps(xl, lcw_ref, lcb_ref, cols, functools.partial(prev_fix_lru, cols))
        ub = u.astype(BF16)
        r = jax.nn.sigmoid(jnp.dot(ub, wa_ref[hd], preferred_element_type=F32) + ba_ref[:, cols])
        i = jax.nn.sigmoid(jnp.dot(ub, wx_ref[hd], preferred_element_type=F32) + bx_ref[:, cols])
        log_a = (-LRU_C * r) * _softplus(-lam_ref[:, cols])
        a = jnp.exp(log_a)
        th = jnp.tanh(log_a)
        beta = jnp.sqrt((-2.0 * th) / (1.0 - th))
        bb = (beta * i) * u
        a_cum, b_cum = _scan8(a, bb)
        if sample:
            h = a_cum * h0_rows(cols) + b_cum
        else:
            carry = hcar_ref[:, cols]
            parts = []
            for t in range(ROWS // SUBLANES):
                rs = slice(t * SUBLANES, (t + 1) * SUBLANES)
                ht = a_cum[rs] * carry + b_cum[rs]
                parts.append(ht)
                carry = ht[SUBLANES - 1:SUBLANES]
            h = jnp.concatenate(parts, axis=0)
            hcar_ref[:, cols] = carry
        hs_scr[:, cols] = h
        g = jax.nn.gelu(pa_ref[:, COL_GLRU + hd * LANES:COL_GLRU + (hd + 1) * LANES])
        yl = h * g
        ssq = ssq + jnp.sum(yl * yl, axis=-1, keepdims=True)
        yl_scr[:, cols] = yl
    inv = lax.rsqrt(ssq * (1.0 / LRU_WIDTH) + EPS)
    for hd in range(LRU_HEADS):
        cols = slice(hd * LANES, (hd + 1) * LANES)
        mixed_ref[:, cols] = ((yl_scr[:, cols] * inv) * lg_ref[:, cols]).astype(BF16)


def _ssd_front(sample, pb_ref, prev_fix_ssd, scw_ref, scb_ref, dtb_ref, alog_ref, dexp_ref,
               xc_scr, xw_scr, ecs_scr, y_scr, xt_scr, dec_scr, c_scr, b_scr, side_work=()):
    seg = SUBLANES if sample else ROWS
    n_seq = ROWS // seg
    n_conv_tiles = SSD_CONV_DIM // LANES
    n_pairs = SSD_HEADS // 2
    pending = list(side_work)
    stride = max(1, (n_conv_tiles + n_pairs) // max(1, len(pending)))
    point = [0]

    def side_point():
        if pending and point[0] % stride == 0:
            pending.pop(0)()
        point[0] += 1

    for t in range(n_conv_tiles):
        side_point()
        cols = slice(t * LANES, (t + 1) * LANES)
        xb = pb_ref[:, COL_XBC + t * LANES:COL_XBC + (t + 1) * LANES]
        v = _conv_taps(xb, scw_ref, scb_ref, cols, functools.partial(prev_fix_ssd, cols))
        xc_scr[:, cols] = _silu(v)

    dt = _softplus(pb_ref[:, COL_DT:COL_DT + LANES] + dtb_ref[...])
    a_neg = -jnp.exp(alog_ref[...])
    d_a = dt * a_neg
    cs = _seg_cumsum(d_a, seg)
    rcs = _seg_rev_excl_cumsum(d_a, seg)
    todt = jnp.exp(rcs) * dt
    dec_tot = jnp.exp(cs + rcs)
    cs_t = cs.T
    dt_t = dt.T
    dec_t = dec_tot.T
    for s in range(n_seq):
        dec_scr[s] = jnp.broadcast_to(dec_t[0:SSD_HEADS, s * seg:s * seg + 1], (SSD_HEADS, LANES))

    ri = _row_iota((ROWS, ROWS))
    ci = _lane_iota((ROWS, ROWS))
    causal = ri >= ci
    if sample:
        causal = jnp.logical_and(causal, (ri // seg) == (ci // seg))
    lo = _lane_iota((ROWS, LANES)) < SSD_HEADDIM

    b_off = SSD_WIDTH
    c_off = SSD_WIDTH + SSD_GROUPS * SSD_STATE
    for g in range(SSD_GROUPS):
        gcols = slice(g * SSD_STATE, (g + 1) * SSD_STATE)
        bg = xc_scr[:, b_off + g * SSD_STATE:b_off + (g + 1) * SSD_STATE].astype(BF16)
        cg = xc_scr[:, c_off + g * SSD_STATE:c_off + (g + 1) * SSD_STATE].astype(BF16)
        c_scr[:, gcols] = xc_scr[:, c_off + g * SSD_STATE:c_off + (g + 1) * SSD_STATE]
        b_scr[:, gcols] = bg
        cb = lax.dot_general(cg, bg, (((1,), (1,)), ((), ())), preferred_element_type=F32)
        heads_per_group = SSD_HEADS // SSD_GROUPS
        for jp in range(heads_per_group // 2):
            side_point()
            h0 = g * heads_per_group + 2 * jp
            h1 = h0 + 1
            cols = slice(h0 * SSD_HEADDIM, h0 * SSD_HEADDIM + LANES)
            x_pair = xc_scr[:, cols]
            ws = []
            cs_cols = []
            for hh in (h0, h1):
                cs_col = _bcast_col(cs, hh)
                cs_cols.append(cs_col)
                dec = jnp.where(causal, jnp.exp(cs_col - cs_t[hh:hh + 1, :]), 0.0)
                ws.append(((cb * dec) * dt_t[hh:hh + 1, :]).astype(BF16))
            lhs = jnp.concatenate(ws, axis=1)
            rhs = jnp.concatenate([jnp.where(lo, x_pair, 0.0), jnp.where(lo, 0.0, x_pair)],
                                  axis=0).astype(BF16)
            y_diag = jnp.dot(lhs, rhs, preferred_element_type=F32)
            y_scr[:, cols] = y_diag + x_pair * dexp_ref[:, cols]
            xw_scr[:, cols] = x_pair * jnp.where(lo, _bcast_col(todt, h0), _bcast_col(todt, h1))
            ecs_scr[:, cols] = jnp.exp(jnp.where(lo, cs_cols[0], cs_cols[1]))
    while pending:
        pending.pop(0)()
    for t in range(SSD_WIDTH // LANES):
        xt_scr[t * LANES:(t + 1) * LANES, :] = xw_scr[:, t * LANES:(t + 1) * LANES].T.astype(BF16)


def _ssd_state_step(c_rows, b_mask, s_old, dec, xt_scr):
    y_parts = []
    s_parts = []
    heads_per_group = SSD_HEADS // SSD_GROUPS
    for g in range(SSD_GROUPS):
        gcols = slice(g * SSD_STATE, (g + 1) * SSD_STATE)
        grows = slice(g * GROUP_COLS, (g + 1) * GROUP_COLS)
        sg = s_old[grows]
        y_parts.append(lax.dot_general(c_rows[:, gcols], sg.astype(BF16), (((1,), (1,)), ((), ())),
                                       preferred_element_type=F32))
        upd = jnp.dot(xt_scr[grows, :], b_mask[:, gcols], preferred_element_type=F32)
        for hh in range(heads_per_group):
            hr = slice(hh * SSD_HEADDIM, (hh + 1) * SSD_HEADDIM)
            habs = g * heads_per_group + hh
            s_parts.append(sg[hr] * dec[habs:habs + 1, :] + upd[hr])
    return jnp.concatenate(y_parts, axis=1), jnp.concatenate(s_parts, axis=0)


def _ssd_finish(pa_ref, y_scr, sg_ref, mixed_ref):
    for g in range(SSD_GROUPS):
        ssq = jnp.zeros((ROWS, 1), F32)
        n_t = GROUP_COLS // LANES
        for t in range(n_t):
            c0 = g * GROUP_COLS + t * LANES
            yz = y_scr[:, c0:c0 + LANES] * _silu(pa_ref[:, COL_Z + c0:COL_Z + c0 + LANES])
            y_scr[:, c0:c0 + LANES] = yz
            ssq = ssq + jnp.sum(yz * yz, axis=-1, keepdims=True)
        inv = lax.rsqrt(ssq * (1.0 / GROUP_COLS) + EPS)
        for t in range(n_t):
            c0 = g * GROUP_COLS + t * LANES
            mixed_ref[:, LRU_WIDTH + c0:LRU_WIDTH + c0 + LANES] = (
                (y_scr[:, c0:c0 + LANES] * inv) * sg_ref[:, c0:c0 + LANES]).astype(BF16)


def _mixer_prompt_kernel(x_ref, xn_ref, mg_ref, *rest):
    n_slabs = D_A // W_A_TILE
    wina_refs = rest[:n_slabs]
    (pb_ref,
     lcw_ref, lcb_ref, wa_ref, wx_ref, ba_ref, bx_ref, lam_ref, lg_ref,
     scw_ref, scb_ref, dtb_ref, alog_ref, dexp_ref, sg_ref,
     mixed_ref, lconv_ref, lh_ref, sconv_ref, sh_ref,
     pa_scr, hn_scr, hcar_ref, ltail_scr, stail_scr, s_scr,
     hs_scr, yl_scr, xc_scr, xw_scr, ecs_scr, y_scr, xt_scr, dec_scr, c_scr, b_scr) = rest[n_slabs:]
    c = pl.program_id(1)
    step = pl.program_id(0) * pl.num_programs(1) + c
    slot = step % 2

    @pl.when(c == 0)
    def _():
        hcar_ref[...] = jnp.zeros_like(hcar_ref)
        ltail_scr[...] = jnp.zeros_like(ltail_scr)
        stail_scr[...] = jnp.zeros_like(stail_scr)
        s_scr[...] = jnp.zeros_like(s_scr)

    @pl.when(step == 0)
    def _():
        hn = _rmsnorm_rows(x_ref[...], mg_ref[...]).astype(BF16)
        for j in range(n_slabs):
            pa_scr[0, :, j * W_A_TILE:(j + 1) * W_A_TILE] = jnp.dot(
                hn, wina_refs[j][...], preferred_element_type=F32)

    hn_scr[...] = _rmsnorm_rows(xn_ref[...], mg_ref[...]).astype(BF16)
    pa_next = pa_scr.at[1 - slot]
    pa_cur = pa_scr.at[slot]

    def project_slab(j):
        def run():
            pa_next[:, j * W_A_TILE:(j + 1) * W_A_TILE] = jnp.dot(
                hn_scr[...], wina_refs[j][...], preferred_element_type=F32)
        return run

    side_work = [project_slab(j) for j in range(n_slabs)]

    pos8 = _row_iota((SUBLANES, LANES))

    def make_fix(tail_scr):
        def fix(cols, s, rolled):
            first = jnp.where(pos8 < s, pltpu.roll(tail_scr[:, cols], s, axis=0), rolled[0:SUBLANES])
            return jnp.concatenate([first, rolled[SUBLANES:]], axis=0)
        return fix

    _lru_part(False, pa_cur, make_fix(ltail_scr), None, hcar_ref,
              lcw_ref, lcb_ref, wa_ref, wx_ref, ba_ref, bx_ref, lam_ref, lg_ref,
              mixed_ref, hs_scr, yl_scr)
    _ssd_front(False, pb_ref, make_fix(stail_scr), scw_ref, scb_ref, dtb_ref, alog_ref, dexp_ref,
               xc_scr, xw_scr, ecs_scr, y_scr, xt_scr, dec_scr, c_scr, b_scr, side_work=side_work)

    ltail_scr[...] = pa_cur[ROWS - SUBLANES:ROWS, COL_XLRU:COL_XLRU + LRU_WIDTH]
    stail_scr[...] = pb_ref[ROWS - SUBLANES:ROWS, COL_XBC:COL_XBC + SSD_CONV_DIM]

    y_off, s_new = _ssd_state_step(c_scr[...].astype(BF16), b_scr[...], s_scr[...], dec_scr[0], xt_scr)
    s_scr[...] = s_new
    for t in range(SSD_WIDTH // LANES):
        cols = slice(t * LANES, (t + 1) * LANES)
        y_scr[:, cols] = y_scr[:, cols] + ecs_scr[:, cols] * y_off[:, cols]
    _ssd_finish(pa_cur, y_scr, sg_ref, mixed_ref)

    @pl.when(c == pl.num_programs(1) - 1)
    def _():
        lconv_ref[0] = pa_cur[ROWS - (CONV_WIDTH - 1):ROWS, COL_XLRU:COL_XLRU + LRU_WIDTH]
        sconv_ref[0] = pb_ref[ROWS - (CONV_WIDTH - 1):ROWS, COL_XBC:COL_XBC + SSD_CONV_DIM]
        lh_ref[0] = hcar_ref[...]
        sh_ref[0] = s_scr[...]


def _mixer_sample_kernel(n_inner, pa_ref, pb_ref, lprev_ref, sprev_ref, h0_ref, sin_ref,
                         lcw_ref, lcb_ref, wa_ref, wx_ref, ba_ref, bx_ref, lam_ref, lg_ref,
                         scw_ref, scb_ref, dtb_ref, alog_ref, dexp_ref, sg_ref,
                         mixed_ref, lconv_ref, lh_ref, sconv_ref, sh_ref,
                         hs_scr, yl_scr, xc_scr, xw_scr, ecs_scr, y_scr, xt_scr, dec_scr, c_scr, b_scr):
    i = pl.program_id(1)
    seg = SUBLANES
    n_seq = ROWS // seg
    seq_per_step = n_seq // n_inner

    @pl.when(i == 0)
    def _():
        pos = _row_iota((ROWS, LANES)) % seg

        def make_fix(prev_ref):
            def fix(cols, s, rolled):
                return jnp.where(pos < s, pltpu.roll(prev_ref[:, cols], ROWS - seg + s, axis=0), rolled)
            return fix

        def h0_rows(cols):
            return jnp.concatenate(
                [jnp.broadcast_to(h0_ref[q:q + 1, cols], (seg, LANES)) for q in range(n_seq)], axis=0)

        _ssd_front(True, pb_ref, make_fix(sprev_ref), scw_ref, scb_ref, dtb_ref, alog_ref, dexp_ref,
                   xc_scr, xw_scr, ecs_scr, y_scr, xt_scr, dec_scr, c_scr, b_scr)
        _lru_part(True, pa_ref, make_fix(lprev_ref), h0_rows, None,
                  lcw_ref, lcb_ref, wa_ref, wx_ref, ba_ref, bx_ref, lam_ref, lg_ref,
                  mixed_ref, hs_scr, yl_scr)
        for sq in range(n_seq):
            tail = slice((sq + 1) * seg - (CONV_WIDTH - 1), (sq + 1) * seg)
            lconv_ref[sq] = pa_ref[tail, COL_XLRU:COL_XLRU + LRU_WIDTH]
            sconv_ref[sq] = pb_ref[tail, COL_XBC:COL_XBC + SSD_CONV_DIM]
            lh_ref[sq:sq + 1, :] = hs_scr[(sq + 1) * seg - 1:(sq + 1) * seg, :]

    rgroup = _row_iota((ROWS, 2 * SSD_STATE)) // seg
    for jj in range(seq_per_step):
        q = i * seq_per_step + jj
        r0 = pl.multiple_of(q * seg, seg)
        c_rows = c_scr[pl.ds(r0, seg), :].astype(BF16)
        b_all = b_scr[...]
        b_mask = jnp.where(rgroup == q, b_all, jnp.zeros_like(b_all))
        y_off, s_new = _ssd_state_step(c_rows, b_mask, sin_ref[jj], dec_scr[q], xt_scr)
        sh_ref[jj] = s_new
        y_scr[pl.ds(r0, seg), :] = y_scr[pl.ds(r0, seg), :] + ecs_scr[pl.ds(r0, seg), :] * y_off

    @pl.when(i == n_inner - 1)
    def _():
        _ssd_finish(pa_ref, y_scr, sg_ref, mixed_ref)


def _mixer_scratch(n_dec):
    return [
        pltpu.VMEM((ROWS, LRU_WIDTH), F32),
        pltpu.VMEM((ROWS, LRU_WIDTH), F32),
        pltpu.VMEM((ROWS, SSD_CONV_DIM), F32),
        pltpu.VMEM((ROWS, SSD_WIDTH), F32),
        pltpu.VMEM((ROWS, SSD_WIDTH), F32),
        pltpu.VMEM((ROWS, SSD_WIDTH), F32),
        pltpu.VMEM((SSD_WIDTH, ROWS), BF16),
        pltpu.VMEM((n_dec, SSD_HEADS, LANES), F32),
        pltpu.VMEM((ROWS, 2 * SSD_STATE), F32),
        pltpu.VMEM((ROWS, 2 * SSD_STATE), BF16),
    ]


def _full_spec(shape, n_grid):
    zeros = (0,) * len(shape)
    if n_grid == 2:
        return pl.BlockSpec(shape, lambda i, j: zeros)
    return pl.BlockSpec(shape, lambda i: zeros)


def _mixer_weight_specs(n_grid):
    return [
        _full_spec((CONV_WIDTH, LRU_WIDTH), n_grid),
        _full_spec((1, LRU_WIDTH), n_grid),
        _full_spec((LRU_HEADS, LANES, LANES), n_grid),
        _full_spec((LRU_HEADS, LANES, LANES), n_grid),
        _full_spec((1, LRU_WIDTH), n_grid),
        _full_spec((1, LRU_WIDTH), n_grid),
        _full_spec((1, LRU_WIDTH), n_grid),
        _full_spec((1, LRU_WIDTH), n_grid),
        _full_spec((CONV_WIDTH, SSD_CONV_DIM), n_grid),
        _full_spec((1, SSD_CONV_DIM), n_grid),
        _full_spec((1, LANES), n_grid),
        _full_spec((1, LANES), n_grid),
        _full_spec((1, SSD_WIDTH), n_grid),
        _full_spec((1, SSD_WIDTH), n_grid),
    ]


def _mixer_prompt(x1, mix_g, w_in_a, proj_b, weights, *, batch, seq_len):
    nc = seq_len // ROWS
    n_slabs = D_A // W_A_TILE

    def slab_spec(j):
        return pl.BlockSpec((D_MODEL, W_A_TILE), lambda b, c: (0, j), pipeline_mode=pl.Buffered(1))

    return pl.pallas_call(
        _mixer_prompt_kernel,
        grid=(batch, nc),
        in_specs=[
            pl.BlockSpec((ROWS, D_MODEL), lambda b, c: (b * nc + c, 0)),
            pl.BlockSpec((ROWS, D_MODEL), lambda b, c: (jnp.minimum(b * nc + c + 1, batch * nc - 1), 0)),
            pl.BlockSpec((1, D_MODEL), lambda b, c: (0, 0)),
        ] + [slab_spec(j) for j in range(n_slabs)] + [
            pl.BlockSpec((ROWS, D_B_PAD), lambda b, c: (b * nc + c, 0)),
        ] + _mixer_weight_specs(2),
        out_specs=[
            pl.BlockSpec((ROWS, LRU_WIDTH + SSD_WIDTH), lambda b, c: (b * nc + c, 0)),
            pl.BlockSpec((1, CONV_WIDTH - 1, LRU_WIDTH), lambda b, c: (b, 0, 0)),
            pl.BlockSpec((1, 1, LRU_WIDTH), lambda b, c: (b, 0, 0)),
            pl.BlockSpec((1, CONV_WIDTH - 1, SSD_CONV_DIM), lambda b, c: (b, 0, 0)),
            pl.BlockSpec((1, SSD_WIDTH, SSD_STATE), lambda b, c: (b, 0, 0)),
        ],
        out_shape=[
            jax.ShapeDtypeStruct((batch * seq_len, LRU_WIDTH + SSD_WIDTH), BF16),
            jax.ShapeDtypeStruct((batch, CONV_WIDTH - 1, LRU_WIDTH), F32),
            jax.ShapeDtypeStruct((batch, 1, LRU_WIDTH), F32),
            jax.ShapeDtypeStruct((batch, CONV_WIDTH - 1, SSD_CONV_DIM), F32),
            jax.ShapeDtypeStruct((batch, SSD_WIDTH, SSD_STATE), F32),
        ],
        scratch_shapes=[
            pltpu.VMEM((2, ROWS, D_A), F32),
            pltpu.VMEM((ROWS, D_MODEL), BF16),
            pltpu.VMEM((1, LRU_WIDTH), F32),
            pltpu.VMEM((SUBLANES, LRU_WIDTH), F32),
            pltpu.VMEM((SUBLANES, SSD_CONV_DIM), F32),
            pltpu.VMEM((SSD_WIDTH, SSD_STATE), F32),
        ] + _mixer_scratch(1),
        compiler_params=_cparams(2),
        name="mixer_prompt",
    )(x1, x1, mix_g, *([w_in_a] * n_slabs), proj_b, *weights)


def _mixer_sample(proj_a, proj_b, lprev, sprev, h0, s_in, weights, *, batch, seq_len, row_block_offset,
                  n_inner=SAMPLE_STATE_STEPS):
    n_seq = ROWS // seq_len
    n_outer = batch // n_seq
    sps = n_seq // n_inner
    return pl.pallas_call(
        functools.partial(_mixer_sample_kernel, n_inner),
        grid=(n_outer, n_inner),
        in_specs=[
            pl.BlockSpec((ROWS, D_A), lambda o, i: (o, 0)),
            pl.BlockSpec((ROWS, D_B_PAD), lambda o, i: (row_block_offset + o, 0)),
            pl.BlockSpec((ROWS, LRU_WIDTH), lambda o, i: (o, 0)),
            pl.BlockSpec((ROWS, SSD_CONV_DIM), lambda o, i: (o, 0)),
            pl.BlockSpec((n_seq, LRU_WIDTH), lambda o, i: (o, 0)),
            pl.BlockSpec((sps, SSD_WIDTH, SSD_STATE), lambda o, i: (o * n_inner + i, 0, 0)),
        ] + _mixer_weight_specs(2),
        out_specs=[
            pl.BlockSpec((ROWS, LRU_WIDTH + SSD_WIDTH), lambda o, i: (o, 0)),
            pl.BlockSpec((n_seq, CONV_WIDTH - 1, LRU_WIDTH), lambda o, i: (o, 0, 0)),
            pl.BlockSpec((n_seq, LRU_WIDTH), lambda o, i: (o, 0)),
            pl.BlockSpec((n_seq, CONV_WIDTH - 1, SSD_CONV_DIM), lambda o, i: (o, 0, 0)),
            pl.BlockSpec((sps, SSD_WIDTH, SSD_STATE), lambda o, i: (o * n_inner + i, 0, 0)),
        ],
        out_shape=[
            jax.ShapeDtypeStruct((batch * seq_len, LRU_WIDTH + SSD_WIDTH), BF16),
            jax.ShapeDtypeStruct((batch, CONV_WIDTH - 1, LRU_WIDTH), F32),
            jax.ShapeDtypeStruct((batch, LRU_WIDTH), F32),
            jax.ShapeDtypeStruct((batch, CONV_WIDTH - 1, SSD_CONV_DIM), F32),
            jax.ShapeDtypeStruct((batch, SSD_WIDTH, SSD_STATE), F32),
        ],
        scratch_shapes=_mixer_scratch(n_seq),
        compiler_params=_cparams(2),
        name="mixer_sample",
    )(proj_a, proj_b, lprev, sprev, h0, s_in, *weights)


def _row(v):
    return v.reshape(1, -1).astype(F32)


def _pad_lanes(v):
    v = v.reshape(1, -1).astype(F32)
    return jnp.pad(v, ((0, 0), (0, LANES - v.shape[1])))


def kernel(x_prompt, mem_prompt, x_sample, cache_mem_k, cache_mem_v, state_lru_conv, state_lru_h, state_ssd_conv, state_ssd_h, ffn1_norm_g, ffn1_w_gate, ffn1_w_up, ffn1_w_down, mix_norm_g, w_in, lru_conv_w, lru_conv_b, lru_w_a, lru_b_a, lru_w_x, lru_b_x, lru_lambda, lru_out_norm_g, ssd_conv_w, ssd_conv_b, ssd_dt_bias, ssd_a_log, ssd_d, ssd_out_norm_g, w_out, xattn_norm_g, mem_norm_g, xattn_w_q, xattn_w_k, xattn_w_v, xattn_w_o, ffn2_norm_g, ffn2_w_gate, ffn2_w_up, ffn2_w_down, final_norm_g):
    depth = ffn1_norm_g.shape[0]
    assert depth == 1
    bp, tp, d = x_prompt.shape
    bs, ts, _ = x_sample.shape
    mp = bp * tp
    ms = bs * ts
    assert tp % ROWS == 0 and ROWS % ts == 0 and ts == SUBLANES and mp % ROWS == 0
    l = 0

    w_in_bf = w_in[l].astype(BF16)
    w_in_b = jnp.pad(w_in_bf[:, D_A:], ((0, 0), (0, D_B_PAD - D_B)))
    mixer_weights = (
        lru_conv_w[l], _row(lru_conv_b[l]), lru_w_a[l].astype(BF16), lru_w_x[l].astype(BF16),
        _row(lru_b_a[l]), _row(lru_b_x[l]), _row(lru_lambda[l]), _row(lru_out_norm_g[l]),
        ssd_conv_w[l], _row(ssd_conv_b[l]), _pad_lanes(ssd_dt_bias[l]), _pad_lanes(ssd_a_log[l]),
        _row(jnp.repeat(ssd_d[l], SSD_HEADDIM)), _row(ssd_out_norm_g[l]),
    )

    (x1,) = _ffn((x_prompt.reshape(mp, d), x_sample.reshape(ms, d)), _row(ffn1_norm_g[l]),
                 ffn1_w_gate[l].astype(BF16), ffn1_w_up[l].astype(BF16), ffn1_w_down[l].astype(BF16),
                 _row(final_norm_g), out_rows=(mp + ms,), final_norm=False)
    mix_g = _row(mix_norm_g[l])
    proj_b = _norm_matmul(x1, mix_g, w_in_b, out_dtype=F32, name="in_proj_b")
    proj_a_s = _norm_matmul(x1, mix_g, w_in_bf, out_dtype=F32, name="in_proj_a_sample",
                            row_offset=mp, n_cols=D_A)

    mixed_p, p_lc, p_lh, p_sc, p_sh = _mixer_prompt(x1, mix_g, w_in_bf, proj_b, mixer_weights,
                                                    batch=bp, seq_len=tp)

    pad_rows = ((0, 0), (SUBLANES - (CONV_WIDTH - 1), 0), (0, 0))
    lprev = jnp.pad(state_lru_conv[l], pad_rows).reshape(ms, LRU_WIDTH)
    sprev = jnp.pad(state_ssd_conv[l], pad_rows).reshape(ms, SSD_CONV_DIM)
    mixed_s, s_lc, s_lh, s_sc, s_sh = _mixer_sample(
        proj_a_s, proj_b, lprev, sprev, state_lru_h[l], state_ssd_h[l].reshape(bs, SSD_WIDTH, SSD_STATE),
        mixer_weights, batch=bs, seq_len=ts, row_block_offset=mp // ROWS)

    x2 = _matmul_residual(mixed_p, mixed_s, w_out[l].astype(BF16), x1, tn=OUT_PROJ_TN, name="out_proj")

    q = _norm_matmul(x2, _row(xattn_norm_g[l]), xattn_w_q[l].astype(BF16), out_dtype=BF16, name="q_proj")
    mem = mem_prompt.reshape(bp * N_MEM, d)
    mk = _norm_matmul(mem, _row(mem_norm_g[l]), xattn_w_k[l].astype(BF16), out_dtype=F32, name="mem_k")
    mv = _norm_matmul(mem, _row(mem_norm_g[l]), xattn_w_v[l].astype(BF16), out_dtype=F32, name="mem_v")
    o_p = _xattn_prompt(q, mk.reshape(bp, N_MEM, d), mv.reshape(bp, N_MEM, d), seq_len=tp, tq=XATTN_TQ)
    o_s = _xattn_sample(q, cache_mem_k[l], cache_mem_v[l], seq_len=ts, n_seq=XATTN_CACHE_SEQS,
                        row_block_offset=mp // (XATTN_CACHE_SEQS * ts))
    x3 = _matmul_residual(o_p, o_s, xattn_w_o[l].astype(BF16), x2, tn=PROJ_TN, name="o_proj")

    y_p, y_s = _ffn((x3,), _row(ffn2_norm_g[l]), ffn2_w_gate[l].astype(BF16), ffn2_w_up[l].astype(BF16),
                    ffn2_w_down[l].astype(BF16), _row(final_norm_g), out_rows=(mp, ms), final_norm=True)

    y_prompt = y_p.reshape(bp, tp, d)
    y_sample = y_s.reshape(bs, ts, d)
    hshape = (SSD_HEADS, SSD_HEADDIM, SSD_STATE)
    return (y_prompt, y_sample,
            p_lc[None], p_lh.reshape(1, bp, LRU_WIDTH), p_sc[None], p_sh.reshape((1, bp) + hshape),
            mk.reshape(1, bp, N_MEM, XATTN_HEADS, XATTN_HEAD_DIM),
            mv.reshape(1, bp, N_MEM, XATTN_HEADS, XATTN_HEAD_DIM),
            s_lc[None], s_lh[None], s_sc[None], s_sh.reshape((1, bs) + hshape))
```

```python
import functools

import jax
import jax.numpy as jnp
from jax import lax
from jax.experimental import pallas as pl
from jax.experimental.pallas import tpu as pltpu

F32 = jnp.float32
BF16 = jnp.bfloat16

D_MODEL = 2048
LRU_WIDTH = 2048
LRU_HEADS = 16
LRU_C = 8.0
CONV_WIDTH = 4
SSD_WIDTH = 2048
SSD_HEADDIM = 64
SSD_HEADS = 32
SSD_GROUPS = 2
SSD_STATE = 128
SSD_CONV_DIM = SSD_WIDTH + 2 * SSD_GROUPS * SSD_STATE
D_IN = 3 * 2048 + SSD_CONV_DIM + SSD_HEADS
N_MEM = 256
XATTN_HEADS = 4
XATTN_HEAD_DIM = 512
EPS = 1e-6

LANES = 128
SUBLANES = 8
VMEM_LIMIT_BYTES = 56 * 1024 * 1024

D_A = 3 * 2048
W_A_TILE = 512
COL_XLRU = 0
COL_GLRU = 2048
COL_Z = 4096
D_B = SSD_CONV_DIM + SSD_HEADS
D_B_PAD = 3072
COL_XBC = 0
COL_DT = SSD_CONV_DIM
ROWS = 128
GROUP_COLS = SSD_WIDTH // SSD_GROUPS

FFN_TM, FFN_TF = 512, 512
PROJ_TM, PROJ_TN = 1024, 1024
OUT_PROJ_TN = 512
XATTN_TQ = 1024
XATTN_CACHE_SEQS = 4
SAMPLE_STATE_STEPS = 4


def _cparams(n_axes):
    return pltpu.CompilerParams(
        dimension_semantics=("arbitrary",) * n_axes,
        vmem_limit_bytes=VMEM_LIMIT_BYTES)


def _rmsnorm_rows(x, g):
    ms = jnp.mean(x * x, axis=-1, keepdims=True)
    return (x * lax.rsqrt(ms + EPS)) * g


def _softplus(x):
    return jnp.maximum(x, 0.0) + jnp.log1p(jnp.exp(-jnp.abs(x)))


def _silu(x):
    return x * jax.nn.sigmoid(x)


def _when_rows(i, n_a, fn_a, fn_b, extra=None):
    in_a = i < n_a
    in_b = i >= n_a
    if extra is not None:
        in_a = jnp.logical_and(in_a, extra)
        in_b = jnp.logical_and(in_b, extra)
    pl.when(in_a)(fn_a)
    pl.when(in_b)(fn_b)


def _ffn_kernel(n_a, split_in, split_out, final_norm, *refs):
    refs = list(refs)
    xa_ref = refs.pop(0)
    xb_ref = refs.pop(0) if split_in else xa_ref
    g_ref, wg_ref, wu_ref, wd_ref, gf_ref = refs[:5]
    oa_ref = refs[5]
    ob_ref = refs[6] if split_out else oa_ref
    if split_out:
        h_scr, acc_scr = refs[-2:]
    else:
        h_scr, acc_scr = refs[-1], oa_ref
    i = pl.program_id(0)
    j = pl.program_id(1)

    def prologue(x_ref):
        def run():
            x = x_ref[...]
            h_scr[...] = _rmsnorm_rows(x, g_ref[...]).astype(BF16)
            acc_scr[...] = x
        return run

    if split_in:
        _when_rows(i, n_a, prologue(xa_ref), prologue(xb_ref), extra=(j == 0))
    else:
        pl.when(j == 0)(prologue(xa_ref))

    h = h_scr[...]
    gate = jnp.dot(h, wg_ref[...], preferred_element_type=F32)
    up = jnp.dot(h, wu_ref[...], preferred_element_type=F32)
    a = (0.5 * _silu(gate) * up).astype(BF16)
    acc_scr[...] += jnp.dot(a, wd_ref[...], preferred_element_type=F32)

    def epilogue(o_ref):
        def run():
            res = acc_scr[...]
            if final_norm:
                res = _rmsnorm_rows(res, gf_ref[...])
            o_ref[...] = res
        return run

    last = j == pl.num_programs(1) - 1
    if split_out:
        _when_rows(i, n_a, epilogue(oa_ref), epilogue(ob_ref), extra=last)
    elif final_norm:
        pl.when(last)(epilogue(oa_ref))


def _ffn(xs, g, wg, wu, wd, gf, *, out_rows, final_norm):
    tm, tf = FFN_TM, FFN_TF
    split_in = len(xs) == 2
    split_out = len(out_rows) == 2
    d = xs[0].shape[1]
    f = wg.shape[1]
    m = sum(x.shape[0] for x in xs)
    assert m == sum(out_rows) and all(x.shape[0] % tm == 0 for x in xs) and all(r % tm == 0 for r in out_rows)
    n_a = (xs[0].shape[0] if split_in else out_rows[0]) // tm
    if split_in and split_out:
        assert xs[0].shape[0] == out_rows[0]

    def first(i, j):
        return (jnp.minimum(i, n_a - 1), 0)

    def second(i, j):
        return (jnp.maximum(i - n_a, 0), 0)

    def whole(i, j):
        return (i, 0)

    x_specs = ([pl.BlockSpec((tm, d), first), pl.BlockSpec((tm, d), second)] if split_in
               else [pl.BlockSpec((tm, d), whole)])
    o_specs = ([pl.BlockSpec((tm, d), first), pl.BlockSpec((tm, d), second)] if split_out
               else [pl.BlockSpec((tm, d), whole)])
    return pl.pallas_call(
        functools.partial(_ffn_kernel, n_a, split_in, split_out, final_norm),
        grid=(m // tm, f // tf),
        in_specs=x_specs + [
            pl.BlockSpec((1, d), lambda i, j: (0, 0)),
            pl.BlockSpec((d, tf), lambda i, j: (0, j)),
            pl.BlockSpec((d, tf), lambda i, j: (0, j)),
            pl.BlockSpec((tf, d), lambda i, j: (j, 0)),
            pl.BlockSpec((1, d), lambda i, j: (0, 0)),
        ],
        out_specs=o_specs,
        out_shape=[jax.ShapeDtypeStruct((r, d), F32) for r in out_rows],
        scratch_shapes=[pltpu.VMEM((tm, d), BF16)] + ([pltpu.VMEM((tm, d), F32)] if split_out else []),
        compiler_params=_cparams(2),
        name="ffn_final" if final_norm else "ffn",
    )(*xs, g, wg, wu, wd, gf)


def _norm_matmul_kernel(x_ref, g_ref, w_ref, o_ref, h_scr):
    @pl.when(pl.program_id(1) == 0)
    def _():
        h_scr[...] = _rmsnorm_rows(x_ref[...], g_ref[...]).astype(BF16)

    o_ref[...] = jnp.dot(h_scr[...], w_ref[...], preferred_element_type=F32).astype(o_ref.dtype)


def _norm_matmul(x, g, w, *, out_dtype, name, row_offset=0, n_cols=None):
    tm, tn = PROJ_TM, PROJ_TN
    k = x.shape[1]
    m = x.shape[0] - row_offset
    n = w.shape[1] if n_cols is None else n_cols
    assert m % tm == 0 and row_offset % tm == 0 and n % tn == 0
    first = row_offset // tm
    return pl.pallas_call(
        _norm_matmul_kernel,
        grid=(m // tm, n // tn),
        in_specs=[
            pl.BlockSpec((tm, k), lambda i, j: (first + i, 0)),
            pl.BlockSpec((1, k), lambda i, j: (0, 0)),
            pl.BlockSpec((k, tn), lambda i, j: (0, j)),
        ],
        out_specs=pl.BlockSpec((tm, tn), lambda i, j: (i, j)),
        out_shape=jax.ShapeDtypeStruct((m, n), out_dtype),
        scratch_shapes=[pltpu.VMEM((tm, k), BF16)],
        compiler_params=_cparams(2),
        name=name,
    )(x, g, w)


def _matmul_residual_kernel(n_a, aa_ref, ab_ref, w_ref, r_ref, o_ref):
    def run(a_ref):
        def body():
            o_ref[...] = r_ref[...] + jnp.dot(a_ref[...], w_ref[...], preferred_element_type=F32)
        return body

    _when_rows(pl.program_id(0), n_a, run(aa_ref), run(ab_ref))


def _matmul_residual(a_first, a_second, w, res, *, tn, name):
    tm = PROJ_TM
    k = a_first.shape[1]
    m = a_first.shape[0] + a_second.shape[0]
    n = w.shape[1]
    assert a_first.shape[0] % tm == 0 and a_second.shape[0] % tm == 0 and res.shape == (m, n)
    n_a = a_first.shape[0] // tm
    return pl.pallas_call(
        functools.partial(_matmul_residual_kernel, n_a),
        grid=(m // tm, n // tn),
        in_specs=[
            pl.BlockSpec((tm, k), lambda i, j: (jnp.minimum(i, n_a - 1), 0)),
            pl.BlockSpec((tm, k), lambda i, j: (jnp.maximum(i - n_a, 0), 0)),
            pl.BlockSpec((k, tn), lambda i, j: (0, j)),
            pl.BlockSpec((tm, tn), lambda i, j: (i, j)),
        ],
        out_specs=pl.BlockSpec((tm, tn), lambda i, j: (i, j)),
        out_shape=jax.ShapeDtypeStruct((m, n), F32),
        compiler_params=_cparams(2),
        name=name,
    )(a_first, a_second, w, res)


def _xattn_kernel(n_seq, tq, q_ref, k_ref, v_ref, o_ref):
    scale = XATTN_HEAD_DIM ** -0.5
    for s in range(n_seq):
        rows = slice(s * tq, (s + 1) * tq)
        for h in range(XATTN_HEADS):
            cols = slice(h * XATTN_HEAD_DIM, (h + 1) * XATTN_HEAD_DIM)
            q = q_ref[rows, cols]
            k = k_ref[s, :, cols].astype(BF16)
            v = v_ref[s, :, cols].astype(BF16)
            sc = lax.dot_general(q, k, (((1,), (1,)), ((), ())),
                                 preferred_element_type=F32) * scale
            mx = jnp.max(sc, axis=-1, keepdims=True)
            e = jnp.exp(sc - mx)
            p = e / jnp.sum(e, axis=-1, keepdims=True)
            o = jnp.dot(p.astype(BF16), v, preferred_element_type=F32)
            o_ref[rows, cols] = o.astype(BF16)


def _xattn_prompt(q, k, v, *, seq_len, tq):
    b = k.shape[0]
    nt = seq_len // tq
    d = q.shape[1]
    return pl.pallas_call(
        functools.partial(_xattn_kernel, 1, tq),
        grid=(b, nt),
        in_specs=[
            pl.BlockSpec((tq, d), lambda i, j: (i * nt + j, 0)),
            pl.BlockSpec((1, N_MEM, d), lambda i, j: (i, 0, 0)),
            pl.BlockSpec((1, N_MEM, d), lambda i, j: (i, 0, 0)),
        ],
        out_specs=pl.BlockSpec((tq, d), lambda i, j: (i * nt + j, 0)),
        out_shape=jax.ShapeDtypeStruct((b * seq_len, d), BF16),
        compiler_params=_cparams(2),
        name="xattn_prompt",
    )(q, k, v)


def _xattn_cache_kernel(n_seq, tq, q_ref, k_ref, v_ref, o_ref):
    scale = XATTN_HEAD_DIM ** -0.5
    n_rows = XATTN_HEADS * tq
    n_cols = N_MEM * XATTN_HEADS
    own = (_lane_iota((n_rows, n_cols)) % XATTN_HEADS) == (_row_iota((n_rows, n_cols)) // tq)
    qf = q_ref[...].astype(F32)
    outs = []
    for s in range(n_seq):
        qs = qf[s * tq:(s + 1) * tq]
        q4 = jnp.concatenate(
            [qs[:, h * XATTN_HEAD_DIM:(h + 1) * XATTN_HEAD_DIM] for h in range(XATTN_HEADS)],
            axis=0).astype(BF16)
        kf = k_ref[s].reshape(n_cols, XATTN_HEAD_DIM).astype(BF16)
        vf = v_ref[s].reshape(n_cols, XATTN_HEAD_DIM).astype(BF16)
        sc = lax.dot_general(q4, kf, (((1,), (1,)), ((), ())), preferred_element_type=F32) * scale
        sc = jnp.where(own, sc, -1e30)
        mx = jnp.max(sc, axis=-1, keepdims=True)
        e = jnp.exp(sc - mx)
        p = e / jnp.sum(e, axis=-1, keepdims=True)
        o4 = jnp.dot(p.astype(BF16), vf, preferred_element_type=F32)
        outs.append(jnp.concatenate([o4[h * tq:(h + 1) * tq] for h in range(XATTN_HEADS)], axis=1))
    o_ref[...] = jnp.concatenate(outs, axis=0).astype(BF16)


def _xattn_sample(q, k, v, *, seq_len, n_seq, row_block_offset):
    b = k.shape[0]
    d = q.shape[1]
    rows = n_seq * seq_len
    kv_block = (n_seq, N_MEM, XATTN_HEADS, XATTN_HEAD_DIM)
    return pl.pallas_call(
        functools.partial(_xattn_cache_kernel, n_seq, seq_len),
        grid=(b // n_seq,),
        in_specs=[
            pl.BlockSpec((rows, d), lambda i: (row_block_offset + i, 0)),
            pl.BlockSpec(kv_block, lambda i: (i, 0, 0, 0)),
            pl.BlockSpec(kv_block, lambda i: (i, 0, 0, 0)),
        ],
        out_specs=pl.BlockSpec((rows, d), lambda i: (i, 0)),
        out_shape=jax.ShapeDtypeStruct((b * seq_len, d), BF16),
        compiler_params=_cparams(1),
        name="xattn_sample",
    )(q, k, v)


def _row_iota(shape):
    return lax.broadcasted_iota(jnp.int32, shape, 0)


def _lane_iota(shape):
    return lax.broadcasted_iota(jnp.int32, shape, 1)


def _seg_cumsum(x, seg_len):
    pos = _row_iota(x.shape) % seg_len
    s = 1
    while s < seg_len:
        x = x + jnp.where(pos >= s, pltpu.roll(x, s, axis=0), 0.0)
        s *= 2
    return x


def _seg_rev_excl_cumsum(x, seg_len):
    n = x.shape[0]
    pos = _row_iota(x.shape) % seg_len
    y = jnp.where(pos < seg_len - 1, pltpu.roll(x, n - 1, axis=0), 0.0)
    s = 1
    while s < seg_len:
        y = y + jnp.where(pos < seg_len - s, pltpu.roll(y, n - s, axis=0), 0.0)
        s *= 2
    return y


def _scan8(a, b):
    shape = a.shape
    tiled = (shape[0] // SUBLANES, SUBLANES, shape[1])
    a = a.reshape(tiled)
    b = b.reshape(tiled)
    pos = lax.broadcasted_iota(jnp.int32, tiled, 1)
    for s in (1, 2, 4):
        m = pos >= s
        a_sh = pltpu.roll(a, s, axis=1)
        b_sh = pltpu.roll(b, s, axis=1)
        b = jnp.where(m, a * b_sh + b, b)
        a = jnp.where(m, a * a_sh, a)
    return a.reshape(shape), b.reshape(shape)


def _conv_taps(x, w_ref, b_ref, cols, fix):
    acc = b_ref[:, cols] + x * w_ref[CONV_WIDTH - 1:CONV_WIDTH, cols]
    for s in range(1, CONV_WIDTH):
        xs = fix(s, pltpu.roll(x, s, axis=0))
        k = CONV_WIDTH - 1 - s
        acc = acc + xs * w_ref[k:k + 1, cols]
    return acc


def _bcast_col(v, c):
    return jnp.broadcast_to(v[:, c:c + 1], (v.shape[0], LANES))


class _SideWork:
    def __init__(self, items, n_points):
        self.items = list(items)
        self.stride = max(1, n_points // max(1, len(self.items)))
        self.count = 0

    def point(self):
        if self.items and self.count % self.stride == 0:
            self.items.pop(0)()
        self.count += 1

    def flush(self):
        while self.items:
            self.items.pop(0)()


N_SIDE_POINTS = LRU_HEADS + SSD_CONV_DIM // LANES + SSD_HEADS // 2


def _lru_part(sample, pa_ref, prev_fix_lru, h0_rows, hcar_ref,
              lcw_ref, lcb_ref, wa_ref, wx_ref, ba_ref, bx_ref, lam_ref, lg_ref,
              mixed_ref, hs_scr, yl_scr, side=None):
    ssq = jnp.zeros((ROWS, 1), F32)
    for hd in range(LRU_HEADS):
        if side is not None:
            side.point()
        cols = slice(hd * LANES, (hd + 1) * LANES)
        xl = pa_ref[:, COL_XLRU + hd * LANES:COL_XLRU + (hd + 1) * LANES]
        u = _conv_taps(xl, lcw_ref, lcb_ref, cols, functools.partial(prev_fix_lru, cols))
        ub = u.astype(BF16)
        r = jax.nn.sigmoid(jnp.dot(ub, wa_ref[hd], preferred_element_type=F32) + ba_ref[:, cols])
        i = jax.nn.sigmoid(jnp.dot(ub, wx_ref[hd], preferred_element_type=F32) + bx_ref[:, cols])
        log_a = (-LRU_C * r) * _softplus(-lam_ref[:, cols])
        a = jnp.exp(log_a)
        th = jnp.tanh(log_a)
        beta = jnp.sqrt((-2.0 * th) / (1.0 - th))
        bb = (beta * i) * u
        a_cum, b_cum = _scan8(a, bb)
        if sample:
            h = a_cum * h0_rows(cols) + b_cum
        else:
            carry = hcar_ref[:, cols]
            parts = []
            for t in range(ROWS // SUBLANES):
                rs = slice(t * SUBLANES, (t + 1) * SUBLANES)
                ht = a_cum[rs] * carry + b_cum[rs]
                parts.append(ht)
                carry = ht[SUBLANES - 1:SUBLANES]
            h = jnp.concatenate(parts, axis=0)
            hcar_ref[:, cols] = carry
        hs_scr[:, cols] = h
        g = jax.nn.gelu(pa_ref[:, COL_GLRU + hd * LANES:COL_GLRU + (hd + 1) * LANES])
        yl = h * g
        ssq = ssq + jnp.sum(yl * yl, axis=-1, keepdims=True)
        yl_scr[:, cols] = yl
    inv = lax.rsqrt(ssq * (1.0 / LRU_WIDTH) + EPS)
    for hd in range(LRU_HEADS):
        cols = slice(hd * LANES, (hd + 1) * LANES)
        mixed_ref[:, cols] = ((yl_scr[:, cols] * inv) * lg_ref[:, cols]).astype(BF16)


def _ssd_front(sample, pb_ref, prev_fix_ssd, scw_ref, scb_ref, dtb_ref, alog_ref, dexp_ref,
               xc_scr, xw_scr, ecs_scr, y_scr, xt_scr, dec_scr, c_scr, b_scr, side=None):
    seg = SUBLANES if sample else ROWS
    n_seq = ROWS // seg
    n_conv_tiles = SSD_CONV_DIM // LANES

    def side_point():
        if side is not None:
            side.point()

    for t in range(n_conv_tiles):
        side_point()
        cols = slice(t * LANES, (t + 1) * LANES)
        xb = pb_ref[:, COL_XBC + t * LANES:COL_XBC + (t + 1) * LANES]
        v = _conv_taps(xb, scw_ref, scb_ref, cols, functools.partial(prev_fix_ssd, cols))
        xc_scr[:, cols] = _silu(v)

    dt = _softplus(pb_ref[:, COL_DT:COL_DT + LANES] + dtb_ref[...])
    a_neg = -jnp.exp(alog_ref[...])
    d_a = dt * a_neg
    cs = _seg_cumsum(d_a, seg)
    rcs = _seg_rev_excl_cumsum(d_a, seg)
    todt = jnp.exp(rcs) * dt
    dec_tot = jnp.exp(cs + rcs)
    cs_t = cs.T
    dt_t = dt.T
    dec_t = dec_tot.T
    for s in range(n_seq):
        dec_scr[s] = jnp.broadcast_to(dec_t[0:SSD_HEADS, s * seg:s * seg + 1], (SSD_HEADS, LANES))

    ri = _row_iota((ROWS, ROWS))
    ci = _lane_iota((ROWS, ROWS))
    causal = ri >= ci
    if sample:
        causal = jnp.logical_and(causal, (ri // seg) == (ci // seg))
    lo = _lane_iota((ROWS, LANES)) < SSD_HEADDIM

    b_off = SSD_WIDTH
    c_off = SSD_WIDTH + SSD_GROUPS * SSD_STATE
    for g in range(SSD_GROUPS):
        gcols = slice(g * SSD_STATE, (g + 1) * SSD_STATE)
        bg = xc_scr[:, b_off + g * SSD_STATE:b_off + (g + 1) * SSD_STATE].astype(BF16)
        cg = xc_scr[:, c_off + g * SSD_STATE:c_off + (g + 1) * SSD_STATE].astype(BF16)
        c_scr[:, gcols] = xc_scr[:, c_off + g * SSD_STATE:c_off + (g + 1) * SSD_STATE]
        b_scr[:, gcols] = bg
        cb = lax.dot_general(cg, bg, (((1,), (1,)), ((), ())), preferred_element_type=F32)
        heads_per_group = SSD_HEADS // SSD_GROUPS
        for jp in range(heads_per_group // 2):
            side_point()
            h0 = g * heads_per_group + 2 * jp
            h1 = h0 + 1
            cols = slice(h0 * SSD_HEADDIM, h0 * SSD_HEADDIM + LANES)
            x_pair = xc_scr[:, cols]
            ws = []
            cs_cols = []
            for hh in (h0, h1):
                cs_col = _bcast_col(cs, hh)
                cs_cols.append(cs_col)
                dec = jnp.where(causal, jnp.exp(cs_col - cs_t[hh:hh + 1, :]), 0.0)
                ws.append(((cb * dec) * dt_t[hh:hh + 1, :]).astype(BF16))
            lhs = jnp.concatenate(ws, axis=1)
            rhs = jnp.concatenate([jnp.where(lo, x_pair, 0.0), jnp.where(lo, 0.0, x_pair)],
                                  axis=0).astype(BF16)
            y_diag = jnp.dot(lhs, rhs, preferred_element_type=F32)
            y_scr[:, cols] = y_diag + x_pair * dexp_ref[:, cols]
            xw_scr[:, cols] = x_pair * jnp.where(lo, _bcast_col(todt, h0), _bcast_col(todt, h1))
            ecs_scr[:, cols] = jnp.exp(jnp.where(lo, cs_cols[0], cs_cols[1]))
    for t in range(SSD_WIDTH // LANES):
        xt_scr[t * LANES:(t + 1) * LANES, :] = xw_scr[:, t * LANES:(t + 1) * LANES].T.astype(BF16)


def _ssd_state_step(c_rows, b_mask, s_old, dec, xt_scr):
    y_parts = []
    s_parts = []
    heads_per_group = SSD_HEADS // SSD_GROUPS
    for g in range(SSD_GROUPS):
        gcols = slice(g * SSD_STATE, (g + 1) * SSD_STATE)
        grows = slice(g * GROUP_COLS, (g + 1) * GROUP_COLS)
        sg = s_old[grows]
        y_parts.append(lax.dot_general(c_rows[:, gcols], sg.astype(BF16), (((1,), (1,)), ((), ())),
                                       preferred_element_type=F32))
        upd = jnp.dot(xt_scr[grows, :], b_mask[:, gcols], preferred_element_type=F32)
        for hh in range(heads_per_group):
            hr = slice(hh * SSD_HEADDIM, (hh + 1) * SSD_HEADDIM)
            habs = g * heads_per_group + hh
            s_parts.append(sg[hr] * dec[habs:habs + 1, :] + upd[hr])
    return jnp.concatenate(y_parts, axis=1), jnp.concatenate(s_parts, axis=0)


def _ssd_finish(pa_ref, y_scr, sg_ref, mixed_ref):
    for g in range(SSD_GROUPS):
        ssq = jnp.zeros((ROWS, 1), F32)
        n_t = GROUP_COLS // LANES
        for t in range(n_t):
            c0 = g * GROUP_COLS + t * LANES
            yz = y_scr[:, c0:c0 + LANES] * _silu(pa_ref[:, COL_Z + c0:COL_Z + c0 + LANES])
            y_scr[:, c0:c0 + LANES] = yz
            ssq = ssq + jnp.sum(yz * yz, axis=-1, keepdims=True)
        inv = lax.rsqrt(ssq * (1.0 / GROUP_COLS) + EPS)
        for t in range(n_t):
            c0 = g * GROUP_COLS + t * LANES
            mixed_ref[:, LRU_WIDTH + c0:LRU_WIDTH + c0 + LANES] = (
                (y_scr[:, c0:c0 + LANES] * inv) * sg_ref[:, c0:c0 + LANES]).astype(BF16)


def _mixer_prompt_kernel(x_ref, xn_ref, mg_ref, *rest):
    n_slabs = D_A // W_A_TILE
    wina_refs = rest[:n_slabs]
    (pb_ref,
     lcw_ref, lcb_ref, wa_ref, wx_ref, ba_ref, bx_ref, lam_ref, lg_ref,
     scw_ref, scb_ref, dtb_ref, alog_ref, dexp_ref, sg_ref,
     mixed_ref, lconv_ref, lh_ref, sconv_ref, sh_ref,
     pa_scr, hn_scr, hcar_ref, ltail_scr, stail_scr, s_scr,
     hs_scr, yl_scr, xc_scr, xw_scr, ecs_scr, y_scr, xt_scr, dec_scr, c_scr, b_scr) = rest[n_slabs:]
    c = pl.program_id(1)
    step = pl.program_id(0) * pl.num_programs(1) + c
    slot = step % 2

    @pl.when(c == 0)
    def _():
        hcar_ref[...] = jnp.zeros_like(hcar_ref)
        ltail_scr[...] = jnp.zeros_like(ltail_scr)
        stail_scr[...] = jnp.zeros_like(stail_scr)
        s_scr[...] = jnp.zeros_like(s_scr)

    @pl.when(step == 0)
    def _():
        hn = _rmsnorm_rows(x_ref[...], mg_ref[...]).astype(BF16)
        for j in range(n_slabs):
            pa_scr[0, :, j * W_A_TILE:(j + 1) * W_A_TILE] = jnp.dot(
                hn, wina_refs[j][...], preferred_element_type=F32)

    hn_scr[...] = _rmsnorm_rows(xn_ref[...], mg_ref[...]).astype(BF16)
    pa_next = pa_scr.at[1 - slot]
    pa_cur = pa_scr.at[slot]

    def project_slab(j):
        def run():
            pa_next[:, j * W_A_TILE:(j + 1) * W_A_TILE] = jnp.dot(
                hn_scr[...], wina_refs[j][...], preferred_element_type=F32)
        return run

    side = _SideWork([project_slab(j) for j in range(n_slabs)], N_SIDE_POINTS)

    pos8 = _row_iota((SUBLANES, LANES))

    def make_fix(tail_scr):
        def fix(cols, s, rolled):
            first = jnp.where(pos8 < s, pltpu.roll(tail_scr[:, cols], s, axis=0), rolled[0:SUBLANES])
            return jnp.concatenate([first, rolled[SUBLANES:]], axis=0)
        return fix

    _lru_part(False, pa_cur, make_fix(ltail_scr), None, hcar_ref,
              lcw_ref, lcb_ref, wa_ref, wx_ref, ba_ref, bx_ref, lam_ref, lg_ref,
              mixed_ref, hs_scr, yl_scr, side=side)
    _ssd_front(False, pb_ref, make_fix(stail_scr), scw_ref, scb_ref, dtb_ref, alog_ref, dexp_ref,
               xc_scr, xw_scr, ecs_scr, y_scr, xt_scr, dec_scr, c_scr, b_scr, side=side)
    side.flush()

    ltail_scr[...] = pa_cur[ROWS - SUBLANES:ROWS, COL_XLRU:COL_XLRU + LRU_WIDTH]
    stail_scr[...] = pb_ref[ROWS - SUBLANES:ROWS, COL_XBC:COL_XBC + SSD_CONV_DIM]

    y_off, s_new = _ssd_state_step(c_scr[...].astype(BF16), b_scr[...], s_scr[...], dec_scr[0], xt_scr)
    s_scr[...] = s_new
    for t in range(SSD_WIDTH // LANES):
        cols = slice(t * LANES, (t + 1) * LANES)
        y_scr[:, cols] = y_scr[:, cols] + ecs_scr[:, cols] * y_off[:, cols]
    _ssd_finish(pa_cur, y_scr, sg_ref, mixed_ref)

    @pl.when(c == pl.num_programs(1) - 1)
    def _():
        lconv_ref[0] = pa_cur[ROWS - (CONV_WIDTH - 1):ROWS, COL_XLRU:COL_XLRU + LRU_WIDTH]
        sconv_ref[0] = pb_ref[ROWS - (CONV_WIDTH - 1):ROWS, COL_XBC:COL_XBC + SSD_CONV_DIM]
        lh_ref[0] = hcar_ref[...]
        sh_ref[0] = s_scr[...]


def _mixer_sample_kernel(n_inner, pa_ref, pb_ref, lprev_ref, sprev_ref, h0_ref, sin_ref,
                         lcw_ref, lcb_ref, wa_ref, wx_ref, ba_ref, bx_ref, lam_ref, lg_ref,
                         scw_ref, scb_ref, dtb_ref, alog_ref, dexp_ref, sg_ref,
                         mixed_ref, lconv_ref, lh_ref, sconv_ref, sh_ref,
                         hs_scr, yl_scr, xc_scr, xw_scr, ecs_scr, y_scr, xt_scr, dec_scr, c_scr, b_scr):
    i = pl.program_id(1)
    seg = SUBLANES
    n_seq = ROWS // seg
    seq_per_step = n_seq // n_inner

    @pl.when(i == 0)
    def _():
        pos = _row_iota((ROWS, LANES)) % seg

        def make_fix(prev_ref):
            def fix(cols, s, rolled):
                return jnp.where(pos < s, pltpu.roll(prev_ref[:, cols], ROWS - seg + s, axis=0), rolled)
            return fix

        def h0_rows(cols):
            return jnp.concatenate(
                [jnp.broadcast_to(h0_ref[q:q + 1, cols], (seg, LANES)) for q in range(n_seq)], axis=0)

        _ssd_front(True, pb_ref, make_fix(sprev_ref), scw_ref, scb_ref, dtb_ref, alog_ref, dexp_ref,
                   xc_scr, xw_scr, ecs_scr, y_scr, xt_scr, dec_scr, c_scr, b_scr)
        _lru_part(True, pa_ref, make_fix(lprev_ref), h0_rows, None,
                  lcw_ref, lcb_ref, wa_ref, wx_ref, ba_ref, bx_ref, lam_ref, lg_ref,
                  mixed_ref, hs_scr, yl_scr)
        for sq in range(n_seq):
            tail = slice((sq + 1) * seg - (CONV_WIDTH - 1), (sq + 1) * seg)
            lconv_ref[sq] = pa_ref[tail, COL_XLRU:COL_XLRU + LRU_WIDTH]
            sconv_ref[sq] = pb_ref[tail, COL_XBC:COL_XBC + SSD_CONV_DIM]
            lh_ref[sq:sq + 1, :] = hs_scr[(sq + 1) * seg - 1:(sq + 1) * seg, :]

    rgroup = _row_iota((ROWS, 2 * SSD_STATE)) // seg
    for jj in range(seq_per_step):
        q = i * seq_per_step + jj
        r0 = pl.multiple_of(q * seg, seg)
        c_rows = c_scr[pl.ds(r0, seg), :].astype(BF16)
        b_all = b_scr[...]
        b_mask = jnp.where(rgroup == q, b_all, jnp.zeros_like(b_all))
        y_off, s_new = _ssd_state_step(c_rows, b_mask, sin_ref[jj], dec_scr[q], xt_scr)
        sh_ref[jj] = s_new
        y_scr[pl.ds(r0, seg), :] = y_scr[pl.ds(r0, seg), :] + ecs_scr[pl.ds(r0, seg), :] * y_off

    @pl.when(i == n_inner - 1)
    def _():
        _ssd_finish(pa_ref, y_scr, sg_ref, mixed_ref)


def _mixer_scratch(n_dec):
    return [
        pltpu.VMEM((ROWS, LRU_WIDTH), F32),
        pltpu.VMEM((ROWS, LRU_WIDTH), F32),
        pltpu.VMEM((ROWS, SSD_CONV_DIM), F32),
        pltpu.VMEM((ROWS, SSD_WIDTH), F32),
        pltpu.VMEM((ROWS, SSD_WIDTH), F32),
        pltpu.VMEM((ROWS, SSD_WIDTH), F32),
        pltpu.VMEM((SSD_WIDTH, ROWS), BF16),
        pltpu.VMEM((n_dec, SSD_HEADS, LANES), F32),
        pltpu.VMEM((ROWS, 2 * SSD_STATE), F32),
        pltpu.VMEM((ROWS, 2 * SSD_STATE), BF16),
    ]


def _full_spec(shape, n_grid):
    zeros = (0,) * len(shape)
    if n_grid == 2:
        return pl.BlockSpec(shape, lambda i, j: zeros)
    return pl.BlockSpec(shape, lambda i: zeros)


def _mixer_weight_specs(n_grid):
    return [
        _full_spec((CONV_WIDTH, LRU_WIDTH), n_grid),
        _full_spec((1, LRU_WIDTH), n_grid),
        _full_spec((LRU_HEADS, LANES, LANES), n_grid),
        _full_spec((LRU_HEADS, LANES, LANES), n_grid),
        _full_spec((1, LRU_WIDTH), n_grid),
        _full_spec((1, LRU_WIDTH), n_grid),
        _full_spec((1, LRU_WIDTH), n_grid),
        _full_spec((1, LRU_WIDTH), n_grid),
        _full_spec((CONV_WIDTH, SSD_CONV_DIM), n_grid),
        _full_spec((1, SSD_CONV_DIM), n_grid),
        _full_spec((1, LANES), n_grid),
        _full_spec((1, LANES), n_grid),
        _full_spec((1, SSD_WIDTH), n_grid),
        _full_spec((1, SSD_WIDTH), n_grid),
    ]


def _mixer_prompt(x1, mix_g, w_in_a, proj_b, weights, *, batch, seq_len):
    nc = seq_len // ROWS
    n_slabs = D_A // W_A_TILE

    def slab_spec(j):
        return pl.BlockSpec((D_MODEL, W_A_TILE), lambda b, c: (0, j), pipeline_mode=pl.Buffered(1))

    return pl.pallas_call(
        _mixer_prompt_kernel,
        grid=(batch, nc),
        in_specs=[
            pl.BlockSpec((ROWS, D_MODEL), lambda b, c: (b * nc + c, 0)),
            pl.BlockSpec((ROWS, D_MODEL), lambda b, c: (jnp.minimum(b * nc + c + 1, batch * nc - 1), 0)),
            pl.BlockSpec((1, D_MODEL), lambda b, c: (0, 0)),
        ] + [slab_spec(j) for j in range(n_slabs)] + [
            pl.BlockSpec((ROWS, D_B_PAD), lambda b, c: (b * nc + c, 0)),
        ] + _mixer_weight_specs(2),
        out_specs=[
            pl.BlockSpec((ROWS, LRU_WIDTH + SSD_WIDTH), lambda b, c: (b * nc + c, 0)),
            pl.BlockSpec((1, CONV_WIDTH - 1, LRU_WIDTH), lambda b, c: (b, 0, 0)),
            pl.BlockSpec((1, 1, LRU_WIDTH), lambda b, c: (b, 0, 0)),
            pl.BlockSpec((1, CONV_WIDTH - 1, SSD_CONV_DIM), lambda b, c: (b, 0, 0)),
            pl.BlockSpec((1, SSD_WIDTH, SSD_STATE), lambda b, c: (b, 0, 0)),
        ],
        out_shape=[
            jax.ShapeDtypeStruct((batch * seq_len, LRU_WIDTH + SSD_WIDTH), BF16),
            jax.ShapeDtypeStruct((batch, CONV_WIDTH - 1, LRU_WIDTH), F32),
            jax.ShapeDtypeStruct((batch, 1, LRU_WIDTH), F32),
            jax.ShapeDtypeStruct((batch, CONV_WIDTH - 1, SSD_CONV_DIM), F32),
            jax.ShapeDtypeStruct((batch, SSD_WIDTH, SSD_STATE), F32),
        ],
        scratch_shapes=[
            pltpu.VMEM((2, ROWS, D_A), F32),
            pltpu.VMEM((ROWS, D_MODEL), BF16),
            pltpu.VMEM((1, LRU_WIDTH), F32),
            pltpu.VMEM((SUBLANES, LRU_WIDTH), F32),
            pltpu.VMEM((SUBLANES, SSD_CONV_DIM), F32),
            pltpu.VMEM((SSD_WIDTH, SSD_STATE), F32),
        ] + _mixer_scratch(1),
        compiler_params=_cparams(2),
        name="mixer_prompt",
    )(x1, x1, mix_g, *([w_in_a] * n_slabs), proj_b, *weights)


def _mixer_sample(proj_a, proj_b, lprev, sprev, h0, s_in, weights, *, batch, seq_len, row_block_offset,
                  n_inner=SAMPLE_STATE_STEPS):
    n_seq = ROWS // seq_len
    n_outer = batch // n_seq
    sps = n_seq // n_inner
    return pl.pallas_call(
        functools.partial(_mixer_sample_kernel, n_inner),
        grid=(n_outer, n_inner),
        in_specs=[
            pl.BlockSpec((ROWS, D_A), lambda o, i: (o, 0)),
            pl.BlockSpec((ROWS, D_B_PAD), lambda o, i: (row_block_offset + o, 0)),
            pl.BlockSpec((ROWS, LRU_WIDTH), lambda o, i: (o, 0)),
            pl.BlockSpec((ROWS, SSD_CONV_DIM), lambda o, i: (o, 0)),
            pl.BlockSpec((n_seq, LRU_WIDTH), lambda o, i: (o, 0)),
            pl.BlockSpec((sps, SSD_WIDTH, SSD_STATE), lambda o, i: (o * n_inner + i, 0, 0)),
        ] + _mixer_weight_specs(2),
        out_specs=[
            pl.BlockSpec((ROWS, LRU_WIDTH + SSD_WIDTH), lambda o, i: (o, 0)),
            pl.BlockSpec((n_seq, CONV_WIDTH - 1, LRU_WIDTH), lambda o, i: (o, 0, 0)),
            pl.BlockSpec((n_seq, LRU_WIDTH), lambda o, i: (o, 0)),
            pl.BlockSpec((n_seq, CONV_WIDTH - 1, SSD_CONV_DIM), lambda o, i: (o, 0, 0)),
            pl.BlockSpec((sps, SSD_WIDTH, SSD_STATE), lambda o, i: (o * n_inner + i, 0, 0)),
        ],
        out_shape=[
            jax.ShapeDtypeStruct((batch * seq_len, LRU_WIDTH + SSD_WIDTH), BF16),
            jax.ShapeDtypeStruct((batch, CONV_WIDTH - 1, LRU_WIDTH), F32),
            jax.ShapeDtypeStruct((batch, LRU_WIDTH), F32),
            jax.ShapeDtypeStruct((batch, CONV_WIDTH - 1, SSD_CONV_DIM), F32),
            jax.ShapeDtypeStruct((batch, SSD_WIDTH, SSD_STATE), F32),
        ],
        scratch_shapes=_mixer_scratch(n_seq),
        compiler_params=_cparams(2),
        name="mixer_sample",
    )(proj_a, proj_b, lprev, sprev, h0, s_in, *weights)


def _row(v):
    return v.reshape(1, -1).astype(F32)


def _pad_lanes(v):
    v = v.reshape(1, -1).astype(F32)
    return jnp.pad(v, ((0, 0), (0, LANES - v.shape[1])))


def kernel(x_prompt, mem_prompt, x_sample, cache_mem_k, cache_mem_v, state_lru_conv, state_lru_h, state_ssd_conv, state_ssd_h, ffn1_norm_g, ffn1_w_gate, ffn1_w_up, ffn1_w_down, mix_norm_g, w_in, lru_conv_w, lru_conv_b, lru_w_a, lru_b_a, lru_w_x, lru_b_x, lru_lambda, lru_out_norm_g, ssd_conv_w, ssd_conv_b, ssd_dt_bias, ssd_a_log, ssd_d, ssd_out_norm_g, w_out, xattn_norm_g, mem_norm_g, xattn_w_q, xattn_w_k, xattn_w_v, xattn_w_o, ffn2_norm_g, ffn2_w_gate, ffn2_w_up, ffn2_w_down, final_norm_g):
    depth = ffn1_norm_g.shape[0]
    assert depth == 1
    bp, tp, d = x_prompt.shape
    bs, ts, _ = x_sample.shape
    mp = bp * tp
    ms = bs * ts
    assert tp % ROWS == 0 and ROWS % ts == 0 and ts == SUBLANES and mp % ROWS == 0
    l = 0

    w_in_bf = w_in[l].astype(BF16)
    w_in_b = jnp.pad(w_in_bf[:, D_A:], ((0, 0), (0, D_B_PAD - D_B)))
    mixer_weights = (
        lru_conv_w[l], _row(lru_conv_b[l]), lru_w_a[l].astype(BF16), lru_w_x[l].astype(BF16),
        _row(lru_b_a[l]), _row(lru_b_x[l]), _row(lru_lambda[l]), _row(lru_out_norm_g[l]),
        ssd_conv_w[l], _row(ssd_conv_b[l]), _pad_lanes(ssd_dt_bias[l]), _pad_lanes(ssd_a_log[l]),
        _row(jnp.repeat(ssd_d[l], SSD_HEADDIM)), _row(ssd_out_norm_g[l]),
    )

    (x1,) = _ffn((x_prompt.reshape(mp, d), x_sample.reshape(ms, d)), _row(ffn1_norm_g[l]),
                 ffn1_w_gate[l].astype(BF16), ffn1_w_up[l].astype(BF16), ffn1_w_down[l].astype(BF16),
                 _row(final_norm_g), out_rows=(mp + ms,), final_norm=False)
    mix_g = _row(mix_norm_g[l])
    proj_b = _norm_matmul(x1, mix_g, w_in_b, out_dtype=F32, name="in_proj_b")
    proj_a_s = _norm_matmul(x1, mix_g, w_in_bf, out_dtype=F32, name="in_proj_a_sample",
                            row_offset=mp, n_cols=D_A)

    mixed_p, p_lc, p_lh, p_sc, p_sh = _mixer_prompt(x1, mix_g, w_in_bf, proj_b, mixer_weights,
                                                    batch=bp, seq_len=tp)

    pad_rows = ((0, 0), (SUBLANES - (CONV_WIDTH - 1), 0), (0, 0))
    lprev = jnp.pad(state_lru_conv[l], pad_rows).reshape(ms, LRU_WIDTH)
    sprev = jnp.pad(state_ssd_conv[l], pad_rows).reshape(ms, SSD_CONV_DIM)
    mixed_s, s_lc, s_lh, s_sc, s_sh = _mixer_sample(
        proj_a_s, proj_b, lprev, sprev, state_lru_h[l], state_ssd_h[l].reshape(bs, SSD_WIDTH, SSD_STATE),
        mixer_weights, batch=bs, seq_len=ts, row_block_offset=mp // ROWS)

    x2 = _matmul_residual(mixed_p, mixed_s, w_out[l].astype(BF16), x1, tn=OUT_PROJ_TN, name="out_proj")

    q = _norm_matmul(x2, _row(xattn_norm_g[l]), xattn_w_q[l].astype(BF16), out_dtype=BF16, name="q_proj")
    mem = mem_prompt.reshape(bp * N_MEM, d)
    mk = _norm_matmul(mem, _row(mem_norm_g[l]), xattn_w_k[l].astype(BF16), out_dtype=F32, name="mem_k")
    mv = _norm_matmul(mem, _row(mem_norm_g[l]), xattn_w_v[l].astype(BF16), out_dtype=F32, name="mem_v")
    o_p = _xattn_prompt(q, mk.reshape(bp, N_MEM, d), mv.reshape(bp, N_MEM, d), seq_len=tp, tq=XATTN_TQ)
    o_s = _xattn_sample(q, cache_mem_k[l], cache_mem_v[l], seq_len=ts, n_seq=XATTN_CACHE_SEQS,
                        row_block_offset=mp // (XATTN_CACHE_SEQS * ts))
    x3 = _matmul_residual(o_p, o_s, xattn_w_o[l].astype(BF16), x2, tn=PROJ_TN, name="o_proj")

    y_p, y_s = _ffn((x3,), _row(ffn2_norm_g[l]), ffn2_w_gate[l].astype(BF16), ffn2_w_up[l].astype(BF16),
                    ffn2_w_down[l].astype(BF16), _row(final_norm_g), out_rows=(mp, ms), final_norm=True)

    y_prompt = y_p.reshape(bp, tp, d)
    y_sample = y_s.reshape(bs, ts, d)
    hshape = (SSD_HEADS, SSD_HEADDIM, SSD_STATE)
    return (y_prompt, y_sample,
            p_lc[None], p_lh.reshape(1, bp, LRU_WIDTH), p_sc[None], p_sh.reshape((1, bp) + hshape),
            mk.reshape(1, bp, N_MEM, XATTN_HEADS, XATTN_HEAD_DIM),
            mv.reshape(1, bp, N_MEM, XATTN_HEADS, XATTN_HEAD_DIM),
            s_lc[None], s_lh[None], s_sc[None], s_sh.reshape((1, bs) + hshape))
```

```python
import functools

import jax
import jax.numpy as jnp
from jax import lax
from jax.experimental import pallas as pl
from jax.experimental.pallas import tpu as pltpu

F32 = jnp.float32
BF16 = jnp.bfloat16

D_MODEL = 2048
LRU_WIDTH = 2048
LRU_HEADS = 16
LRU_C = 8.0
CONV_WIDTH = 4
SSD_WIDTH = 2048
SSD_HEADDIM = 64
SSD_HEADS = 32
SSD_GROUPS = 2
SSD_STATE = 128
SSD_CONV_DIM = SSD_WIDTH + 2 * SSD_GROUPS * SSD_STATE
D_IN = 3 * 2048 + SSD_CONV_DIM + SSD_HEADS
N_MEM = 256
XATTN_HEADS = 4
XATTN_HEAD_DIM = 512
EPS = 1e-6

LANES = 128
SUBLANES = 8
VMEM_LIMIT_BYTES = 56 * 1024 * 1024

D_A = 3 * 2048
W_A_TILE = 512
COL_XLRU = 0
COL_GLRU = 2048
COL_Z = 4096
D_B = SSD_CONV_DIM + SSD_HEADS
D_B_PAD = 3072
COL_XBC = 0
COL_DT = SSD_CONV_DIM
ROWS = 128
GROUP_COLS = SSD_WIDTH // SSD_GROUPS

FFN_TM, FFN_TF = 512, 512
PROJ_TM, PROJ_TN = 1024, 1024
RESIDENT_TM = 512
XATTN_TQ = 2048
XATTN_CACHE_SEQS = 4
SAMPLE_STATE_STEPS = 4


def _cparams(n_axes):
    return pltpu.CompilerParams(
        dimension_semantics=("arbitrary",) * n_axes,
        vmem_limit_bytes=VMEM_LIMIT_BYTES)


def _rmsnorm_rows(x, g):
    ms = jnp.mean(x * x, axis=-1, keepdims=True)
    return (x * lax.rsqrt(ms + EPS)) * g


def _softplus(x):
    return jnp.maximum(x, 0.0) + jnp.log1p(jnp.exp(-jnp.abs(x)))


def _silu(x):
    return x * jax.nn.sigmoid(x)


def _when_rows(i, n_a, fn_a, fn_b, extra=None):
    in_a = i < n_a
    in_b = i >= n_a
    if extra is not None:
        in_a = jnp.logical_and(in_a, extra)
        in_b = jnp.logical_and(in_b, extra)
    pl.when(in_a)(fn_a)
    pl.when(in_b)(fn_b)


def _ffn_kernel(n_a, split_in, split_out, final_norm, *refs):
    refs = list(refs)
    xa_ref = refs.pop(0)
    xb_ref = refs.pop(0) if split_in else xa_ref
    g_ref, wg_ref, wu_ref, wd_ref, gf_ref = refs[:5]
    oa_ref = refs[5]
    ob_ref = refs[6] if split_out else oa_ref
    if split_out:
        h_scr, acc_scr = refs[-2:]
    else:
        h_scr, acc_scr = refs[-1], oa_ref
    i = pl.program_id(0)
    j = pl.program_id(1)

    def prologue(x_ref):
        def run():
            x = x_ref[...]
            h_scr[...] = _rmsnorm_rows(x, g_ref[...]).astype(BF16)
            acc_scr[...] = x
        return run

    if split_in:
        _when_rows(i, n_a, prologue(xa_ref), prologue(xb_ref), extra=(j == 0))
    else:
        pl.when(j == 0)(prologue(xa_ref))

    h = h_scr[...]
    gate = jnp.dot(h, wg_ref[...], preferred_element_type=F32)
    up = jnp.dot(h, wu_ref[...], preferred_element_type=F32)
    a = (0.5 * _silu(gate) * up).astype(BF16)
    acc_scr[...] += jnp.dot(a, wd_ref[...], preferred_element_type=F32)

    def epilogue(o_ref):
        def run():
            res = acc_scr[...]
            if final_norm:
                res = _rmsnorm_rows(res, gf_ref[...])
            o_ref[...] = res
        return run

    last = j == pl.num_programs(1) - 1
    if split_out:
        _when_rows(i, n_a, epilogue(oa_ref), epilogue(ob_ref), extra=last)
    elif final_norm:
        pl.when(last)(epilogue(oa_ref))


def _ffn(xs, g, wg, wu, wd, gf, *, out_rows, final_norm):
    tm, tf = FFN_TM, FFN_TF
    split_in = len(xs) == 2
    split_out = len(out_rows) == 2
    d = xs[0].shape[1]
    f = wg.shape[1]
    m = sum(x.shape[0] for x in xs)
    assert m == sum(out_rows) and all(x.shape[0] % tm == 0 for x in xs) and all(r % tm == 0 for r in out_rows)
    n_a = (xs[0].shape[0] if split_in else out_rows[0]) // tm
    if split_in and split_out:
        assert xs[0].shape[0] == out_rows[0]

    def first(i, j):
        return (jnp.minimum(i, n_a - 1), 0)

    def second(i, j):
        return (jnp.maximum(i - n_a, 0), 0)

    def whole(i, j):
        return (i, 0)

    x_specs = ([pl.BlockSpec((tm, d), first), pl.BlockSpec((tm, d), second)] if split_in
               else [pl.BlockSpec((tm, d), whole)])
    o_specs = ([pl.BlockSpec((tm, d), first), pl.BlockSpec((tm, d), second)] if split_out
               else [pl.BlockSpec((tm, d), whole)])
    return pl.pallas_call(
        functools.partial(_ffn_kernel, n_a, split_in, split_out, final_norm),
        grid=(m // tm, f // tf),
        in_specs=x_specs + [
            pl.BlockSpec((1, d), lambda i, j: (0, 0)),
            pl.BlockSpec((d, tf), lambda i, j: (0, j)),
            pl.BlockSpec((d, tf), lambda i, j: (0, j)),
            pl.BlockSpec((tf, d), lambda i, j: (j, 0)),
            pl.BlockSpec((1, d), lambda i, j: (0, 0)),
        ],
        out_specs=o_specs,
        out_shape=[jax.ShapeDtypeStruct((r, d), F32) for r in out_rows],
        scratch_shapes=[pltpu.VMEM((tm, d), BF16)] + ([pltpu.VMEM((tm, d), F32)] if split_out else []),
        compiler_params=_cparams(2),
        name="ffn_final" if final_norm else "ffn",
    )(*xs, g, wg, wu, wd, gf)


def _weight_spec(k, n, tn):
    if tn == n:
        return pl.BlockSpec((k, n), lambda i, j: (0, 0), pipeline_mode=pl.Buffered(1))
    return pl.BlockSpec((k, tn), lambda i, j: (0, j))


def _norm_matmul_kernel(x_ref, g_ref, w_ref, o_ref, h_scr):
    @pl.when(pl.program_id(1) == 0)
    def _():
        h_scr[...] = _rmsnorm_rows(x_ref[...], g_ref[...]).astype(BF16)

    o_ref[...] = jnp.dot(h_scr[...], w_ref[...], preferred_element_type=F32).astype(o_ref.dtype)


def _norm_matmul(x, g, w, *, out_dtype, name, row_offset=0, n_cols=None, tn=PROJ_TN, tm=PROJ_TM):
    k = x.shape[1]
    m = x.shape[0] - row_offset
    n = w.shape[1] if n_cols is None else n_cols
    assert m % tm == 0 and row_offset % tm == 0 and n % tn == 0
    first = row_offset // tm
    return pl.pallas_call(
        _norm_matmul_kernel,
        grid=(m // tm, n // tn),
        in_specs=[
            pl.BlockSpec((tm, k), lambda i, j: (first + i, 0)),
            pl.BlockSpec((1, k), lambda i, j: (0, 0)),
            _weight_spec(k, n, tn),
        ],
        out_specs=pl.BlockSpec((tm, tn), lambda i, j: (i, j)),
        out_shape=jax.ShapeDtypeStruct((m, n), out_dtype),
        scratch_shapes=[pltpu.VMEM((tm, k), BF16)],
        compiler_params=_cparams(2),
        name=name,
    )(x, g, w)


def _matmul_residual_kernel(n_a, aa_ref, ab_ref, w_ref, r_ref, o_ref):
    def run(a_ref):
        def body():
            o_ref[...] = r_ref[...] + jnp.dot(a_ref[...], w_ref[...], preferred_element_type=F32)
        return body

    _when_rows(pl.program_id(0), n_a, run(aa_ref), run(ab_ref))


def _matmul_residual(a_first, a_second, w, res, *, tn, name, tm=PROJ_TM):
    k = a_first.shape[1]
    m = a_first.shape[0] + a_second.shape[0]
    n = w.shape[1]
    assert a_first.shape[0] % tm == 0 and a_second.shape[0] % tm == 0 and res.shape == (m, n)
    n_a = a_first.shape[0] // tm
    return pl.pallas_call(
        functools.partial(_matmul_residual_kernel, n_a),
        grid=(m // tm, n // tn),
        in_specs=[
            pl.BlockSpec((tm, k), lambda i, j: (jnp.minimum(i, n_a - 1), 0)),
            pl.BlockSpec((tm, k), lambda i, j: (jnp.maximum(i - n_a, 0), 0)),
            _weight_spec(k, n, tn),
            pl.BlockSpec((tm, tn), lambda i, j: (i, j)),
        ],
        out_specs=pl.BlockSpec((tm, tn), lambda i, j: (i, j)),
        out_shape=jax.ShapeDtypeStruct((m, n), F32),
        compiler_params=_cparams(2),
        name=name,
    )(a_first, a_second, w, res)


def _xattn_kernel(n_seq, tq, q_ref, k_ref, v_ref, o_ref):
    scale = XATTN_HEAD_DIM ** -0.5
    for s in range(n_seq):
        rows = slice(s * tq, (s + 1) * tq)
        for h in range(XATTN_HEADS):
            cols = slice(h * XATTN_HEAD_DIM, (h + 1) * XATTN_HEAD_DIM)
            q = q_ref[rows, cols]
            k = k_ref[s, :, cols].astype(BF16)
            v = v_ref[s, :, cols].astype(BF16)
            sc = lax.dot_general(q, k, (((1,), (1,)), ((), ())),
                                 preferred_element_type=F32) * scale
            mx = jnp.max(sc, axis=-1, keepdims=True)
            e = jnp.exp(sc - mx)
            p = e / jnp.sum(e, axis=-1, keepdims=True)
            o = jnp.dot(p.astype(BF16), v, preferred_element_type=F32)
            o_ref[rows, cols] = o.astype(BF16)


def _xattn_prompt(q, k, v, *, seq_len, tq):
    b = k.shape[0]
    nt = seq_len // tq
    d = q.shape[1]
    return pl.pallas_call(
        functools.partial(_xattn_kernel, 1, tq),
        grid=(b, nt),
        in_specs=[
            pl.BlockSpec((tq, d), lambda i, j: (i * nt + j, 0)),
            pl.BlockSpec((1, N_MEM, d), lambda i, j: (i, 0, 0)),
            pl.BlockSpec((1, N_MEM, d), lambda i, j: (i, 0, 0)),
        ],
        out_specs=pl.BlockSpec((tq, d), lambda i, j: (i * nt + j, 0)),
        out_shape=jax.ShapeDtypeStruct((b * seq_len, d), BF16),
        compiler_params=_cparams(2),
        name="xattn_prompt",
    )(q, k, v)


def _xattn_cache_kernel(n_seq, tq, q_ref, k_ref, v_ref, o_ref):
    scale = XATTN_HEAD_DIM ** -0.5
    n_rows = XATTN_HEADS * tq
    n_cols = N_MEM * XATTN_HEADS
    own = (_lane_iota((n_rows, n_cols)) % XATTN_HEADS) == (_row_iota((n_rows, n_cols)) // tq)
    qf = q_ref[...].astype(F32)
    outs = []
    for s in range(n_seq):
        qs = qf[s * tq:(s + 1) * tq]
        q4 = jnp.concatenate(
            [qs[:, h * XATTN_HEAD_DIM:(h + 1) * XATTN_HEAD_DIM] for h in range(XATTN_HEADS)],
            axis=0).astype(BF16)
        kf = k_ref[s].reshape(n_cols, XATTN_HEAD_DIM).astype(BF16)
        vf = v_ref[s].reshape(n_cols, XATTN_HEAD_DIM).astype(BF16)
        sc = lax.dot_general(q4, kf, (((1,), (1,)), ((), ())), preferred_element_type=F32) * scale
        sc = jnp.where(own, sc, -1e30)
        mx = jnp.max(sc, axis=-1, keepdims=True)
        e = jnp.exp(sc - mx)
        p = e / jnp.sum(e, axis=-1, keepdims=True)
        o4 = jnp.dot(p.astype(BF16), vf, preferred_element_type=F32)
        outs.append(jnp.concatenate([o4[h * tq:(h + 1) * tq] for h in range(XATTN_HEADS)], axis=1))
    o_ref[...] = jnp.concatenate(outs, axis=0).astype(BF16)


def _xattn_sample(q, k, v, *, seq_len, n_seq, row_block_offset):
    b = k.shape[0]
    d = q.shape[1]
    rows = n_seq * seq_len
    kv_block = (n_seq, N_MEM, XATTN_HEADS, XATTN_HEAD_DIM)
    return pl.pallas_call(
        functools.partial(_xattn_cache_kernel, n_seq, seq_len),
        grid=(b // n_seq,),
        in_specs=[
            pl.BlockSpec((rows, d), lambda i: (row_block_offset + i, 0)),
            pl.BlockSpec(kv_block, lambda i: (i, 0, 0, 0)),
            pl.BlockSpec(kv_block, lambda i: (i, 0, 0, 0)),
        ],
        out_specs=pl.BlockSpec((rows, d), lambda i: (i, 0)),
        out_shape=jax.ShapeDtypeStruct((b * seq_len, d), BF16),
        compiler_params=_cparams(1),
        name="xattn_sample",
    )(q, k, v)


def _row_iota(shape):
    return lax.broadcasted_iota(jnp.int32, shape, 0)


def _lane_iota(shape):
    return lax.broadcasted_iota(jnp.int32, shape, 1)


def _seg_cumsum(x, seg_len):
    pos = _row_iota(x.shape) % seg_len
    s = 1
    while s < seg_len:
        x = x + jnp.where(pos >= s, pltpu.roll(x, s, axis=0), 0.0)
        s *= 2
    return x


def _seg_rev_excl_cumsum(x, seg_len):
    n = x.shape[0]
    pos = _row_iota(x.shape) % seg_len
    y = jnp.where(pos < seg_len - 1, pltpu.roll(x, n - 1, axis=0), 0.0)
    s = 1
    while s < seg_len:
        y = y + jnp.where(pos < seg_len - s, pltpu.roll(y, n - s, axis=0), 0.0)
        s *= 2
    return y


def _scan8(a, b):
    shape = a.shape
    tiled = (shape[0] // SUBLANES, SUBLANES, shape[1])
    a = a.reshape(tiled)
    b = b.reshape(tiled)
    pos = lax.broadcasted_iota(jnp.int32, tiled, 1)
    for s in (1, 2, 4):
        m = pos >= s
        a_sh = pltpu.roll(a, s, axis=1)
        b_sh = pltpu.roll(b, s, axis=1)
        b = jnp.where(m, a * b_sh + b, b)
        a = jnp.where(m, a * a_sh, a)
    return a.reshape(shape), b.reshape(shape)


def _conv_taps(x, w_ref, b_ref, cols, fix):
    acc = b_ref[:, cols] + x * w_ref[CONV_WIDTH - 1:CONV_WIDTH, cols]
    for s in range(1, CONV_WIDTH):
        xs = fix(s, pltpu.roll(x, s, axis=0))
        k = CONV_WIDTH - 1 - s
        acc = acc + xs * w_ref[k:k + 1, cols]
    return acc


def _bcast_col(v, c):
    return jnp.broadcast_to(v[:, c:c + 1], (v.shape[0], LANES))


def _lru_part(sample, pa_ref, prev_fix_lru, h0_rows, hcar_ref,
              lcw_ref, lcb_ref, wa_ref, wx_ref, ba_ref, bx_ref, lam_ref, lg_ref,
              mixed_ref, hs_scr, yl_scr):
    ssq = jnp.zeros((ROWS, 1), F32)
    for hd in range(LRU_HEADS):
        cols = slice(hd * LANES, (hd + 1) * LANES)
        xl = pa_ref[:, COL_XLRU + hd * LANES:COL_XLRU + (hd + 1) * LANES]
        u = _conv_taps(xl, lcw_ref, lcb_ref, cols, functools.partial(prev_fix_lru, cols))
        ub = u.astype(BF16)
        r = jax.nn.sigmoid(jnp.dot(ub, wa_ref[hd], preferred_element_type=F32) + ba_ref[:, cols])
        i = jax.nn.sigmoid(jnp.dot(ub, wx_ref[hd], preferred_element_type=F32) + bx_ref[:, cols])
        log_a = (-LRU_C * r) * _softplus(-lam_ref[:, cols])
        a = jnp.exp(log_a)
        th = jnp.tanh(log_a)
        beta = jnp.sqrt((-2.0 * th) / (1.0 - th))
        bb = (beta * i) * u
        a_cum, b_cum = _scan8(a, bb)
        if sample:
            h = a_cum * h0_rows(cols) + b_cum
        else:
            carry = hcar_ref[:, cols]
            parts = []
            for t in range(ROWS // SUBLANES):
                rs = slice(t * SUBLANES, (t + 1) * SUBLANES)
                ht = a_cum[rs] * carry + b_cum[rs]
                parts.append(ht)
                carry = ht[SUBLANES - 1:SUBLANES]
            h = jnp.concatenate(parts, axis=0)
            hcar_ref[:, cols] = carry
        hs_scr[:, cols] = h
        g = jax.nn.gelu(pa_ref[:, COL_GLRU + hd * LANES:COL_GLRU + (hd + 1) * LANES])
        yl = h * g
        ssq = ssq + jnp.sum(yl * yl, axis=-1, keepdims=True)
        yl_scr[:, cols] = yl
    inv = lax.rsqrt(ssq * (1.0 / LRU_WIDTH) + EPS)
    for hd in range(LRU_HEADS):
        cols = slice(hd * LANES, (hd + 1) * LANES)
        mixed_ref[:, cols] = ((yl_scr[:, cols] * inv) * lg_ref[:, cols]).astype(BF16)


def _ssd_front(sample, pb_ref, prev_fix_ssd, scw_ref, scb_ref, dtb_ref, alog_ref, dexp_ref,
               xc_scr, xw_scr, ecs_scr, y_scr, xt_scr, dec_scr, c_scr, b_scr):
    seg = SUBLANES if sample else ROWS
    n_seq = ROWS // seg

    for t in range(SSD_CONV_DIM // LANES):
        cols = slice(t * LANES, (t + 1) * LANES)
        xb = pb_ref[:, COL_XBC + t * LANES:COL_XBC + (t + 1) * LANES]
        v = _conv_taps(xb, scw_ref, scb_ref, cols, functools.partial(prev_fix_ssd, cols))
        xc_scr[:, cols] = _silu(v)

    dt = _softplus(pb_ref[:, COL_DT:COL_DT + LANES] + dtb_ref[...])
    a_neg = -jnp.exp(alog_ref[...])
    d_a = dt * a_neg
    cs = _seg_cumsum(d_a, seg)
    rcs = _seg_rev_excl_cumsum(d_a, seg)
    todt = jnp.exp(rcs) * dt
    dec_tot = jnp.exp(cs + rcs)
    cs_t = cs.T
    dt_t = dt.T
    dec_t = dec_tot.T
    for s in range(n_seq):
        dec_scr[s] = jnp.broadcast_to(dec_t[0:SSD_HEADS, s * seg:s * seg + 1], (SSD_HEADS, LANES))

    ri = _row_iota((ROWS, ROWS))
    ci = _lane_iota((ROWS, ROWS))
    causal = ri >= ci
    if sample:
        causal = jnp.logical_and(causal, (ri // seg) == (ci // seg))
    lo = _lane_iota((ROWS, LANES)) < SSD_HEADDIM

    b_off = SSD_WIDTH
    c_off = SSD_WIDTH + SSD_GROUPS * SSD_STATE
    for g in range(SSD_GROUPS):
        gcols = slice(g * SSD_STATE, (g + 1) * SSD_STATE)
        bg = xc_scr[:, b_off + g * SSD_STATE:b_off + (g + 1) * SSD_STATE].astype(BF16)
        cg = xc_scr[:, c_off + g * SSD_STATE:c_off + (g + 1) * SSD_STATE].astype(BF16)
        c_scr[:, gcols] = xc_scr[:, c_off + g * SSD_STATE:c_off + (g + 1) * SSD_STATE]
        b_scr[:, gcols] = bg
        cb = lax.dot_general(cg, bg, (((1,), (1,)), ((), ())), preferred_element_type=F32)
        heads_per_group = SSD_HEADS // SSD_GROUPS
        for jp in range(heads_per_group // 2):
            h0 = g * heads_per_group + 2 * jp
            h1 = h0 + 1
            cols = slice(h0 * SSD_HEADDIM, h0 * SSD_HEADDIM + LANES)
            x_pair = xc_scr[:, cols]
            ws = []
            cs_cols = []
            for hh in (h0, h1):
                cs_col = _bcast_col(cs, hh)
                cs_cols.append(cs_col)
                dec = jnp.where(causal, jnp.exp(cs_col - cs_t[hh:hh + 1, :]), 0.0)
                ws.append(((cb * dec) * dt_t[hh:hh + 1, :]).astype(BF16))
            lhs = jnp.concatenate(ws, axis=1)
            rhs = jnp.concatenate([jnp.where(lo, x_pair, 0.0), jnp.where(lo, 0.0, x_pair)],
                                  axis=0).astype(BF16)
            y_diag = jnp.dot(lhs, rhs, preferred_element_type=F32)
            y_scr[:, cols] = y_diag + x_pair * dexp_ref[:, cols]
            xw_scr[:, cols] = x_pair * jnp.where(lo, _bcast_col(todt, h0), _bcast_col(todt, h1))
            ecs_scr[:, cols] = jnp.exp(jnp.where(lo, cs_cols[0], cs_cols[1]))
    for t in range(SSD_WIDTH // LANES):
        xt_scr[t * LANES:(t + 1) * LANES, :] = xw_scr[:, t * LANES:(t + 1) * LANES].T.astype(BF16)


def _ssd_state_step(c_rows, b_mask, s_old, dec, xt_scr):
    y_parts = []
    s_parts = []
    heads_per_group = SSD_HEADS // SSD_GROUPS
    for g in range(SSD_GROUPS):
        gcols = slice(g * SSD_STATE, (g + 1) * SSD_STATE)
        grows = slice(g * GROUP_COLS, (g + 1) * GROUP_COLS)
        sg = s_old[grows]
        y_parts.append(lax.dot_general(c_rows[:, gcols], sg.astype(BF16), (((1,), (1,)), ((), ())),
                                       preferred_element_type=F32))
        upd = jnp.dot(xt_scr[grows, :], b_mask[:, gcols], preferred_element_type=F32)
        for hh in range(heads_per_group):
            hr = slice(hh * SSD_HEADDIM, (hh + 1) * SSD_HEADDIM)
            habs = g * heads_per_group + hh
            s_parts.append(sg[hr] * dec[habs:habs + 1, :] + upd[hr])
    return jnp.concatenate(y_parts, axis=1), jnp.concatenate(s_parts, axis=0)


def _ssd_finish(pa_ref, y_scr, sg_ref, mixed_ref):
    for g in range(SSD_GROUPS):
        ssq = jnp.zeros((ROWS, 1), F32)
        n_t = GROUP_COLS // LANES
        for t in range(n_t):
            c0 = g * GROUP_COLS + t * LANES
            yz = y_scr[:, c0:c0 + LANES] * _silu(pa_ref[:, COL_Z + c0:COL_Z + c0 + LANES])
            y_scr[:, c0:c0 + LANES] = yz
            ssq = ssq + jnp.sum(yz * yz, axis=-1, keepdims=True)
        inv = lax.rsqrt(ssq * (1.0 / GROUP_COLS) + EPS)
        for t in range(n_t):
            c0 = g * GROUP_COLS + t * LANES
            mixed_ref[:, LRU_WIDTH + c0:LRU_WIDTH + c0 + LANES] = (
                (y_scr[:, c0:c0 + LANES] * inv) * sg_ref[:, c0:c0 + LANES]).astype(BF16)


def _mixer_prompt_kernel(x_ref, xn_ref, mg_ref, *rest):
    n_slabs = D_A // W_A_TILE
    wina_refs = rest[:n_slabs]
    (pb_ref,
     lcw_ref, lcb_ref, wa_ref, wx_ref, ba_ref, bx_ref, lam_ref, lg_ref,
     scw_ref, scb_ref, dtb_ref, alog_ref, dexp_ref, sg_ref,
     mixed_ref, lconv_ref, lh_ref, sconv_ref, sh_ref,
     pa_scr, hn_scr, hcar_ref, ltail_scr, stail_scr, s_scr,
     hs_scr, yl_scr, xc_scr, xw_scr, ecs_scr, y_scr, xt_scr, dec_scr, c_scr, b_scr) = rest[n_slabs:]
    c = pl.program_id(1)
    step = pl.program_id(0) * pl.num_programs(1) + c
    slot = step % 2

    @pl.when(c == 0)
    def _():
        hcar_ref[...] = jnp.zeros_like(hcar_ref)
        ltail_scr[...] = jnp.zeros_like(ltail_scr)
        stail_scr[...] = jnp.zeros_like(stail_scr)
        s_scr[...] = jnp.zeros_like(s_scr)

    @pl.when(step == 0)
    def _():
        hn = _rmsnorm_rows(x_ref[...], mg_ref[...]).astype(BF16)
        for j in range(n_slabs):
            pa_scr[0, :, j * W_A_TILE:(j + 1) * W_A_TILE] = jnp.dot(
                hn, wina_refs[j][...], preferred_element_type=F32)

    hn_scr[...] = _rmsnorm_rows(xn_ref[...], mg_ref[...]).astype(BF16)
    pa_next = pa_scr.at[1 - slot]
    pa_cur = pa_scr.at[slot]

    for j in range(n_slabs):
        pa_next[:, j * W_A_TILE:(j + 1) * W_A_TILE] = jnp.dot(
            hn_scr[...], wina_refs[j][...], preferred_element_type=F32)

    pos8 = _row_iota((SUBLANES, LANES))

    def make_fix(tail_scr):
        def fix(cols, s, rolled):
            first = jnp.where(pos8 < s, pltpu.roll(tail_scr[:, cols], s, axis=0), rolled[0:SUBLANES])
            return jnp.concatenate([first, rolled[SUBLANES:]], axis=0)
        return fix

    _lru_part(False, pa_cur, make_fix(ltail_scr), None, hcar_ref,
              lcw_ref, lcb_ref, wa_ref, wx_ref, ba_ref, bx_ref, lam_ref, lg_ref,
              mixed_ref, hs_scr, yl_scr)
    _ssd_front(False, pb_ref, make_fix(stail_scr), scw_ref, scb_ref, dtb_ref, alog_ref, dexp_ref,
               xc_scr, xw_scr, ecs_scr, y_scr, xt_scr, dec_scr, c_scr, b_scr)

    ltail_scr[...] = pa_cur[ROWS - SUBLANES:ROWS, COL_XLRU:COL_XLRU + LRU_WIDTH]
    stail_scr[...] = pb_ref[ROWS - SUBLANES:ROWS, COL_XBC:COL_XBC + SSD_CONV_DIM]

    y_off, s_new = _ssd_state_step(c_scr[...].astype(BF16), b_scr[...], s_scr[...], dec_scr[0], xt_scr)
    s_scr[...] = s_new
    for t in range(SSD_WIDTH // LANES):
        cols = slice(t * LANES, (t + 1) * LANES)
        y_scr[:, cols] = y_scr[:, cols] + ecs_scr[:, cols] * y_off[:, cols]
    _ssd_finish(pa_cur, y_scr, sg_ref, mixed_ref)

    @pl.when(c == pl.num_programs(1) - 1)
    def _():
        lconv_ref[0] = pa_cur[ROWS - (CONV_WIDTH - 1):ROWS, COL_XLRU:COL_XLRU + LRU_WIDTH]
        sconv_ref[0] = pb_ref[ROWS - (CONV_WIDTH - 1):ROWS, COL_XBC:COL_XBC + SSD_CONV_DIM]
        lh_ref[0] = hcar_ref[...]
        sh_ref[0] = s_scr[...]


def _mixer_sample_kernel(n_inner, pa_ref, pb_ref, lprev_ref, sprev_ref, h0_ref, sin_ref,
                         lcw_ref, lcb_ref, wa_ref, wx_ref, ba_ref, bx_ref, lam_ref, lg_ref,
                         scw_ref, scb_ref, dtb_ref, alog_ref, dexp_ref, sg_ref,
                         mixed_ref, lconv_ref, lh_ref, sconv_ref, sh_ref,
                         hs_scr, yl_scr, xc_scr, xw_scr, ecs_scr, y_scr, xt_scr, dec_scr, c_scr, b_scr):
    i = pl.program_id(1)
    seg = SUBLANES
    n_seq = ROWS // seg
    seq_per_step = n_seq // n_inner

    @pl.when(i == 0)
    def _():
        pos = _row_iota((ROWS, LANES)) % seg

        def make_fix(prev_ref):
            def fix(cols, s, rolled):
                return jnp.where(pos < s, pltpu.roll(prev_ref[:, cols], ROWS - seg + s, axis=0), rolled)
            return fix

        def h0_rows(cols):
            return jnp.concatenate(
                [jnp.broadcast_to(h0_ref[q:q + 1, cols], (seg, LANES)) for q in range(n_seq)], axis=0)

        _ssd_front(True, pb_ref, make_fix(sprev_ref), scw_ref, scb_ref, dtb_ref, alog_ref, dexp_ref,
                   xc_scr, xw_scr, ecs_scr, y_scr, xt_scr, dec_scr, c_scr, b_scr)
        _lru_part(True, pa_ref, make_fix(lprev_ref), h0_rows, None,
                  lcw_ref, lcb_ref, wa_ref, wx_ref, ba_ref, bx_ref, lam_ref, lg_ref,
                  mixed_ref, hs_scr, yl_scr)
        for sq in range(n_seq):
            tail = slice((sq + 1) * seg - (CONV_WIDTH - 1), (sq + 1) * seg)
            lconv_ref[sq] = pa_ref[tail, COL_XLRU:COL_XLRU + LRU_WIDTH]
            sconv_ref[sq] = pb_ref[tail, COL_XBC:COL_XBC + SSD_CONV_DIM]
            lh_ref[sq:sq + 1, :] = hs_scr[(sq + 1) * seg - 1:(sq + 1) * seg, :]

    rgroup = _row_iota((ROWS, 2 * SSD_STATE)) // seg
    for jj in range(seq_per_step):
        q = i * seq_per_step + jj
        r0 = pl.multiple_of(q * seg, seg)
        c_rows = c_scr[pl.ds(r0, seg), :].astype(BF16)
        b_all = b_scr[...]
        b_mask = jnp.where(rgroup == q, b_all, jnp.zeros_like(b_all))
        y_off, s_new = _ssd_state_step(c_rows, b_mask, sin_ref[jj], dec_scr[q], xt_scr)
        sh_ref[jj] = s_new
        y_scr[pl.ds(r0, seg), :] = y_scr[pl.ds(r0, seg), :] + ecs_scr[pl.ds(r0, seg), :] * y_off

    @pl.when(i == n_inner - 1)
    def _():
        _ssd_finish(pa_ref, y_scr, sg_ref, mixed_ref)


def _mixer_scratch(n_dec):
    return [
        pltpu.VMEM((ROWS, LRU_WIDTH), F32),
        pltpu.VMEM((ROWS, LRU_WIDTH), F32),
        pltpu.VMEM((ROWS, SSD_CONV_DIM), F32),
        pltpu.VMEM((ROWS, SSD_WIDTH), F32),
        pltpu.VMEM((ROWS, SSD_WIDTH), F32),
        pltpu.VMEM((ROWS, SSD_WIDTH), F32),
        pltpu.VMEM((SSD_WIDTH, ROWS), BF16),
        pltpu.VMEM((n_dec, SSD_HEADS, LANES), F32),
        pltpu.VMEM((ROWS, 2 * SSD_STATE), F32),
        pltpu.VMEM((ROWS, 2 * SSD_STATE), BF16),
    ]


def _full_spec(shape, n_grid):
    zeros = (0,) * len(shape)
    if n_grid == 2:
        return pl.BlockSpec(shape, lambda i, j: zeros)
    return pl.BlockSpec(shape, lambda i: zeros)


def _mixer_weight_specs(n_grid):
    return [
        _full_spec((CONV_WIDTH, LRU_WIDTH), n_grid),
        _full_spec((1, LRU_WIDTH), n_grid),
        _full_spec((LRU_HEADS, LANES, LANES), n_grid),
        _full_spec((LRU_HEADS, LANES, LANES), n_grid),
        _full_spec((1, LRU_WIDTH), n_grid),
        _full_spec((1, LRU_WIDTH), n_grid),
        _full_spec((1, LRU_WIDTH), n_grid),
        _full_spec((1, LRU_WIDTH), n_grid),
        _full_spec((CONV_WIDTH, SSD_CONV_DIM), n_grid),
        _full_spec((1, SSD_CONV_DIM), n_grid),
        _full_spec((1, LANES), n_grid),
        _full_spec((1, LANES), n_grid),
        _full_spec((1, SSD_WIDTH), n_grid),
        _full_spec((1, SSD_WIDTH), n_grid),
    ]


def _mixer_prompt(x1, mix_g, w_in_a, proj_b, weights, *, batch, seq_len):
    nc = seq_len // ROWS
    n_slabs = D_A // W_A_TILE

    def slab_spec(j):
        return pl.BlockSpec((D_MODEL, W_A_TILE), lambda b, c: (0, j), pipeline_mode=pl.Buffered(1))

    return pl.pallas_call(
        _mixer_prompt_kernel,
        grid=(batch, nc),
        in_specs=[
            pl.BlockSpec((ROWS, D_MODEL), lambda b, c: (b * nc + c, 0)),
            pl.BlockSpec((ROWS, D_MODEL), lambda b, c: (jnp.minimum(b * nc + c + 1, batch * nc - 1), 0)),
            pl.BlockSpec((1, D_MODEL), lambda b, c: (0, 0)),
        ] + [slab_spec(j) for j in range(n_slabs)] + [
            pl.BlockSpec((ROWS, D_B_PAD), lambda b, c: (b * nc + c, 0)),
        ] + _mixer_weight_specs(2),
        out_specs=[
            pl.BlockSpec((ROWS, LRU_WIDTH + SSD_WIDTH), lambda b, c: (b * nc + c, 0)),
            pl.BlockSpec((1, CONV_WIDTH - 1, LRU_WIDTH), lambda b, c: (b, 0, 0)),
            pl.BlockSpec((1, 1, LRU_WIDTH), lambda b, c: (b, 0, 0)),
            pl.BlockSpec((1, CONV_WIDTH - 1, SSD_CONV_DIM), lambda b, c: (b, 0, 0)),
            pl.BlockSpec((1, SSD_WIDTH, SSD_STATE), lambda b, c: (b, 0, 0)),
        ],
        out_shape=[
            jax.ShapeDtypeStruct((batch * seq_len, LRU_WIDTH + SSD_WIDTH), BF16),
            jax.ShapeDtypeStruct((batch, CONV_WIDTH - 1, LRU_WIDTH), F32),
            jax.ShapeDtypeStruct((batch, 1, LRU_WIDTH), F32),
            jax.ShapeDtypeStruct((batch, CONV_WIDTH - 1, SSD_CONV_DIM), F32),
            jax.ShapeDtypeStruct((batch, SSD_WIDTH, SSD_STATE), F32),
        ],
        scratch_shapes=[
            pltpu.VMEM((2, ROWS, D_A), F32),
            pltpu.VMEM((ROWS, D_MODEL), BF16),
            pltpu.VMEM((1, LRU_WIDTH), F32),
            pltpu.VMEM((SUBLANES, LRU_WIDTH), F32),
            pltpu.VMEM((SUBLANES, SSD_CONV_DIM), F32),
            pltpu.VMEM((SSD_WIDTH, SSD_STATE), F32),
        ] + _mixer_scratch(1),
        compiler_params=_cparams(2),
        name="mixer_prompt",
    )(x1, x1, mix_g, *([w_in_a] * n_slabs), proj_b, *weights)


def _mixer_sample(proj_a, proj_b, lprev, sprev, h0, s_in, weights, *, batch, seq_len, row_block_offset,
                  n_inner=SAMPLE_STATE_STEPS):
    n_seq = ROWS // seq_len
    n_outer = batch // n_seq
    sps = n_seq // n_inner
    return pl.pallas_call(
        functools.partial(_mixer_sample_kernel, n_inner),
        grid=(n_outer, n_inner),
        in_specs=[
            pl.BlockSpec((ROWS, D_A), lambda o, i: (o, 0)),
            pl.BlockSpec((ROWS, D_B_PAD), lambda o, i: (row_block_offset + o, 0)),
            pl.BlockSpec((ROWS, LRU_WIDTH), lambda o, i: (o, 0)),
            pl.BlockSpec((ROWS, SSD_CONV_DIM), lambda o, i: (o, 0)),
            pl.BlockSpec((n_seq, LRU_WIDTH), lambda o, i: (o, 0)),
            pl.BlockSpec((sps, SSD_WIDTH, SSD_STATE), lambda o, i: (o * n_inner + i, 0, 0)),
        ] + _mixer_weight_specs(2),
        out_specs=[
            pl.BlockSpec((ROWS, LRU_WIDTH + SSD_WIDTH), lambda o, i: (o, 0)),
            pl.BlockSpec((n_seq, CONV_WIDTH - 1, LRU_WIDTH), lambda o, i: (o, 0, 0)),
            pl.BlockSpec((n_seq, LRU_WIDTH), lambda o, i: (o, 0)),
            pl.BlockSpec((n_seq, CONV_WIDTH - 1, SSD_CONV_DIM), lambda o, i: (o, 0, 0)),
            pl.BlockSpec((sps, SSD_WIDTH, SSD_STATE), lambda o, i: (o * n_inner + i, 0, 0)),
        ],
        out_shape=[
            jax.ShapeDtypeStruct((batch * seq_len, LRU_WIDTH + SSD_WIDTH), BF16),
            jax.ShapeDtypeStruct((batch, CONV_WIDTH - 1, LRU_WIDTH), F32),
            jax.ShapeDtypeStruct((batch, LRU_WIDTH), F32),
            jax.ShapeDtypeStruct((batch, CONV_WIDTH - 1, SSD_CONV_DIM), F32),
            jax.ShapeDtypeStruct((batch, SSD_WIDTH, SSD_STATE), F32),
        ],
        scratch_shapes=_mixer_scratch(n_seq),
        compiler_params=_cparams(2),
        name="mixer_sample",
    )(proj_a, proj_b, lprev, sprev, h0, s_in, *weights)


def _row(v):
    return v.reshape(1, -1).astype(F32)


def _pad_lanes(v):
    v = v.reshape(1, -1).astype(F32)
    return jnp.pad(v, ((0, 0), (0, LANES - v.shape[1])))


def kernel(x_prompt, mem_prompt, x_sample, cache_mem_k, cache_mem_v, state_lru_conv, state_lru_h, state_ssd_conv, state_ssd_h, ffn1_norm_g, ffn1_w_gate, ffn1_w_up, ffn1_w_down, mix_norm_g, w_in, lru_conv_w, lru_conv_b, lru_w_a, lru_b_a, lru_w_x, lru_b_x, lru_lambda, lru_out_norm_g, ssd_conv_w, ssd_conv_b, ssd_dt_bias, ssd_a_log, ssd_d, ssd_out_norm_g, w_out, xattn_norm_g, mem_norm_g, xattn_w_q, xattn_w_k, xattn_w_v, xattn_w_o, ffn2_norm_g, ffn2_w_gate, ffn2_w_up, ffn2_w_down, final_norm_g):
    depth = ffn1_norm_g.shape[0]
    assert depth == 1
    bp, tp, d = x_prompt.shape
    bs, ts, _ = x_sample.shape
    mp = bp * tp
    ms = bs * ts
    assert tp % ROWS == 0 and ROWS % ts == 0 and ts == SUBLANES and mp % ROWS == 0
    l = 0

    w_in_bf = w_in[l].astype(BF16)
    w_in_b = jnp.pad(w_in_bf[:, D_A:], ((0, 0), (0, D_B_PAD - D_B)))
    mixer_weights = (
        lru_conv_w[l], _row(lru_conv_b[l]), lru_w_a[l].astype(BF16), lru_w_x[l].astype(BF16),
        _row(lru_b_a[l]), _row(lru_b_x[l]), _row(lru_lambda[l]), _row(lru_out_norm_g[l]),
        ssd_conv_w[l], _row(ssd_conv_b[l]), _pad_lanes(ssd_dt_bias[l]), _pad_lanes(ssd_a_log[l]),
        _row(jnp.repeat(ssd_d[l], SSD_HEADDIM)), _row(ssd_out_norm_g[l]),
    )

    (x1,) = _ffn((x_prompt.reshape(mp, d), x_sample.reshape(ms, d)), _row(ffn1_norm_g[l]),
                 ffn1_w_gate[l].astype(BF16), ffn1_w_up[l].astype(BF16), ffn1_w_down[l].astype(BF16),
                 _row(final_norm_g), out_rows=(mp + ms,), final_norm=False)
    mix_g = _row(mix_norm_g[l])
    proj_b = _norm_matmul(x1, mix_g, w_in_b, out_dtype=F32, name="in_proj_b", tn=D_B_PAD, tm=RESIDENT_TM)
    proj_a_s = _norm_matmul(x1, mix_g, w_in_bf, out_dtype=F32, name="in_proj_a_sample",
                            row_offset=mp, n_cols=D_A)

    mixed_p, p_lc, p_lh, p_sc, p_sh = _mixer_prompt(x1, mix_g, w_in_bf, proj_b, mixer_weights,
                                                    batch=bp, seq_len=tp)

    pad_rows = ((0, 0), (SUBLANES - (CONV_WIDTH - 1), 0), (0, 0))
    lprev = jnp.pad(state_lru_conv[l], pad_rows).reshape(ms, LRU_WIDTH)
    sprev = jnp.pad(state_ssd_conv[l], pad_rows).reshape(ms, SSD_CONV_DIM)
    mixed_s, s_lc, s_lh, s_sc, s_sh = _mixer_sample(
        proj_a_s, proj_b, lprev, sprev, state_lru_h[l], state_ssd_h[l].reshape(bs, SSD_WIDTH, SSD_STATE),
        mixer_weights, batch=bs, seq_len=ts, row_block_offset=mp // ROWS)

    x2 = _matmul_residual(mixed_p, mixed_s, w_out[l].astype(BF16), x1, tn=d, tm=RESIDENT_TM, name="out_proj")

    q = _norm_matmul(x2, _row(xattn_norm_g[l]), xattn_w_q[l].astype(BF16), out_dtype=BF16, name="q_proj",
                     tn=d)
    mem = mem_prompt.reshape(bp * N_MEM, d)
    mk = _norm_matmul(mem, _row(mem_norm_g[l]), xattn_w_k[l].astype(BF16), out_dtype=F32, name="mem_k",
                      tn=d, tm=RESIDENT_TM)
    mv = _norm_matmul(mem, _row(mem_norm_g[l]), xattn_w_v[l].astype(BF16), out_dtype=F32, name="mem_v",
                      tn=d, tm=RESIDENT_TM)
    o_p = _xattn_prompt(q, mk.reshape(bp, N_MEM, d), mv.reshape(bp, N_MEM, d), seq_len=tp, tq=XATTN_TQ)
    o_s = _xattn_sample(q, cache_mem_k[l], cache_mem_v[l], seq_len=ts, n_seq=XATTN_CACHE_SEQS,
                        row_block_offset=mp // (XATTN_CACHE_SEQS * ts))
    x3 = _matmul_residual(o_p, o_s, xattn_w_o[l].astype(BF16), x2, tn=d, tm=RESIDENT_TM, name="o_proj")

    y_p, y_s = _ffn((x3,), _row(ffn2_norm_g[l]), ffn2_w_gate[l].astype(BF16), ffn2_w_up[l].astype(BF16),
                    ffn2_w_down[l].astype(BF16), _row(final_norm_g), out_rows=(mp, ms), final_norm=True)

    y_prompt = y_p.reshape(bp, tp, d)
    y_sample = y_s.reshape(bs, ts, d)
    hshape = (SSD_HEADS, SSD_HEADDIM, SSD_STATE)
    return (y_prompt, y_sample,
            p_lc[None], p_lh.reshape(1, bp, LRU_WIDTH), p_sc[None], p_sh.reshape((1, bp) + hshape),
            mk.reshape(1, bp, N_MEM, XATTN_HEADS, XATTN_HEAD_DIM),
            mv.reshape(1, bp, N_MEM, XATTN_HEADS, XATTN_HEAD_DIM),
            s_lc[None], s_lh[None], s_sc[None], s_sh.reshape((1, bs) + hshape))
```

```python
import functools

import jax
import jax.numpy as jnp
from jax import lax
from jax.experimental import pallas as pl
from jax.experimental.pallas import tpu as pltpu

F32 = jnp.float32
BF16 = jnp.bfloat16

D_MODEL = 2048
LRU_WIDTH = 2048
LRU_HEADS = 16
LRU_C = 8.0
CONV_WIDTH = 4
SSD_WIDTH = 2048
SSD_HEADDIM = 64
SSD_HEADS = 32
SSD_GROUPS = 2
SSD_STATE = 128
SSD_CONV_DIM = SSD_WIDTH + 2 * SSD_GROUPS * SSD_STATE
D_IN = 3 * 2048 + SSD_CONV_DIM + SSD_HEADS
N_MEM = 256
XATTN_HEADS = 4
XATTN_HEAD_DIM = 512
EPS = 1e-6

LANES = 128
SUBLANES = 8
VMEM_LIMIT_BYTES = 56 * 1024 * 1024

D_A = 3 * 2048
W_A_TILE = 512
COL_XLRU = 0
COL_GLRU = 2048
COL_Z = 4096
D_B = SSD_CONV_DIM + SSD_HEADS
D_B_PAD = 3072
COL_XBC = 0
COL_DT = SSD_CONV_DIM
ROWS = 128
GROUP_COLS = SSD_WIDTH // SSD_GROUPS

FFN_TM, FFN_TF = 512, 512
PROJ_TM, PROJ_TN = 1024, 1024
RESIDENT_TM = 512
XATTN_TQ = 2048
XATTN_CACHE_SEQS = 4
SAMPLE_STATE_STEPS = 4


def _cparams(n_axes):
    return pltpu.CompilerParams(
        dimension_semantics=("arbitrary",) * n_axes,
        vmem_limit_bytes=VMEM_LIMIT_BYTES)


def _rmsnorm_rows(x, g):
    ms = jnp.mean(x * x, axis=-1, keepdims=True)
    return (x * lax.rsqrt(ms + EPS)) * g


def _softplus(x):
    return jnp.maximum(x, 0.0) + jnp.log1p(jnp.exp(-jnp.abs(x)))


def _silu(x):
    return x * jax.nn.sigmoid(x)


def _when_rows(i, n_a, fn_a, fn_b, extra=None):
    in_a = i < n_a
    in_b = i >= n_a
    if extra is not None:
        in_a = jnp.logical_and(in_a, extra)
        in_b = jnp.logical_and(in_b, extra)
    pl.when(in_a)(fn_a)
    pl.when(in_b)(fn_b)


def _ffn_kernel(n_a, split_in, split_out, final_norm, n_riders, *refs):
    refs = list(refs)
    xa_ref = refs.pop(0)
    xb_ref = refs.pop(0) if split_in else xa_ref
    g_ref, wg_ref, wu_ref, wd_ref, gf_ref = refs[:5]
    refs = refs[5:]
    src_refs = [refs.pop(0) for _ in range(n_riders)]
    oa_ref = refs.pop(0)
    ob_ref = refs.pop(0) if split_out else oa_ref
    for src_ref in src_refs:
        refs.pop(0)[...] = src_ref[...].astype(BF16)
    if split_out:
        h_scr, acc_scr = refs[-2:]
    else:
        h_scr, acc_scr = refs[-1], oa_ref
    i = pl.program_id(0)
    j = pl.program_id(1)

    def prologue(x_ref):
        def run():
            x = x_ref[...]
            h_scr[...] = _rmsnorm_rows(x, g_ref[...]).astype(BF16)
            acc_scr[...] = x
        return run

    if split_in:
        _when_rows(i, n_a, prologue(xa_ref), prologue(xb_ref), extra=(j == 0))
    else:
        pl.when(j == 0)(prologue(xa_ref))

    h = h_scr[...]
    gate = jnp.dot(h, wg_ref[...], preferred_element_type=F32)
    up = jnp.dot(h, wu_ref[...], preferred_element_type=F32)
    a = (0.5 * _silu(gate) * up).astype(BF16)
    acc_scr[...] += jnp.dot(a, wd_ref[...], preferred_element_type=F32)

    def epilogue(o_ref):
        def run():
            res = acc_scr[...]
            if final_norm:
                res = _rmsnorm_rows(res, gf_ref[...])
            o_ref[...] = res
        return run

    last = j == pl.num_programs(1) - 1
    if split_out:
        _when_rows(i, n_a, epilogue(oa_ref), epilogue(ob_ref), extra=last)
    elif final_norm:
        pl.when(last)(epilogue(oa_ref))


def _ffn(xs, g, wg, wu, wd, gf, *, out_rows, final_norm, cast_srcs=()):
    tm, tf = FFN_TM, FFN_TF
    split_in = len(xs) == 2
    split_out = len(out_rows) == 2
    d = xs[0].shape[1]
    f = wg.shape[1]
    m = sum(x.shape[0] for x in xs)
    assert m == sum(out_rows) and all(x.shape[0] % tm == 0 for x in xs) and all(r % tm == 0 for r in out_rows)
    n_a = (xs[0].shape[0] if split_in else out_rows[0]) // tm
    if split_in and split_out:
        assert xs[0].shape[0] == out_rows[0]

    def first(i, j):
        return (jnp.minimum(i, n_a - 1), 0)

    def second(i, j):
        return (jnp.maximum(i - n_a, 0), 0)

    def whole(i, j):
        return (i, 0)

    x_specs = ([pl.BlockSpec((tm, d), first), pl.BlockSpec((tm, d), second)] if split_in
               else [pl.BlockSpec((tm, d), whole)])
    o_specs = ([pl.BlockSpec((tm, d), first), pl.BlockSpec((tm, d), second)] if split_out
               else [pl.BlockSpec((tm, d), whole)])
    grid = (m // tm, f // tf)
    riders = [_cast_rider(src, grid) for src in cast_srcs]
    return pl.pallas_call(
        functools.partial(_ffn_kernel, n_a, split_in, split_out, final_norm, len(riders)),
        grid=grid,
        in_specs=x_specs + [
            pl.BlockSpec((1, d), lambda i, j: (0, 0)),
            pl.BlockSpec((d, tf), lambda i, j: (0, j)),
            pl.BlockSpec((d, tf), lambda i, j: (0, j)),
            pl.BlockSpec((tf, d), lambda i, j: (j, 0)),
            pl.BlockSpec((1, d), lambda i, j: (0, 0)),
        ] + [spec for spec, _ in riders],
        out_specs=o_specs + [spec for spec, _ in riders],
        out_shape=[jax.ShapeDtypeStruct((r, d), F32) for r in out_rows] + [shape for _, shape in riders],
        scratch_shapes=[pltpu.VMEM((tm, d), BF16)] + ([pltpu.VMEM((tm, d), F32)] if split_out else []),
        compiler_params=_cparams(2),
        name="ffn_final" if final_norm else "ffn",
    )(*xs, g, wg, wu, wd, gf, *cast_srcs)


def _weight_spec(k, n, tn):
    if tn == n:
        return pl.BlockSpec((k, n), lambda i, j: (0, 0), pipeline_mode=pl.Buffered(1))
    return pl.BlockSpec((k, tn), lambda i, j: (0, j))


def _cast_rider(src, grid):
    rows, cols = src.shape
    n_steps = grid[0] * grid[1]
    n_blocks = max(b for b in range(1, n_steps + 1) if rows % b == 0 and (rows // b) % 16 == 0)
    inner = grid[1]
    spec = pl.BlockSpec((rows // n_blocks, cols),
                        lambda i, j: (jnp.minimum(i * inner + j, n_blocks - 1), 0))
    return spec, jax.ShapeDtypeStruct((rows, cols), BF16)


def _norm_matmul_kernel(has_rider, x_ref, g_ref, w_ref, *rest):
    if has_rider:
        src_ref, o_ref, dst_ref, h_scr = rest
        dst_ref[...] = src_ref[...].astype(BF16)
    else:
        o_ref, h_scr = rest

    @pl.when(pl.program_id(1) == 0)
    def _():
        h_scr[...] = _rmsnorm_rows(x_ref[...], g_ref[...]).astype(BF16)

    o_ref[...] = jnp.dot(h_scr[...], w_ref[...], preferred_element_type=F32).astype(o_ref.dtype)


def _norm_matmul(x, g, w, *, out_dtype, name, row_offset=0, n_cols=None, tn=PROJ_TN, tm=PROJ_TM,
                 cast_src=None):
    k = x.shape[1]
    m = x.shape[0] - row_offset
    n = w.shape[1] if n_cols is None else n_cols
    assert m % tm == 0 and row_offset % tm == 0 and n % tn == 0
    first = row_offset // tm
    in_specs = [
        pl.BlockSpec((tm, k), lambda i, j: (first + i, 0)),
        pl.BlockSpec((1, k), lambda i, j: (0, 0)),
        _weight_spec(k, n, tn),
    ]
    out_specs = [pl.BlockSpec((tm, tn), lambda i, j: (i, j))]
    out_shape = [jax.ShapeDtypeStruct((m, n), out_dtype)]
    operands = [x, g, w]
    if cast_src is not None:
        spec, shape = _cast_rider(cast_src, (m // tm, n // tn))
        in_specs.append(spec)
        out_specs.append(spec)
        out_shape.append(shape)
        operands.append(cast_src)
    outs = pl.pallas_call(
        functools.partial(_norm_matmul_kernel, cast_src is not None),
        grid=(m // tm, n // tn),
        in_specs=in_specs,
        out_specs=out_specs,
        out_shape=out_shape,
        scratch_shapes=[pltpu.VMEM((tm, k), BF16)],
        compiler_params=_cparams(2),
        name=name,
    )(*operands)
    return outs[0] if cast_src is None else tuple(outs)


def _matmul_residual_kernel(n_a, has_rider, aa_ref, ab_ref, w_ref, r_ref, *rest):
    if has_rider:
        src_ref, o_ref, dst_ref = rest
        dst_ref[...] = src_ref[...].astype(BF16)
    else:
        (o_ref,) = rest

    def run(a_ref):
        def body():
            o_ref[...] = r_ref[...] + jnp.dot(a_ref[...], w_ref[...], preferred_element_type=F32)
        return body

    _when_rows(pl.program_id(0), n_a, run(aa_ref), run(ab_ref))


def _matmul_residual(a_first, a_second, w, res, *, tn, name, tm=PROJ_TM, cast_src=None):
    k = a_first.shape[1]
    m = a_first.shape[0] + a_second.shape[0]
    n = w.shape[1]
    assert a_first.shape[0] % tm == 0 and a_second.shape[0] % tm == 0 and res.shape == (m, n)
    n_a = a_first.shape[0] // tm
    in_specs = [
        pl.BlockSpec((tm, k), lambda i, j: (jnp.minimum(i, n_a - 1), 0)),
        pl.BlockSpec((tm, k), lambda i, j: (jnp.maximum(i - n_a, 0), 0)),
        _weight_spec(k, n, tn),
        pl.BlockSpec((tm, tn), lambda i, j: (i, j)),
    ]
    out_specs = [pl.BlockSpec((tm, tn), lambda i, j: (i, j))]
    out_shape = [jax.ShapeDtypeStruct((m, n), F32)]
    operands = [a_first, a_second, w, res]
    if cast_src is not None:
        spec, shape = _cast_rider(cast_src, (m // tm, n // tn))
        in_specs.append(spec)
        out_specs.append(spec)
        out_shape.append(shape)
        operands.append(cast_src)
    outs = pl.pallas_call(
        functools.partial(_matmul_residual_kernel, n_a, cast_src is not None),
        grid=(m // tm, n // tn),
        in_specs=in_specs,
        out_specs=out_specs,
        out_shape=out_shape,
        compiler_params=_cparams(2),
        name=name,
    )(*operands)
    return outs[0] if cast_src is None else tuple(outs)


def _xattn_kernel(n_seq, tq, q_ref, k_ref, v_ref, o_ref):
    scale = XATTN_HEAD_DIM ** -0.5
    for s in range(n_seq):
        rows = slice(s * tq, (s + 1) * tq)
        for h in range(XATTN_HEADS):
            cols = slice(h * XATTN_HEAD_DIM, (h + 1) * XATTN_HEAD_DIM)
            q = q_ref[rows, cols]
            k = k_ref[s, :, cols].astype(BF16)
            v = v_ref[s, :, cols].astype(BF16)
            sc = lax.dot_general(q, k, (((1,), (1,)), ((), ())),
                                 preferred_element_type=F32) * scale
            mx = jnp.max(sc, axis=-1, keepdims=True)
            e = jnp.exp(sc - mx)
            p = e / jnp.sum(e, axis=-1, keepdims=True)
            o = jnp.dot(p.astype(BF16), v, preferred_element_type=F32)
            o_ref[rows, cols] = o.astype(BF16)


def _xattn_prompt(q, k, v, *, seq_len, tq):
    b = k.shape[0]
    nt = seq_len // tq
    d = q.shape[1]
    return pl.pallas_call(
        functools.partial(_xattn_kernel, 1, tq),
        grid=(b, nt),
        in_specs=[
            pl.BlockSpec((tq, d), lambda i, j: (i * nt + j, 0)),
            pl.BlockSpec((1, N_MEM, d), lambda i, j: (i, 0, 0)),
            pl.BlockSpec((1, N_MEM, d), lambda i, j: (i, 0, 0)),
        ],
        out_specs=pl.BlockSpec((tq, d), lambda i, j: (i * nt + j, 0)),
        out_shape=jax.ShapeDtypeStruct((b * seq_len, d), BF16),
        compiler_params=_cparams(2),
        name="xattn_prompt",
    )(q, k, v)


def _xattn_cache_kernel(n_seq, tq, q_ref, k_ref, v_ref, o_ref):
    scale = XATTN_HEAD_DIM ** -0.5
    n_rows = XATTN_HEADS * tq
    n_cols = N_MEM * XATTN_HEADS
    own = (_lane_iota((n_rows, n_cols)) % XATTN_HEADS) == (_row_iota((n_rows, n_cols)) // tq)
    qf = q_ref[...].astype(F32)
    outs = []
    for s in range(n_seq):
        qs = qf[s * tq:(s + 1) * tq]
        q4 = jnp.concatenate(
            [qs[:, h * XATTN_HEAD_DIM:(h + 1) * XATTN_HEAD_DIM] for h in range(XATTN_HEADS)],
            axis=0).astype(BF16)
        kf = k_ref[s].reshape(n_cols, XATTN_HEAD_DIM).astype(BF16)
        vf = v_ref[s].reshape(n_cols, XATTN_HEAD_DIM).astype(BF16)
        sc = lax.dot_general(q4, kf, (((1,), (1,)), ((), ())), preferred_element_type=F32) * scale
        sc = jnp.where(own, sc, -1e30)
        mx = jnp.max(sc, axis=-1, keepdims=True)
        e = jnp.exp(sc - mx)
        p = e / jnp.sum(e, axis=-1, keepdims=True)
        o4 = jnp.dot(p.astype(BF16), vf, preferred_element_type=F32)
        outs.append(jnp.concatenate([o4[h * tq:(h + 1) * tq] for h in range(XATTN_HEADS)], axis=1))
    o_ref[...] = jnp.concatenate(outs, axis=0).astype(BF16)


def _xattn_sample(q, k, v, *, seq_len, n_seq, row_block_offset):
    b = k.shape[0]
    d = q.shape[1]
    rows = n_seq * seq_len
    kv_block = (n_seq, N_MEM, XATTN_HEADS, XATTN_HEAD_DIM)
    return pl.pallas_call(
        functools.partial(_xattn_cache_kernel, n_seq, seq_len),
        grid=(b // n_seq,),
        in_specs=[
            pl.BlockSpec((rows, d), lambda i: (row_block_offset + i, 0)),
            pl.BlockSpec(kv_block, lambda i: (i, 0, 0, 0)),
            pl.BlockSpec(kv_block, lambda i: (i, 0, 0, 0)),
        ],
        out_specs=pl.BlockSpec((rows, d), lambda i: (i, 0)),
        out_shape=jax.ShapeDtypeStruct((b * seq_len, d), BF16),
        compiler_params=_cparams(1),
        name="xattn_sample",
    )(q, k, v)


def _row_iota(shape):
    return lax.broadcasted_iota(jnp.int32, shape, 0)


def _lane_iota(shape):
    return lax.broadcasted_iota(jnp.int32, shape, 1)


def _seg_cumsum(x, seg_len):
    pos = _row_iota(x.shape) % seg_len
    s = 1
    while s < seg_len:
        x = x + jnp.where(pos >= s, pltpu.roll(x, s, axis=0), 0.0)
        s *= 2
    return x


def _seg_rev_excl_cumsum(x, seg_len):
    n = x.shape[0]
    pos = _row_iota(x.shape) % seg_len
    y = jnp.where(pos < seg_len - 1, pltpu.roll(x, n - 1, axis=0), 0.0)
    s = 1
    while s < seg_len:
        y = y + jnp.where(pos < seg_len - s, pltpu.roll(y, n - s, axis=0), 0.0)
        s *= 2
    return y


def _scan8(a, b):
    shape = a.shape
    tiled = (shape[0] // SUBLANES, SUBLANES, shape[1])
    a = a.reshape(tiled)
    b = b.reshape(tiled)
    pos = lax.broadcasted_iota(jnp.int32, tiled, 1)
    for s in (1, 2, 4):
        m = pos >= s
        a_sh = pltpu.roll(a, s, axis=1)
        b_sh = pltpu.roll(b, s, axis=1)
        b = jnp.where(m, a * b_sh + b, b)
        a = jnp.where(m, a * a_sh, a)
    return a.reshape(shape), b.reshape(shape)


def _conv_taps(x, w_ref, b_ref, cols, fix):
    acc = b_ref[:, cols] + x * w_ref[CONV_WIDTH - 1:CONV_WIDTH, cols]
    for s in range(1, CONV_WIDTH):
        xs = fix(s, pltpu.roll(x, s, axis=0))
        k = CONV_WIDTH - 1 - s
        acc = acc + xs * w_ref[k:k + 1, cols]
    return acc


def _bcast_col(v, c):
    return jnp.broadcast_to(v[:, c:c + 1], (v.shape[0], LANES))


def _lru_part(sample, pa_ref, prev_fix_lru, h0_rows, hcar_ref,
              lcw_ref, lcb_ref, wa_ref, wx_ref, ba_ref, bx_ref, lam_ref, lg_ref,
              mixed_ref, hs_scr, yl_scr):
    ssq = jnp.zeros((ROWS, 1), F32)
    for hd in range(LRU_HEADS):
        cols = slice(hd * LANES, (hd + 1) * LANES)
        xl = pa_ref[:, COL_XLRU + hd * LANES:COL_XLRU + (hd + 1) * LANES]
        u = _conv_taps(xl, lcw_ref, lcb_ref, cols, functools.partial(prev_fix_lru, cols))
        ub = u.astype(BF16)
        r = jax.nn.sigmoid(jnp.dot(ub, wa_ref[hd], preferred_element_type=F32) + ba_ref[:, cols])
        i = jax.nn.sigmoid(jnp.dot(ub, wx_ref[hd], preferred_element_type=F32) + bx_ref[:, cols])
        log_a = (-LRU_C * r) * _softplus(-lam_ref[:, cols])
        a = jnp.exp(log_a)
        th = jnp.tanh(log_a)
        beta = jnp.sqrt((-2.0 * th) / (1.0 - th))
        bb = (beta * i) * u
        a_cum, b_cum = _scan8(a, bb)
        if sample:
            h = a_cum * h0_rows(cols) + b_cum
        else:
            carry = hcar_ref[:, cols]
            parts = []
            for t in range(ROWS // SUBLANES):
                rs = slice(t * SUBLANES, (t + 1) * SUBLANES)
                ht = a_cum[rs] * carry + b_cum[rs]
                parts.append(ht)
                carry = ht[SUBLANES - 1:SUBLANES]
            h = jnp.concatenate(parts, axis=0)
            hcar_ref[:, cols] = carry
        hs_scr[:, cols] = h
        g = jax.nn.gelu(pa_ref[:, COL_GLRU + hd * LANES:COL_GLRU + (hd + 1) * LANES])
        yl = h * g
        ssq = ssq + jnp.sum(yl * yl, axis=-1, keepdims=True)
        yl_scr[:, cols] = yl
    inv = lax.rsqrt(ssq * (1.0 / LRU_WIDTH) + EPS)
    for hd in range(LRU_HEADS):
        cols = slice(hd * LANES, (hd + 1) * LANES)
        mixed_ref[:, cols] = ((yl_scr[:, cols] * inv) * lg_ref[:, cols]).astype(BF16)


def _ssd_front(sample, pb_ref, prev_fix_ssd, scw_ref, scb_ref, dtb_ref, alog_ref, dexp_ref,
               xc_scr, xw_scr, ecs_scr, y_scr, xt_scr, dec_scr, c_scr, b_scr):
    seg = SUBLANES if sample else ROWS
    n_seq = ROWS // seg

    for t in range(SSD_CONV_DIM // LANES):
        cols = slice(t * LANES, (t + 1) * LANES)
        xb = pb_ref[:, COL_XBC + t * LANES:COL_XBC + (t + 1) * LANES]
        v = _conv_taps(xb, scw_ref, scb_ref, cols, functools.partial(prev_fix_ssd, cols))
        xc_scr[:, cols] = _silu(v)

    dt = _softplus(pb_ref[:, COL_DT:COL_DT + LANES] + dtb_ref[...])
    a_neg = -jnp.exp(alog_ref[...])
    d_a = dt * a_neg
    cs = _seg_cumsum(d_a, seg)
    rcs = _seg_rev_excl_cumsum(d_a, seg)
    todt = jnp.exp(rcs) * dt
    dec_tot = jnp.exp(cs + rcs)
    cs_t = cs.T
    dt_t = dt.T
    dec_t = dec_tot.T
    for s in range(n_seq):
        dec_scr[s] = jnp.broadcast_to(dec_t[0:SSD_HEADS, s * seg:s * seg + 1], (SSD_HEADS, LANES))

    ri = _row_iota((ROWS, ROWS))
    ci = _lane_iota((ROWS, ROWS))
    causal = ri >= ci
    if sample:
        causal = jnp.logical_and(causal, (ri // seg) == (ci // seg))
    lo = _lane_iota((ROWS, LANES)) < SSD_HEADDIM

    b_off = SSD_WIDTH
    c_off = SSD_WIDTH + SSD_GROUPS * SSD_STATE
    for g in range(SSD_GROUPS):
        gcols = slice(g * SSD_STATE, (g + 1) * SSD_STATE)
        bg = xc_scr[:, b_off + g * SSD_STATE:b_off + (g + 1) * SSD_STATE].astype(BF16)
        cg = xc_scr[:, c_off + g * SSD_STATE:c_off + (g + 1) * SSD_STATE].astype(BF16)
        c_scr[:, gcols] = xc_scr[:, c_off + g * SSD_STATE:c_off + (g + 1) * SSD_STATE]
        b_scr[:, gcols] = bg
        cb = lax.dot_general(cg, bg, (((1,), (1,)), ((), ())), preferred_element_type=F32)
        heads_per_group = SSD_HEADS // SSD_GROUPS
        for jp in range(heads_per_group // 2):
            h0 = g * heads_per_group + 2 * jp
            h1 = h0 + 1
            cols = slice(h0 * SSD_HEADDIM, h0 * SSD_HEADDIM + LANES)
            x_pair = xc_scr[:, cols]
            ws = []
            cs_cols = []
            for hh in (h0, h1):
                cs_col = _bcast_col(cs, hh)
                cs_cols.append(cs_col)
                dec = jnp.where(causal, jnp.exp(cs_col - cs_t[hh:hh + 1, :]), 0.0)
                ws.append(((cb * dec) * dt_t[hh:hh + 1, :]).astype(BF16))
            lhs = jnp.concatenate(ws, axis=1)
            rhs = jnp.concatenate([jnp.where(lo, x_pair, 0.0), jnp.where(lo, 0.0, x_pair)],
                                  axis=0).astype(BF16)
            y_diag = jnp.dot(lhs, rhs, preferred_element_type=F32)
            y_scr[:, cols] = y_diag + x_pair * dexp_ref[:, cols]
            xw_scr[:, cols] = x_pair * jnp.where(lo, _bcast_col(todt, h0), _bcast_col(todt, h1))
            ecs_scr[:, cols] = jnp.exp(jnp.where(lo, cs_cols[0], cs_cols[1]))
    for t in range(SSD_WIDTH // LANES):
        xt_scr[t * LANES:(t + 1) * LANES, :] = xw_scr[:, t * LANES:(t + 1) * LANES].T.astype(BF16)


def _ssd_state_step(c_rows, b_mask, s_old, dec, xt_scr):
    y_parts = []
    s_parts = []
    heads_per_group = SSD_HEADS // SSD_GROUPS
    for g in range(SSD_GROUPS):
        gcols = slice(g * SSD_STATE, (g + 1) * SSD_STATE)
        grows = slice(g * GROUP_COLS, (g + 1) * GROUP_COLS)
        sg = s_old[grows]
        y_parts.append(lax.dot_general(c_rows[:, gcols], sg.astype(BF16), (((1,), (1,)), ((), ())),
                                       preferred_element_type=F32))
        upd = jnp.dot(xt_scr[grows, :], b_mask[:, gcols], preferred_element_type=F32)
        for hh in range(heads_per_group):
            hr = slice(hh * SSD_HEADDIM, (hh + 1) * SSD_HEADDIM)
            habs = g * heads_per_group + hh
            s_parts.append(sg[hr] * dec[habs:habs + 1, :] + upd[hr])
    return jnp.concatenate(y_parts, axis=1), jnp.concatenate(s_parts, axis=0)


def _ssd_finish(pa_ref, y_scr, sg_ref, mixed_ref):
    for g in range(SSD_GROUPS):
        ssq = jnp.zeros((ROWS, 1), F32)
        n_t = GROUP_COLS // LANES
        for t in range(n_t):
            c0 = g * GROUP_COLS + t * LANES
            yz = y_scr[:, c0:c0 + LANES] * _silu(pa_ref[:, COL_Z + c0:COL_Z + c0 + LANES])
            y_scr[:, c0:c0 + LANES] = yz
            ssq = ssq + jnp.sum(yz * yz, axis=-1, keepdims=True)
        inv = lax.rsqrt(ssq * (1.0 / GROUP_COLS) + EPS)
        for t in range(n_t):
            c0 = g * GROUP_COLS + t * LANES
            mixed_ref[:, LRU_WIDTH + c0:LRU_WIDTH + c0 + LANES] = (
                (y_scr[:, c0:c0 + LANES] * inv) * sg_ref[:, c0:c0 + LANES]).astype(BF16)


def _mixer_prompt_kernel(x_ref, xn_ref, mg_ref, *rest):
    n_slabs = D_A // W_A_TILE
    wina_refs = rest[:n_slabs]
    (pb_ref,
     lcw_ref, lcb_ref, wa_ref, wx_ref, ba_ref, bx_ref, lam_ref, lg_ref,
     scw_ref, scb_ref, dtb_ref, alog_ref, dexp_ref, sg_ref,
     mixed_ref, lconv_ref, lh_ref, sconv_ref, sh_ref,
     pa_scr, hn_scr, hcar_ref, ltail_scr, stail_scr, s_scr,
     hs_scr, yl_scr, xc_scr, xw_scr, ecs_scr, y_scr, xt_scr, dec_scr, c_scr, b_scr) = rest[n_slabs:]
    c = pl.program_id(1)
    step = pl.program_id(0) * pl.num_programs(1) + c
    slot = step % 2

    @pl.when(c == 0)
    def _():
        hcar_ref[...] = jnp.zeros_like(hcar_ref)
        ltail_scr[...] = jnp.zeros_like(ltail_scr)
        stail_scr[...] = jnp.zeros_like(stail_scr)
        s_scr[...] = jnp.zeros_like(s_scr)

    @pl.when(step == 0)
    def _():
        hn = _rmsnorm_rows(x_ref[...], mg_ref[...]).astype(BF16)
        for j in range(n_slabs):
            pa_scr[0, :, j * W_A_TILE:(j + 1) * W_A_TILE] = jnp.dot(
                hn, wina_refs[j][...], preferred_element_type=F32)

    hn_scr[...] = _rmsnorm_rows(xn_ref[...], mg_ref[...]).astype(BF16)
    pa_next = pa_scr.at[1 - slot]
    pa_cur = pa_scr.at[slot]

    for j in range(n_slabs):
        pa_next[:, j * W_A_TILE:(j + 1) * W_A_TILE] = jnp.dot(
            hn_scr[...], wina_refs[j][...], preferred_element_type=F32)

    pos8 = _row_iota((SUBLANES, LANES))

    def make_fix(tail_scr):
        def fix(cols, s, rolled):
            first = jnp.where(pos8 < s, pltpu.roll(tail_scr[:, cols], s, axis=0), rolled[0:SUBLANES])
            return jnp.concatenate([first, rolled[SUBLANES:]], axis=0)
        return fix

    _lru_part(False, pa_cur, make_fix(ltail_scr), None, hcar_ref,
              lcw_ref, lcb_ref, wa_ref, wx_ref, ba_ref, bx_ref, lam_ref, lg_ref,
              mixed_ref, hs_scr, yl_scr)
    _ssd_front(False, pb_ref, make_fix(stail_scr), scw_ref, scb_ref, dtb_ref, alog_ref, dexp_ref,
               xc_scr, xw_scr, ecs_scr, y_scr, xt_scr, dec_scr, c_scr, b_scr)

    ltail_scr[...] = pa_cur[ROWS - SUBLANES:ROWS, COL_XLRU:COL_XLRU + LRU_WIDTH]
    stail_scr[...] = pb_ref[ROWS - SUBLANES:ROWS, COL_XBC:COL_XBC + SSD_CONV_DIM]

    y_off, s_new = _ssd_state_step(c_scr[...].astype(BF16), b_scr[...], s_scr[...], dec_scr[0], xt_scr)
    s_scr[...] = s_new
    for t in range(SSD_WIDTH // LANES):
        cols = slice(t * LANES, (t + 1) * LANES)
        y_scr[:, cols] = y_scr[:, cols] + ecs_scr[:, cols] * y_off[:, cols]
    _ssd_finish(pa_cur, y_scr, sg_ref, mixed_ref)

    @pl.when(c == pl.num_programs(1) - 1)
    def _():
        lconv_ref[0] = pa_cur[ROWS - (CONV_WIDTH - 1):ROWS, COL_XLRU:COL_XLRU + LRU_WIDTH]
        sconv_ref[0] = pb_ref[ROWS - (CONV_WIDTH - 1):ROWS, COL_XBC:COL_XBC + SSD_CONV_DIM]
        lh_ref[0] = hcar_ref[...]
        sh_ref[0] = s_scr[...]


def _mixer_sample_kernel(n_inner, pa_ref, pb_ref, lprev_ref, sprev_ref, h0_ref, sin_ref,
                         lcw_ref, lcb_ref, wa_ref, wx_ref, ba_ref, bx_ref, lam_ref, lg_ref,
                         scw_ref, scb_ref, dtb_ref, alog_ref, dexp_ref, sg_ref,
                         mixed_ref, lconv_ref, lh_ref, sconv_ref, sh_ref,
                         hs_scr, yl_scr, xc_scr, xw_scr, ecs_scr, y_scr, xt_scr, dec_scr, c_scr, b_scr):
    i = pl.program_id(1)
    seg = SUBLANES
    n_seq = ROWS // seg
    seq_per_step = n_seq // n_inner

    @pl.when(i == 0)
    def _():
        pos = _row_iota((ROWS, LANES)) % seg

        def make_fix(prev_ref):
            def fix(cols, s, rolled):
                return jnp.where(pos < s, pltpu.roll(prev_ref[:, cols], ROWS - seg + s, axis=0), rolled)
            return fix

        def h0_rows(cols):
            return jnp.concatenate(
                [jnp.broadcast_to(h0_ref[q:q + 1, cols], (seg, LANES)) for q in range(n_seq)], axis=0)

        _ssd_front(True, pb_ref, make_fix(sprev_ref), scw_ref, scb_ref, dtb_ref, alog_ref, dexp_ref,
                   xc_scr, xw_scr, ecs_scr, y_scr, xt_scr, dec_scr, c_scr, b_scr)
        _lru_part(True, pa_ref, make_fix(lprev_ref), h0_rows, None,
                  lcw_ref, lcb_ref, wa_ref, wx_ref, ba_ref, bx_ref, lam_ref, lg_ref,
                  mixed_ref, hs_scr, yl_scr)
        for sq in range(n_seq):
            tail = slice((sq + 1) * seg - (CONV_WIDTH - 1), (sq + 1) * seg)
            lconv_ref[sq] = pa_ref[tail, COL_XLRU:COL_XLRU + LRU_WIDTH]
            sconv_ref[sq] = pb_ref[tail, COL_XBC:COL_XBC + SSD_CONV_DIM]
            lh_ref[sq:sq + 1, :] = hs_scr[(sq + 1) * seg - 1:(sq + 1) * seg, :]

    rgroup = _row_iota((ROWS, 2 * SSD_STATE)) // seg
    for jj in range(seq_per_step):
        q = i * seq_per_step + jj
        r0 = pl.multiple_of(q * seg, seg)
        c_rows = c_scr[pl.ds(r0, seg), :].astype(BF16)
        b_all = b_scr[...]
        b_mask = jnp.where(rgroup == q, b_all, jnp.zeros_like(b_all))
        y_off, s_new = _ssd_state_step(c_rows, b_mask, sin_ref[jj], dec_scr[q], xt_scr)
        sh_ref[jj] = s_new
        y_scr[pl.ds(r0, seg), :] = y_scr[pl.ds(r0, seg), :] + ecs_scr[pl.ds(r0, seg), :] * y_off

    @pl.when(i == n_inner - 1)
    def _():
        _ssd_finish(pa_ref, y_scr, sg_ref, mixed_ref)


def _mixer_scratch(n_dec):
    return [
        pltpu.VMEM((ROWS, LRU_WIDTH), F32),
        pltpu.VMEM((ROWS, LRU_WIDTH), F32),
        pltpu.VMEM((ROWS, SSD_CONV_DIM), F32),
        pltpu.VMEM((ROWS, SSD_WIDTH), F32),
        pltpu.VMEM((ROWS, SSD_WIDTH), F32),
        pltpu.VMEM((ROWS, SSD_WIDTH), F32),
        pltpu.VMEM((SSD_WIDTH, ROWS), BF16),
        pltpu.VMEM((n_dec, SSD_HEADS, LANES), F32),
        pltpu.VMEM((ROWS, 2 * SSD_STATE), F32),
        pltpu.VMEM((ROWS, 2 * SSD_STATE), BF16),
    ]


def _full_spec(shape, n_grid):
    zeros = (0,) * len(shape)
    if n_grid == 2:
        return pl.BlockSpec(shape, lambda i, j: zeros)
    return pl.BlockSpec(shape, lambda i: zeros)


def _mixer_weight_specs(n_grid):
    return [
        _full_spec((CONV_WIDTH, LRU_WIDTH), n_grid),
        _full_spec((1, LRU_WIDTH), n_grid),
        _full_spec((LRU_HEADS, LANES, LANES), n_grid),
        _full_spec((LRU_HEADS, LANES, LANES), n_grid),
        _full_spec((1, LRU_WIDTH), n_grid),
        _full_spec((1, LRU_WIDTH), n_grid),
        _full_spec((1, LRU_WIDTH), n_grid),
        _full_spec((1, LRU_WIDTH), n_grid),
        _full_spec((CONV_WIDTH, SSD_CONV_DIM), n_grid),
        _full_spec((1, SSD_CONV_DIM), n_grid),
        _full_spec((1, LANES), n_grid),
        _full_spec((1, LANES), n_grid),
        _full_spec((1, SSD_WIDTH), n_grid),
        _full_spec((1, SSD_WIDTH), n_grid),
    ]


def _mixer_prompt(x1, mix_g, w_in_a, proj_b, weights, *, batch, seq_len):
    nc = seq_len // ROWS
    n_slabs = D_A // W_A_TILE

    def slab_spec(j):
        return pl.BlockSpec((D_MODEL, W_A_TILE), lambda b, c: (0, j), pipeline_mode=pl.Buffered(1))

    return pl.pallas_call(
        _mixer_prompt_kernel,
        grid=(batch, nc),
        in_specs=[
            pl.BlockSpec((ROWS, D_MODEL), lambda b, c: (b * nc + c, 0)),
            pl.BlockSpec((ROWS, D_MODEL), lambda b, c: (jnp.minimum(b * nc + c + 1, batch * nc - 1), 0)),
            pl.BlockSpec((1, D_MODEL), lambda b, c: (0, 0)),
        ] + [slab_spec(j) for j in range(n_slabs)] + [
            pl.BlockSpec((ROWS, D_B_PAD), lambda b, c: (b * nc + c, 0)),
        ] + _mixer_weight_specs(2),
        out_specs=[
            pl.BlockSpec((ROWS, LRU_WIDTH + SSD_WIDTH), lambda b, c: (b * nc + c, 0)),
            pl.BlockSpec((1, CONV_WIDTH - 1, LRU_WIDTH), lambda b, c: (b, 0, 0)),
            pl.BlockSpec((1, 1, LRU_WIDTH), lambda b, c: (b, 0, 0)),
            pl.BlockSpec((1, CONV_WIDTH - 1, SSD_CONV_DIM), lambda b, c: (b, 0, 0)),
            pl.BlockSpec((1, SSD_WIDTH, SSD_STATE), lambda b, c: (b, 0, 0)),
        ],
        out_shape=[
            jax.ShapeDtypeStruct((batch * seq_len, LRU_WIDTH + SSD_WIDTH), BF16),
            jax.ShapeDtypeStruct((batch, CONV_WIDTH - 1, LRU_WIDTH), F32),
            jax.ShapeDtypeStruct((batch, 1, LRU_WIDTH), F32),
            jax.ShapeDtypeStruct((batch, CONV_WIDTH - 1, SSD_CONV_DIM), F32),
            jax.ShapeDtypeStruct((batch, SSD_WIDTH, SSD_STATE), F32),
        ],
        scratch_shapes=[
            pltpu.VMEM((2, ROWS, D_A), F32),
            pltpu.VMEM((ROWS, D_MODEL), BF16),
            pltpu.VMEM((1, LRU_WIDTH), F32),
            pltpu.VMEM((SUBLANES, LRU_WIDTH), F32),
            pltpu.VMEM((SUBLANES, SSD_CONV_DIM), F32),
            pltpu.VMEM((SSD_WIDTH, SSD_STATE), F32),
        ] + _mixer_scratch(1),
        compiler_params=_cparams(2),
        name="mixer_prompt",
    )(x1, x1, mix_g, *([w_in_a] * n_slabs), proj_b, *weights)


def _mixer_sample(proj_a, proj_b, lprev, sprev, h0, s_in, weights, *, batch, seq_len, row_block_offset,
                  n_inner=SAMPLE_STATE_STEPS):
    n_seq = ROWS // seq_len
    n_outer = batch // n_seq
    sps = n_seq // n_inner
    return pl.pallas_call(
        functools.partial(_mixer_sample_kernel, n_inner),
        grid=(n_outer, n_inner),
        in_specs=[
            pl.BlockSpec((ROWS, D_A), lambda o, i: (o, 0)),
            pl.BlockSpec((ROWS, D_B_PAD), lambda o, i: (row_block_offset + o, 0)),
            pl.BlockSpec((ROWS, LRU_WIDTH), lambda o, i: (o, 0)),
            pl.BlockSpec((ROWS, SSD_CONV_DIM), lambda o, i: (o, 0)),
            pl.BlockSpec((n_seq, LRU_WIDTH), lambda o, i: (o, 0)),
            pl.BlockSpec((sps, SSD_WIDTH, SSD_STATE), lambda o, i: (o * n_inner + i, 0, 0)),
        ] + _mixer_weight_specs(2),
        out_specs=[
            pl.BlockSpec((ROWS, LRU_WIDTH + SSD_WIDTH), lambda o, i: (o, 0)),
            pl.BlockSpec((n_seq, CONV_WIDTH - 1, LRU_WIDTH), lambda o, i: (o, 0, 0)),
            pl.BlockSpec((n_seq, LRU_WIDTH), lambda o, i: (o, 0)),
            pl.BlockSpec((n_seq, CONV_WIDTH - 1, SSD_CONV_DIM), lambda o, i: (o, 0, 0)),
            pl.BlockSpec((sps, SSD_WIDTH, SSD_STATE), lambda o, i: (o * n_inner + i, 0, 0)),
        ],
        out_shape=[
            jax.ShapeDtypeStruct((batch * seq_len, LRU_WIDTH + SSD_WIDTH), BF16),
            jax.ShapeDtypeStruct((batch, CONV_WIDTH - 1, LRU_WIDTH), F32),
            jax.ShapeDtypeStruct((batch, LRU_WIDTH), F32),
            jax.ShapeDtypeStruct((batch, CONV_WIDTH - 1, SSD_CONV_DIM), F32),
            jax.ShapeDtypeStruct((batch, SSD_WIDTH, SSD_STATE), F32),
        ],
        scratch_shapes=_mixer_scratch(n_seq),
        compiler_params=_cparams(2),
        name="mixer_sample",
    )(proj_a, proj_b, lprev, sprev, h0, s_in, *weights)


def _row(v):
    return v.reshape(1, -1).astype(F32)


def _pad_lanes(v):
    v = v.reshape(1, -1).astype(F32)
    return jnp.pad(v, ((0, 0), (0, LANES - v.shape[1])))


def kernel(x_prompt, mem_prompt, x_sample, cache_mem_k, cache_mem_v, state_lru_conv, state_lru_h, state_ssd_conv, state_ssd_h, ffn1_norm_g, ffn1_w_gate, ffn1_w_up, ffn1_w_down, mix_norm_g, w_in, lru_conv_w, lru_conv_b, lru_w_a, lru_b_a, lru_w_x, lru_b_x, lru_lambda, lru_out_norm_g, ssd_conv_w, ssd_conv_b, ssd_dt_bias, ssd_a_log, ssd_d, ssd_out_norm_g, w_out, xattn_norm_g, mem_norm_g, xattn_w_q, xattn_w_k, xattn_w_v, xattn_w_o, ffn2_norm_g, ffn2_w_gate, ffn2_w_up, ffn2_w_down, final_norm_g):
    depth = ffn1_norm_g.shape[0]
    assert depth == 1
    bp, tp, d = x_prompt.shape
    bs, ts, _ = x_sample.shape
    mp = bp * tp
    ms = bs * ts
    assert tp % ROWS == 0 and ROWS % ts == 0 and ts == SUBLANES and mp % ROWS == 0
    l = 0

    mixer_weights = (
        lru_conv_w[l], _row(lru_conv_b[l]), lru_w_a[l].astype(BF16), lru_w_x[l].astype(BF16),
        _row(lru_b_a[l]), _row(lru_b_x[l]), _row(lru_lambda[l]), _row(lru_out_norm_g[l]),
        ssd_conv_w[l], _row(ssd_conv_b[l]), _pad_lanes(ssd_dt_bias[l]), _pad_lanes(ssd_a_log[l]),
        _row(jnp.repeat(ssd_d[l], SSD_HEADDIM)), _row(ssd_out_norm_g[l]),
    )

    x1, w_out_bf, w_q_bf, w_k_bf, w_v_bf, w_o_bf = _ffn(
        (x_prompt.reshape(mp, d), x_sample.reshape(ms, d)), _row(ffn1_norm_g[l]),
        ffn1_w_gate[l].astype(BF16), ffn1_w_up[l].astype(BF16), ffn1_w_down[l].astype(BF16),
        _row(final_norm_g), out_rows=(mp + ms,), final_norm=False,
        cast_srcs=(w_out[l], xattn_w_q[l], xattn_w_k[l], xattn_w_v[l], xattn_w_o[l]))
    w_in_bf = w_in[l].astype(BF16)
    w_in_b = jnp.pad(w_in_bf[:, D_A:], ((0, 0), (0, D_B_PAD - D_B)))
    mix_g = _row(mix_norm_g[l])
    proj_b, ffn2_wg = _norm_matmul(x1, mix_g, w_in_b, out_dtype=F32, name="in_proj_b", tn=D_B_PAD,
                                   tm=RESIDENT_TM, cast_src=ffn2_w_gate[l])
    proj_a_s = _norm_matmul(x1, mix_g, w_in_bf, out_dtype=F32, name="in_proj_a_sample",
                            row_offset=mp, n_cols=D_A)

    mixed_p, p_lc, p_lh, p_sc, p_sh = _mixer_prompt(x1, mix_g, w_in_bf, proj_b, mixer_weights,
                                                    batch=bp, seq_len=tp)

    pad_rows = ((0, 0), (SUBLANES - (CONV_WIDTH - 1), 0), (0, 0))
    lprev = jnp.pad(state_lru_conv[l], pad_rows).reshape(ms, LRU_WIDTH)
    sprev = jnp.pad(state_ssd_conv[l], pad_rows).reshape(ms, SSD_CONV_DIM)
    mixed_s, s_lc, s_lh, s_sc, s_sh = _mixer_sample(
        proj_a_s, proj_b, lprev, sprev, state_lru_h[l], state_ssd_h[l].reshape(bs, SSD_WIDTH, SSD_STATE),
        mixer_weights, batch=bs, seq_len=ts, row_block_offset=mp // ROWS)

    x2 = _matmul_residual(mixed_p, mixed_s, w_out_bf, x1, tn=d, tm=RESIDENT_TM, name="out_proj")

    q, ffn2_wu = _norm_matmul(x2, _row(xattn_norm_g[l]), w_q_bf, out_dtype=BF16,
                              name="q_proj", tn=d, cast_src=ffn2_w_up[l])
    mem = mem_prompt.reshape(bp * N_MEM, d)
    mk = _norm_matmul(mem, _row(mem_norm_g[l]), w_k_bf, out_dtype=F32, name="mem_k",
                      tn=d, tm=RESIDENT_TM)
    mv = _norm_matmul(mem, _row(mem_norm_g[l]), w_v_bf, out_dtype=F32, name="mem_v",
                      tn=d, tm=RESIDENT_TM)
    o_p = _xattn_prompt(q, mk.reshape(bp, N_MEM, d), mv.reshape(bp, N_MEM, d), seq_len=tp, tq=XATTN_TQ)
    o_s = _xattn_sample(q, cache_mem_k[l], cache_mem_v[l], seq_len=ts, n_seq=XATTN_CACHE_SEQS,
                        row_block_offset=mp // (XATTN_CACHE_SEQS * ts))
    x3, ffn2_wd = _matmul_residual(o_p, o_s, w_o_bf, x2, tn=d, tm=RESIDENT_TM,
                                   name="o_proj", cast_src=ffn2_w_down[l])

    y_p, y_s = _ffn((x3,), _row(ffn2_norm_g[l]), ffn2_wg, ffn2_wu, ffn2_wd, _row(final_norm_g),
                    out_rows=(mp, ms), final_norm=True)

    y_prompt = y_p.reshape(bp, tp, d)
    y_sample = y_s.reshape(bs, ts, d)
    hshape = (SSD_HEADS, SSD_HEADDIM, SSD_STATE)
    return (y_prompt, y_sample,
            p_lc[None], p_lh.reshape(1, bp, LRU_WIDTH), p_sc[None], p_sh.reshape((1, bp) + hshape),
            mk.reshape(1, bp, N_MEM, XATTN_HEADS, XATTN_HEAD_DIM),
            mv.reshape(1, bp, N_MEM, XATTN_HEADS, XATTN_HEAD_DIM),
            s_lc[None], s_lh[None], s_sc[None], s_sh.reshape((1, bs) + hshape))
```

```python
import functools

import jax
import jax.numpy as jnp
from jax import lax
from jax.experimental import pallas as pl
from jax.experimental.pallas import tpu as pltpu

F32 = jnp.float32
BF16 = jnp.bfloat16

D_MODEL = 2048
LRU_WIDTH = 2048
LRU_HEADS = 16
LRU_C = 8.0
CONV_WIDTH = 4
SSD_WIDTH = 2048
SSD_HEADDIM = 64
SSD_HEADS = 32
SSD_GROUPS = 2
SSD_STATE = 128
SSD_CONV_DIM = SSD_WIDTH + 2 * SSD_GROUPS * SSD_STATE
D_IN = 3 * 2048 + SSD_CONV_DIM + SSD_HEADS
N_MEM = 256
XATTN_HEADS = 4
XATTN_HEAD_DIM = 512
EPS = 1e-6

LANES = 128
SUBLANES = 8
VMEM_LIMIT_BYTES = 56 * 1024 * 1024

D_A = 3 * 2048
W_A_TILE = 512
COL_XLRU = 0
COL_GLRU = 2048
COL_Z = 4096
D_B = SSD_CONV_DIM + SSD_HEADS
D_B_PAD = 3072
COL_XBC = 0
COL_DT = SSD_CONV_DIM
ROWS = 128
GROUP_COLS = SSD_WIDTH // SSD_GROUPS

FFN_TM, FFN_TF = 512, 512
PROJ_TM, PROJ_TN = 1024, 1024
RESIDENT_TM = 512
XATTN_TQ = 2048
XATTN_CACHE_SEQS = 4
SAMPLE_STATE_STEPS = 4


def _cparams(n_axes):
    return pltpu.CompilerParams(
        dimension_semantics=("arbitrary",) * n_axes,
        vmem_limit_bytes=VMEM_LIMIT_BYTES)


def _rmsnorm_rows(x, g):
    ms = jnp.mean(x * x, axis=-1, keepdims=True)
    return (x * lax.rsqrt(ms + EPS)) * g


def _softplus(x):
    return jnp.maximum(x, 0.0) + jnp.log1p(jnp.exp(-jnp.abs(x)))


def _silu(x):
    return x * jax.nn.sigmoid(x)


def _when_rows(i, n_a, fn_a, fn_b, extra=None):
    in_a = i < n_a
    in_b = i >= n_a
    if extra is not None:
        in_a = jnp.logical_and(in_a, extra)
        in_b = jnp.logical_and(in_b, extra)
    pl.when(in_a)(fn_a)
    pl.when(in_b)(fn_b)


def _ffn_kernel(n_a, split_in, split_out, final_norm, n_riders, *refs):
    refs = list(refs)
    xa_ref = refs.pop(0)
    xb_ref = refs.pop(0) if split_in else xa_ref
    g_ref, wg_ref, wu_ref, wd_ref, gf_ref = refs[:5]
    refs = refs[5:]
    src_refs = [refs.pop(0) for _ in range(n_riders)]
    oa_ref = refs.pop(0)
    ob_ref = refs.pop(0) if split_out else oa_ref
    for src_ref in src_refs:
        refs.pop(0)[...] = src_ref[...].astype(BF16)
    if split_out:
        h_scr, acc_scr = refs[-2:]
    else:
        h_scr, acc_scr = refs[-1], oa_ref
    i = pl.program_id(0)
    j = pl.program_id(1)

    def prologue(x_ref):
        def run():
            x = x_ref[...]
            h_scr[...] = _rmsnorm_rows(x, g_ref[...]).astype(BF16)
            acc_scr[...] = x
        return run

    if split_in:
        _when_rows(i, n_a, prologue(xa_ref), prologue(xb_ref), extra=(j == 0))
    else:
        pl.when(j == 0)(prologue(xa_ref))

    h = h_scr[...]
    gate = jnp.dot(h, wg_ref[...], preferred_element_type=F32)
    up = jnp.dot(h, wu_ref[...], preferred_element_type=F32)
    a = (0.5 * _silu(gate) * up).astype(BF16)
    acc_scr[...] += jnp.dot(a, wd_ref[...], preferred_element_type=F32)

    def epilogue(o_ref):
        def run():
            res = acc_scr[...]
            if final_norm:
                res = _rmsnorm_rows(res, gf_ref[...])
            o_ref[...] = res
        return run

    last = j == pl.num_programs(1) - 1
    if split_out:
        _when_rows(i, n_a, epilogue(oa_ref), epilogue(ob_ref), extra=last)
    elif final_norm:
        pl.when(last)(epilogue(oa_ref))


def _ffn(xs, g, wg, wu, wd, gf, *, out_rows, final_norm, cast_srcs=()):
    tm, tf = FFN_TM, FFN_TF
    split_in = len(xs) == 2
    split_out = len(out_rows) == 2
    d = xs[0].shape[1]
    f = wg.shape[1]
    m = sum(x.shape[0] for x in xs)
    assert m == sum(out_rows) and all(x.shape[0] % tm == 0 for x in xs) and all(r % tm == 0 for r in out_rows)
    n_a = (xs[0].shape[0] if split_in else out_rows[0]) // tm
    if split_in and split_out:
        assert xs[0].shape[0] == out_rows[0]

    def first(i, j):
        return (jnp.minimum(i, n_a - 1), 0)

    def second(i, j):
        return (jnp.maximum(i - n_a, 0), 0)

    def whole(i, j):
        return (i, 0)

    x_specs = ([pl.BlockSpec((tm, d), first), pl.BlockSpec((tm, d), second)] if split_in
               else [pl.BlockSpec((tm, d), whole)])
    o_specs = ([pl.BlockSpec((tm, d), first), pl.BlockSpec((tm, d), second)] if split_out
               else [pl.BlockSpec((tm, d), whole)])
    grid = (m // tm, f // tf)
    riders = [_cast_rider(src, grid) for src in cast_srcs]
    return pl.pallas_call(
        functools.partial(_ffn_kernel, n_a, split_in, split_out, final_norm, len(riders)),
        grid=grid,
        in_specs=x_specs + [
            pl.BlockSpec((1, d), lambda i, j: (0, 0)),
            pl.BlockSpec((d, tf), lambda i, j: (0, j)),
            pl.BlockSpec((d, tf), lambda i, j: (0, j)),
            pl.BlockSpec((tf, d), lambda i, j: (j, 0)),
            pl.BlockSpec((1, d), lambda i, j: (0, 0)),
        ] + [spec for spec, _ in riders],
        out_specs=o_specs + [spec for spec, _ in riders],
        out_shape=[jax.ShapeDtypeStruct((r, d), F32) for r in out_rows] + [shape for _, shape in riders],
        scratch_shapes=[pltpu.VMEM((tm, d), BF16)] + ([pltpu.VMEM((tm, d), F32)] if split_out else []),
        compiler_params=_cparams(2),
        name="ffn_final" if final_norm else "ffn",
    )(*xs, g, wg, wu, wd, gf, *cast_srcs)


def _weight_spec(k, n, tn):
    if tn == n:
        return pl.BlockSpec((k, n), lambda i, j: (0, 0), pipeline_mode=pl.Buffered(1))
    return pl.BlockSpec((k, tn), lambda i, j: (0, j))


def _cast_rider(src, grid):
    rows, cols = src.shape
    n_steps = grid[0] * grid[1]
    n_blocks = max(b for b in range(1, n_steps + 1) if rows % b == 0 and (rows // b) % 16 == 0)
    inner = grid[1]
    spec = pl.BlockSpec((rows // n_blocks, cols),
                        lambda i, j: (jnp.minimum(i * inner + j, n_blocks - 1), 0))
    return spec, jax.ShapeDtypeStruct((rows, cols), BF16)


def _norm_matmul_kernel(has_rider, x_ref, g_ref, w_ref, *rest):
    if has_rider:
        src_ref, o_ref, dst_ref, h_scr = rest
        dst_ref[...] = src_ref[...].astype(BF16)
    else:
        o_ref, h_scr = rest

    @pl.when(pl.program_id(1) == 0)
    def _():
        h_scr[...] = _rmsnorm_rows(x_ref[...], g_ref[...]).astype(BF16)

    o_ref[...] = jnp.dot(h_scr[...], w_ref[...], preferred_element_type=F32).astype(o_ref.dtype)


def _norm_matmul(x, g, w, *, out_dtype, name, row_offset=0, n_cols=None, tn=PROJ_TN, tm=PROJ_TM,
                 cast_src=None):
    k = x.shape[1]
    m = x.shape[0] - row_offset
    n = w.shape[1] if n_cols is None else n_cols
    assert m % tm == 0 and row_offset % tm == 0 and n % tn == 0
    first = row_offset // tm
    in_specs = [
        pl.BlockSpec((tm, k), lambda i, j: (first + i, 0)),
        pl.BlockSpec((1, k), lambda i, j: (0, 0)),
        _weight_spec(k, n, tn),
    ]
    out_specs = [pl.BlockSpec((tm, tn), lambda i, j: (i, j))]
    out_shape = [jax.ShapeDtypeStruct((m, n), out_dtype)]
    operands = [x, g, w]
    if cast_src is not None:
        spec, shape = _cast_rider(cast_src, (m // tm, n // tn))
        in_specs.append(spec)
        out_specs.append(spec)
        out_shape.append(shape)
        operands.append(cast_src)
    outs = pl.pallas_call(
        functools.partial(_norm_matmul_kernel, cast_src is not None),
        grid=(m // tm, n // tn),
        in_specs=in_specs,
        out_specs=out_specs,
        out_shape=out_shape,
        scratch_shapes=[pltpu.VMEM((tm, k), BF16)],
        compiler_params=_cparams(2),
        name=name,
    )(*operands)
    return outs[0] if cast_src is None else tuple(outs)


def _matmul_residual_kernel(n_a, has_rider, aa_ref, ab_ref, w_ref, r_ref, *rest):
    if has_rider:
        src_ref, o_ref, dst_ref = rest
        dst_ref[...] = src_ref[...].astype(BF16)
    else:
        (o_ref,) = rest

    def run(a_ref):
        def body():
            o_ref[...] = r_ref[...] + jnp.dot(a_ref[...], w_ref[...], preferred_element_type=F32)
        return body

    _when_rows(pl.program_id(0), n_a, run(aa_ref), run(ab_ref))


def _matmul_residual(a_first, a_second, w, res, *, tn, name, tm=PROJ_TM, cast_src=None):
    k = a_first.shape[1]
    m = a_first.shape[0] + a_second.shape[0]
    n = w.shape[1]
    assert a_first.shape[0] % tm == 0 and a_second.shape[0] % tm == 0 and res.shape == (m, n)
    n_a = a_first.shape[0] // tm
    in_specs = [
        pl.BlockSpec((tm, k), lambda i, j: (jnp.minimum(i, n_a - 1), 0)),
        pl.BlockSpec((tm, k), lambda i, j: (jnp.maximum(i - n_a, 0), 0)),
        _weight_spec(k, n, tn),
        pl.BlockSpec((tm, tn), lambda i, j: (i, j)),
    ]
    out_specs = [pl.BlockSpec((tm, tn), lambda i, j: (i, j))]
    out_shape = [jax.ShapeDtypeStruct((m, n), F32)]
    operands = [a_first, a_second, w, res]
    if cast_src is not None:
        spec, shape = _cast_rider(cast_src, (m // tm, n // tn))
        in_specs.append(spec)
        out_specs.append(spec)
        out_shape.append(shape)
        operands.append(cast_src)
    outs = pl.pallas_call(
        functools.partial(_matmul_residual_kernel, n_a, cast_src is not None),
        grid=(m // tm, n // tn),
        in_specs=in_specs,
        out_specs=out_specs,
        out_shape=out_shape,
        compiler_params=_cparams(2),
        name=name,
    )(*operands)
    return outs[0] if cast_src is None else tuple(outs)


def _xattn_kernel(n_seq, tq, q_ref, k_ref, v_ref, o_ref):
    scale = XATTN_HEAD_DIM ** -0.5
    for s in range(n_seq):
        rows = slice(s * tq, (s + 1) * tq)
        for h in range(XATTN_HEADS):
            cols = slice(h * XATTN_HEAD_DIM, (h + 1) * XATTN_HEAD_DIM)
            q = q_ref[rows, cols]
            k = k_ref[s, :, cols].astype(BF16)
            v = v_ref[s, :, cols].astype(BF16)
            sc = lax.dot_general(q, k, (((1,), (1,)), ((), ())),
                                 preferred_element_type=F32) * scale
            mx = jnp.max(sc, axis=-1, keepdims=True)
            e = jnp.exp(sc - mx)
            p = e / jnp.sum(e, axis=-1, keepdims=True)
            o = jnp.dot(p.astype(BF16), v, preferred_element_type=F32)
            o_ref[rows, cols] = o.astype(BF16)


def _xattn_prompt(q, k, v, *, seq_len, tq):
    b = k.shape[0]
    nt = seq_len // tq
    d = q.shape[1]
    return pl.pallas_call(
        functools.partial(_xattn_kernel, 1, tq),
        grid=(b, nt),
        in_specs=[
            pl.BlockSpec((tq, d), lambda i, j: (i * nt + j, 0)),
            pl.BlockSpec((1, N_MEM, d), lambda i, j: (i, 0, 0)),
            pl.BlockSpec((1, N_MEM, d), lambda i, j: (i, 0, 0)),
        ],
        out_specs=pl.BlockSpec((tq, d), lambda i, j: (i * nt + j, 0)),
        out_shape=jax.ShapeDtypeStruct((b * seq_len, d), BF16),
        compiler_params=_cparams(2),
        name="xattn_prompt",
    )(q, k, v)


def _xattn_cache_kernel(n_seq, tq, q_ref, k_ref, v_ref, o_ref):
    scale = XATTN_HEAD_DIM ** -0.5
    n_rows = XATTN_HEADS * tq
    n_cols = N_MEM * XATTN_HEADS
    own = (_lane_iota((n_rows, n_cols)) % XATTN_HEADS) == (_row_iota((n_rows, n_cols)) // tq)
    qf = q_ref[...].astype(F32)
    outs = []
    for s in range(n_seq):
        qs = qf[s * tq:(s + 1) * tq]
        q4 = jnp.concatenate(
            [qs[:, h * XATTN_HEAD_DIM:(h + 1) * XATTN_HEAD_DIM] for h in range(XATTN_HEADS)],
            axis=0).astype(BF16)
        kf = k_ref[s].reshape(n_cols, XATTN_HEAD_DIM).astype(BF16)
        vf = v_ref[s].reshape(n_cols, XATTN_HEAD_DIM).astype(BF16)
        sc = lax.dot_general(q4, kf, (((1,), (1,)), ((), ())), preferred_element_type=F32) * scale
        sc = jnp.where(own, sc, -1e30)
        mx = jnp.max(sc, axis=-1, keepdims=True)
        e = jnp.exp(sc - mx)
        p = e / jnp.sum(e, axis=-1, keepdims=True)
        o4 = jnp.dot(p.astype(BF16), vf, preferred_element_type=F32)
        outs.append(jnp.concatenate([o4[h * tq:(h + 1) * tq] for h in range(XATTN_HEADS)], axis=1))
    o_ref[...] = jnp.concatenate(outs, axis=0).astype(BF16)


def _xattn_sample(q, k, v, *, seq_len, n_seq, row_block_offset):
    b = k.shape[0]
    d = q.shape[1]
    rows = n_seq * seq_len
    kv_block = (n_seq, N_MEM, XATTN_HEADS, XATTN_HEAD_DIM)
    return pl.pallas_call(
        functools.partial(_xattn_cache_kernel, n_seq, seq_len),
        grid=(b // n_seq,),
        in_specs=[
            pl.BlockSpec((rows, d), lambda i: (row_block_offset + i, 0)),
            pl.BlockSpec(kv_block, lambda i: (i, 0, 0, 0)),
            pl.BlockSpec(kv_block, lambda i: (i, 0, 0, 0)),
        ],
        out_specs=pl.BlockSpec((rows, d), lambda i: (i, 0)),
        out_shape=jax.ShapeDtypeStruct((b * seq_len, d), BF16),
        compiler_params=_cparams(1),
        name="xattn_sample",
    )(q, k, v)


def _row_iota(shape):
    return lax.broadcasted_iota(jnp.int32, shape, 0)


def _lane_iota(shape):
    return lax.broadcasted_iota(jnp.int32, shape, 1)


def _seg_cumsum(x, seg_len):
    pos = _row_iota(x.shape) % seg_len
    s = 1
    while s < seg_len:
        x = x + jnp.where(pos >= s, pltpu.roll(x, s, axis=0), 0.0)
        s *= 2
    return x


def _seg_rev_excl_cumsum(x, seg_len):
    n = x.shape[0]
    pos = _row_iota(x.shape) % seg_len
    y = jnp.where(pos < seg_len - 1, pltpu.roll(x, n - 1, axis=0), 0.0)
    s = 1
    while s < seg_len:
        y = y + jnp.where(pos < seg_len - s, pltpu.roll(y, n - s, axis=0), 0.0)
        s *= 2
    return y


def _scan8(a, b):
    shape = a.shape
    tiled = (shape[0] // SUBLANES, SUBLANES, shape[1])
    a = a.reshape(tiled)
    b = b.reshape(tiled)
    pos = lax.broadcasted_iota(jnp.int32, tiled, 1)
    for s in (1, 2, 4):
        m = pos >= s
        a_sh = pltpu.roll(a, s, axis=1)
        b_sh = pltpu.roll(b, s, axis=1)
        b = jnp.where(m, a * b_sh + b, b)
        a = jnp.where(m, a * a_sh, a)
    return a.reshape(shape), b.reshape(shape)


def _conv_taps(x, w_ref, b_ref, cols, fix):
    acc = b_ref[:, cols] + x * w_ref[CONV_WIDTH - 1:CONV_WIDTH, cols]
    for s in range(1, CONV_WIDTH):
        xs = fix(s, pltpu.roll(x, s, axis=0))
        k = CONV_WIDTH - 1 - s
        acc = acc + xs * w_ref[k:k + 1, cols]
    return acc


def _bcast_col(v, c):
    return jnp.broadcast_to(v[:, c:c + 1], (v.shape[0], LANES))


def _lru_part(sample, pa_ref, prev_fix_lru, h0_rows, hcar_ref,
              lcw_ref, lcb_ref, wa_ref, wx_ref, ba_ref, bx_ref, lam_ref, lg_ref,
              mixed_ref, hs_scr, yl_scr):
    ssq = jnp.zeros((ROWS, 1), F32)
    for hd in range(LRU_HEADS):
        cols = slice(hd * LANES, (hd + 1) * LANES)
        xl = pa_ref[:, COL_XLRU + hd * LANES:COL_XLRU + (hd + 1) * LANES]
        u = _conv_taps(xl, lcw_ref, lcb_ref, cols, functools.partial(prev_fix_lru, cols))
        ub = u.astype(BF16)
        r = jax.nn.sigmoid(jnp.dot(ub, wa_ref[hd], preferred_element_type=F32) + ba_ref[:, cols])
        i = jax.nn.sigmoid(jnp.dot(ub, wx_ref[hd], preferred_element_type=F32) + bx_ref[:, cols])
        log_a = (-LRU_C * r) * _softplus(-lam_ref[:, cols])
        a = jnp.exp(log_a)
        th = jnp.tanh(log_a)
        beta = jnp.sqrt((-2.0 * th) / (1.0 - th))
        bb = (beta * i) * u
        a_cum, b_cum = _scan8(a, bb)
        if sample:
            h = a_cum * h0_rows(cols) + b_cum
        else:
            carry = hcar_ref[:, cols]
            parts = []
            for t in range(ROWS // SUBLANES):
                rs = slice(t * SUBLANES, (t + 1) * SUBLANES)
                ht = a_cum[rs] * carry + b_cum[rs]
                parts.append(ht)
                carry = ht[SUBLANES - 1:SUBLANES]
            h = jnp.concatenate(parts, axis=0)
            hcar_ref[:, cols] = carry
        hs_scr[:, cols] = h
        g = jax.nn.gelu(pa_ref[:, COL_GLRU + hd * LANES:COL_GLRU + (hd + 1) * LANES])
        yl = h * g
        ssq = ssq + jnp.sum(yl * yl, axis=-1, keepdims=True)
        yl_scr[:, cols] = yl
    inv = lax.rsqrt(ssq * (1.0 / LRU_WIDTH) + EPS)
    for hd in range(LRU_HEADS):
        cols = slice(hd * LANES, (hd + 1) * LANES)
        mixed_ref[:, cols] = ((yl_scr[:, cols] * inv) * lg_ref[:, cols]).astype(BF16)


def _ssd_front(sample, pb_ref, prev_fix_ssd, scw_ref, scb_ref, dtb_ref, alog_ref, dexp_ref,
               xc_scr, xw_scr, ecs_scr, y_scr, xt_scr, dec_scr, c_scr, b_scr):
    seg = SUBLANES if sample else ROWS
    n_seq = ROWS // seg

    for t in range(SSD_CONV_DIM // LANES):
        cols = slice(t * LANES, (t + 1) * LANES)
        xb = pb_ref[:, COL_XBC + t * LANES:COL_XBC + (t + 1) * LANES]
        v = _conv_taps(xb, scw_ref, scb_ref, cols, functools.partial(prev_fix_ssd, cols))
        xc_scr[:, cols] = _silu(v)

    dt = _softplus(pb_ref[:, COL_DT:COL_DT + LANES] + dtb_ref[...])
    a_neg = -jnp.exp(alog_ref[...])
    d_a = dt * a_neg
    cs = _seg_cumsum(d_a, seg)
    rcs = _seg_rev_excl_cumsum(d_a, seg)
    todt = jnp.exp(rcs) * dt
    dec_tot = jnp.exp(cs + rcs)
    cs_t = cs.T
    dt_t = dt.T
    dec_t = dec_tot.T
    for s in range(n_seq):
        dec_scr[s] = jnp.broadcast_to(dec_t[0:SSD_HEADS, s * seg:s * seg + 1], (SSD_HEADS, LANES))

    ri = _row_iota((ROWS, ROWS))
    ci = _lane_iota((ROWS, ROWS))
    causal = ri >= ci
    if sample:
        causal = jnp.logical_and(causal, (ri // seg) == (ci // seg))
    lo = _lane_iota((ROWS, LANES)) < SSD_HEADDIM

    b_off = SSD_WIDTH
    c_off = SSD_WIDTH + SSD_GROUPS * SSD_STATE
    for g in range(SSD_GROUPS):
        gcols = slice(g * SSD_STATE, (g + 1) * SSD_STATE)
        bg = xc_scr[:, b_off + g * SSD_STATE:b_off + (g + 1) * SSD_STATE].astype(BF16)
        cg = xc_scr[:, c_off + g * SSD_STATE:c_off + (g + 1) * SSD_STATE].astype(BF16)
        c_scr[:, gcols] = xc_scr[:, c_off + g * SSD_STATE:c_off + (g + 1) * SSD_STATE]
        b_scr[:, gcols] = bg
        cb = lax.dot_general(cg, bg, (((1,), (1,)), ((), ())), preferred_element_type=F32)
        heads_per_group = SSD_HEADS // SSD_GROUPS
        for jp in range(heads_per_group // 2):
            h0 = g * heads_per_group + 2 * jp
            h1 = h0 + 1
            cols = slice(h0 * SSD_HEADDIM, h0 * SSD_HEADDIM + LANES)
            x_pair = xc_scr[:, cols]
            ws = []
            cs_cols = []
            for hh in (h0, h1):
                cs_col = _bcast_col(cs, hh)
                cs_cols.append(cs_col)
                dec = jnp.where(causal, jnp.exp(cs_col - cs_t[hh:hh + 1, :]), 0.0)
                ws.append(((cb * dec) * dt_t[hh:hh + 1, :]).astype(BF16))
            lhs = jnp.concatenate(ws, axis=1)
            rhs = jnp.concatenate([jnp.where(lo, x_pair, 0.0), jnp.where(lo, 0.0, x_pair)],
                                  axis=0).astype(BF16)
            y_diag = jnp.dot(lhs, rhs, preferred_element_type=F32)
            y_scr[:, cols] = y_diag + x_pair * dexp_ref[:, cols]
            xw_scr[:, cols] = x_pair * jnp.where(lo, _bcast_col(todt, h0), _bcast_col(todt, h1))
            ecs_scr[:, cols] = jnp.exp(jnp.where(lo, cs_cols[0], cs_cols[1]))
    for t in range(SSD_WIDTH // LANES):
        xt_scr[t * LANES:(t + 1) * LANES, :] = xw_scr[:, t * LANES:(t + 1) * LANES].T.astype(BF16)


def _ssd_state_step(c_rows, b_mask, s_old, dec, xt_scr):
    y_parts = []
    s_parts = []
    heads_per_group = SSD_HEADS // SSD_GROUPS
    for g in range(SSD_GROUPS):
        gcols = slice(g * SSD_STATE, (g + 1) * SSD_STATE)
        grows = slice(g * GROUP_COLS, (g + 1) * GROUP_COLS)
        sg = s_old[grows]
        y_parts.append(lax.dot_general(c_rows[:, gcols], sg.astype(BF16), (((1,), (1,)), ((), ())),
                                       preferred_element_type=F32))
        upd = jnp.dot(xt_scr[grows, :], b_mask[:, gcols], preferred_element_type=F32)
        for hh in range(heads_per_group):
            hr = slice(hh * SSD_HEADDIM, (hh + 1) * SSD_HEADDIM)
            habs = g * heads_per_group + hh
            s_parts.append(sg[hr] * dec[habs:habs + 1, :] + upd[hr])
    return jnp.concatenate(y_parts, axis=1), jnp.concatenate(s_parts, axis=0)


def _ssd_state_group_inplace(g, c_scr, b_scr, s_ref, dec_ref, y_scr, ecs_scr, xt_scr):
    heads_per_group = SSD_HEADS // SSD_GROUPS
    gcols = slice(g * SSD_STATE, (g + 1) * SSD_STATE)
    grows = slice(g * GROUP_COLS, (g + 1) * GROUP_COLS)
    y_off = lax.dot_general(c_scr[:, gcols].astype(BF16), s_ref[grows].astype(BF16),
                            (((1,), (1,)), ((), ())), preferred_element_type=F32)
    for t in range(GROUP_COLS // LANES):
        cols = slice(g * GROUP_COLS + t * LANES, g * GROUP_COLS + (t + 1) * LANES)
        y_scr[:, cols] = y_scr[:, cols] + ecs_scr[:, cols] * y_off[:, t * LANES:(t + 1) * LANES]
    upd = jnp.dot(xt_scr[grows, :], b_scr[:, gcols], preferred_element_type=F32)
    for hh in range(heads_per_group):
        habs = g * heads_per_group + hh
        rows = slice(g * GROUP_COLS + hh * SSD_HEADDIM, g * GROUP_COLS + (hh + 1) * SSD_HEADDIM)
        s_ref[rows, :] = (s_ref[rows, :] * dec_ref[habs:habs + 1, :]
                          + upd[hh * SSD_HEADDIM:(hh + 1) * SSD_HEADDIM])


def _ssd_finish(pa_ref, y_scr, sg_ref, mixed_ref):
    for g in range(SSD_GROUPS):
        ssq = jnp.zeros((ROWS, 1), F32)
        n_t = GROUP_COLS // LANES
        for t in range(n_t):
            c0 = g * GROUP_COLS + t * LANES
            yz = y_scr[:, c0:c0 + LANES] * _silu(pa_ref[:, COL_Z + c0:COL_Z + c0 + LANES])
            y_scr[:, c0:c0 + LANES] = yz
            ssq = ssq + jnp.sum(yz * yz, axis=-1, keepdims=True)
        inv = lax.rsqrt(ssq * (1.0 / GROUP_COLS) + EPS)
        for t in range(n_t):
            c0 = g * GROUP_COLS + t * LANES
            mixed_ref[:, LRU_WIDTH + c0:LRU_WIDTH + c0 + LANES] = (
                (y_scr[:, c0:c0 + LANES] * inv) * sg_ref[:, c0:c0 + LANES]).astype(BF16)


def _mixer_prompt_kernel(x_ref, xn_ref, mg_ref, *rest):
    n_slabs = D_A // W_A_TILE
    wina_refs = rest[:n_slabs]
    (pb_ref,
     lcw_ref, lcb_ref, wa_ref, wx_ref, ba_ref, bx_ref, lam_ref, lg_ref,
     scw_ref, scb_ref, dtb_ref, alog_ref, dexp_ref, sg_ref,
     mixed_ref, lconv_ref, lh_ref, sconv_ref, sh_ref,
     pa_scr, hn_scr, hcar_ref, ltail_scr, stail_scr, s_scr,
     hs_scr, yl_scr, xc_scr, xw_scr, ecs_scr, y_scr, xt_scr, dec_scr, c_scr, b_scr) = rest[n_slabs:]
    c = pl.program_id(1)
    step = pl.program_id(0) * pl.num_programs(1) + c
    slot = step % 2

    @pl.when(c == 0)
    def _():
        hcar_ref[...] = jnp.zeros_like(hcar_ref)
        ltail_scr[...] = jnp.zeros_like(ltail_scr)
        stail_scr[...] = jnp.zeros_like(stail_scr)
        s_scr[...] = jnp.zeros_like(s_scr)

    @pl.when(step == 0)
    def _():
        hn = _rmsnorm_rows(x_ref[...], mg_ref[...]).astype(BF16)
        for j in range(n_slabs):
            pa_scr[0, :, j * W_A_TILE:(j + 1) * W_A_TILE] = jnp.dot(
                hn, wina_refs[j][...], preferred_element_type=F32)

    hn_scr[...] = _rmsnorm_rows(xn_ref[...], mg_ref[...]).astype(BF16)
    pa_next = pa_scr.at[1 - slot]
    pa_cur = pa_scr.at[slot]

    for j in range(n_slabs):
        pa_next[:, j * W_A_TILE:(j + 1) * W_A_TILE] = jnp.dot(
            hn_scr[...], wina_refs[j][...], preferred_element_type=F32)

    pos8 = _row_iota((SUBLANES, LANES))

    def make_fix(tail_scr):
        def fix(cols, s, rolled):
            first = jnp.where(pos8 < s, pltpu.roll(tail_scr[:, cols], s, axis=0), rolled[0:SUBLANES])
            return jnp.concatenate([first, rolled[SUBLANES:]], axis=0)
        return fix

    _lru_part(False, pa_cur, make_fix(ltail_scr), None, hcar_ref,
              lcw_ref, lcb_ref, wa_ref, wx_ref, ba_ref, bx_ref, lam_ref, lg_ref,
              mixed_ref, hs_scr, yl_scr)
    _ssd_front(False, pb_ref, make_fix(stail_scr), scw_ref, scb_ref, dtb_ref, alog_ref, dexp_ref,
               xc_scr, xw_scr, ecs_scr, y_scr, xt_scr, dec_scr, c_scr, b_scr)

    ltail_scr[...] = pa_cur[ROWS - SUBLANES:ROWS, COL_XLRU:COL_XLRU + LRU_WIDTH]
    stail_scr[...] = pb_ref[ROWS - SUBLANES:ROWS, COL_XBC:COL_XBC + SSD_CONV_DIM]

    for g in range(SSD_GROUPS):
        _ssd_state_group_inplace(g, c_scr, b_scr, s_scr, dec_scr.at[0], y_scr, ecs_scr, xt_scr)
    _ssd_finish(pa_cur, y_scr, sg_ref, mixed_ref)

    @pl.when(c == pl.num_programs(1) - 1)
    def _():
        lconv_ref[0] = pa_cur[ROWS - (CONV_WIDTH - 1):ROWS, COL_XLRU:COL_XLRU + LRU_WIDTH]
        sconv_ref[0] = pb_ref[ROWS - (CONV_WIDTH - 1):ROWS, COL_XBC:COL_XBC + SSD_CONV_DIM]
        lh_ref[0] = hcar_ref[...]
        sh_ref[0] = s_scr[...]


def _mixer_sample_kernel(n_inner, pa_ref, pb_ref, lprev_ref, sprev_ref, h0_ref, sin_ref,
                         lcw_ref, lcb_ref, wa_ref, wx_ref, ba_ref, bx_ref, lam_ref, lg_ref,
                         scw_ref, scb_ref, dtb_ref, alog_ref, dexp_ref, sg_ref,
                         mixed_ref, lconv_ref, lh_ref, sconv_ref, sh_ref,
                         hs_scr, yl_scr, xc_scr, xw_scr, ecs_scr, y_scr, xt_scr, dec_scr, c_scr, b_scr):
    i = pl.program_id(1)
    seg = SUBLANES
    n_seq = ROWS // seg
    seq_per_step = n_seq // n_inner

    @pl.when(i == 0)
    def _():
        pos = _row_iota((ROWS, LANES)) % seg

        def make_fix(prev_ref):
            def fix(cols, s, rolled):
                return jnp.where(pos < s, pltpu.roll(prev_ref[:, cols], ROWS - seg + s, axis=0), rolled)
            return fix

        def h0_rows(cols):
            return jnp.concatenate(
                [jnp.broadcast_to(h0_ref[q:q + 1, cols], (seg, LANES)) for q in range(n_seq)], axis=0)

        _ssd_front(True, pb_ref, make_fix(sprev_ref), scw_ref, scb_ref, dtb_ref, alog_ref, dexp_ref,
                   xc_scr, xw_scr, ecs_scr, y_scr, xt_scr, dec_scr, c_scr, b_scr)
        _lru_part(True, pa_ref, make_fix(lprev_ref), h0_rows, None,
                  lcw_ref, lcb_ref, wa_ref, wx_ref, ba_ref, bx_ref, lam_ref, lg_ref,
                  mixed_ref, hs_scr, yl_scr)
        for sq in range(n_seq):
            tail = slice((sq + 1) * seg - (CONV_WIDTH - 1), (sq + 1) * seg)
            lconv_ref[sq] = pa_ref[tail, COL_XLRU:COL_XLRU + LRU_WIDTH]
            sconv_ref[sq] = pb_ref[tail, COL_XBC:COL_XBC + SSD_CONV_DIM]
            lh_ref[sq:sq + 1, :] = hs_scr[(sq + 1) * seg - 1:(sq + 1) * seg, :]

    rgroup = _row_iota((ROWS, 2 * SSD_STATE)) // seg
    for jj in range(seq_per_step):
        q = i * seq_per_step + jj
        r0 = pl.multiple_of(q * seg, seg)
        c_rows = c_scr[pl.ds(r0, seg), :].astype(BF16)
        b_all = b_scr[...]
        b_mask = jnp.where(rgroup == q, b_all, jnp.zeros_like(b_all))
        y_off, s_new = _ssd_state_step(c_rows, b_mask, sin_ref[jj], dec_scr[q], xt_scr)
        sh_ref[jj] = s_new
        y_scr[pl.ds(r0, seg), :] = y_scr[pl.ds(r0, seg), :] + ecs_scr[pl.ds(r0, seg), :] * y_off

    @pl.when(i == n_inner - 1)
    def _():
        _ssd_finish(pa_ref, y_scr, sg_ref, mixed_ref)


def _mixer_scratch(n_dec):
    return [
        pltpu.VMEM((ROWS, LRU_WIDTH), F32),
        pltpu.VMEM((ROWS, LRU_WIDTH), F32),
        pltpu.VMEM((ROWS, SSD_CONV_DIM), F32),
        pltpu.VMEM((ROWS, SSD_WIDTH), F32),
        pltpu.VMEM((ROWS, SSD_WIDTH), F32),
        pltpu.VMEM((ROWS, SSD_WIDTH), F32),
        pltpu.VMEM((SSD_WIDTH, ROWS), BF16),
        pltpu.VMEM((n_dec, SSD_HEADS, LANES), F32),
        pltpu.VMEM((ROWS, 2 * SSD_STATE), F32),
        pltpu.VMEM((ROWS, 2 * SSD_STATE), BF16),
    ]


def _full_spec(shape, n_grid):
    zeros = (0,) * len(shape)
    if n_grid == 2:
        return pl.BlockSpec(shape, lambda i, j: zeros)
    return pl.BlockSpec(shape, lambda i: zeros)


def _mixer_weight_specs(n_grid):
    return [
        _full_spec((CONV_WIDTH, LRU_WIDTH), n_grid),
        _full_spec((1, LRU_WIDTH), n_grid),
        _full_spec((LRU_HEADS, LANES, LANES), n_grid),
        _full_spec((LRU_HEADS, LANES, LANES), n_grid),
        _full_spec((1, LRU_WIDTH), n_grid),
        _full_spec((1, LRU_WIDTH), n_grid),
        _full_spec((1, LRU_WIDTH), n_grid),
        _full_spec((1, LRU_WIDTH), n_grid),
        _full_spec((CONV_WIDTH, SSD_CONV_DIM), n_grid),
        _full_spec((1, SSD_CONV_DIM), n_grid),
        _full_spec((1, LANES), n_grid),
        _full_spec((1, LANES), n_grid),
        _full_spec((1, SSD_WIDTH), n_grid),
        _full_spec((1, SSD_WIDTH), n_grid),
    ]


def _mixer_prompt(x1, mix_g, w_in_a, proj_b, weights, *, batch, seq_len):
    nc = seq_len // ROWS
    n_slabs = D_A // W_A_TILE

    def slab_spec(j):
        return pl.BlockSpec((D_MODEL, W_A_TILE), lambda b, c: (0, j), pipeline_mode=pl.Buffered(1))

    return pl.pallas_call(
        _mixer_prompt_kernel,
        grid=(batch, nc),
        in_specs=[
            pl.BlockSpec((ROWS, D_MODEL), lambda b, c: (b * nc + c, 0)),
            pl.BlockSpec((ROWS, D_MODEL), lambda b, c: (jnp.minimum(b * nc + c + 1, batch * nc - 1), 0)),
            pl.BlockSpec((1, D_MODEL), lambda b, c: (0, 0)),
        ] + [slab_spec(j) for j in range(n_slabs)] + [
            pl.BlockSpec((ROWS, D_B_PAD), lambda b, c: (b * nc + c, 0)),
        ] + _mixer_weight_specs(2),
        out_specs=[
            pl.BlockSpec((ROWS, LRU_WIDTH + SSD_WIDTH), lambda b, c: (b * nc + c, 0)),
            pl.BlockSpec((1, CONV_WIDTH - 1, LRU_WIDTH), lambda b, c: (b, 0, 0)),
            pl.BlockSpec((1, 1, LRU_WIDTH), lambda b, c: (b, 0, 0)),
            pl.BlockSpec((1, CONV_WIDTH - 1, SSD_CONV_DIM), lambda b, c: (b, 0, 0)),
            pl.BlockSpec((1, SSD_WIDTH, SSD_STATE), lambda b, c: (b, 0, 0)),
        ],
        out_shape=[
            jax.ShapeDtypeStruct((batch * seq_len, LRU_WIDTH + SSD_WIDTH), BF16),
            jax.ShapeDtypeStruct((batch, CONV_WIDTH - 1, LRU_WIDTH), F32),
            jax.ShapeDtypeStruct((batch, 1, LRU_WIDTH), F32),
            jax.ShapeDtypeStruct((batch, CONV_WIDTH - 1, SSD_CONV_DIM), F32),
            jax.ShapeDtypeStruct((batch, SSD_WIDTH, SSD_STATE), F32),
        ],
        scratch_shapes=[
            pltpu.VMEM((2, ROWS, D_A), F32),
            pltpu.VMEM((ROWS, D_MODEL), BF16),
            pltpu.VMEM((1, LRU_WIDTH), F32),
            pltpu.VMEM((SUBLANES, LRU_WIDTH), F32),
            pltpu.VMEM((SUBLANES, SSD_CONV_DIM), F32),
            pltpu.VMEM((SSD_WIDTH, SSD_STATE), F32),
        ] + _mixer_scratch(1),
        compiler_params=_cparams(2),
        name="mixer_prompt",
    )(x1, x1, mix_g, *([w_in_a] * n_slabs), proj_b, *weights)


def _mixer_sample(proj_a, proj_b, lprev, sprev, h0, s_in, weights, *, batch, seq_len, row_block_offset,
                  n_inner=SAMPLE_STATE_STEPS):
    n_seq = ROWS // seq_len
    n_outer = batch // n_seq
    sps = n_seq // n_inner
    return pl.pallas_call(
        functools.partial(_mixer_sample_kernel, n_inner),
        grid=(n_outer, n_inner),
        in_specs=[
            pl.BlockSpec((ROWS, D_A), lambda o, i: (o, 0)),
            pl.BlockSpec((ROWS, D_B_PAD), lambda o, i: (row_block_offset + o, 0)),
            pl.BlockSpec((ROWS, LRU_WIDTH), lambda o, i: (o, 0)),
            pl.BlockSpec((ROWS, SSD_CONV_DIM), lambda o, i: (o, 0)),
            pl.BlockSpec((n_seq, LRU_WIDTH), lambda o, i: (o, 0)),
            pl.BlockSpec((sps, SSD_WIDTH, SSD_STATE), lambda o, i: (o * n_inner + i, 0, 0)),
        ] + _mixer_weight_specs(2),
        out_specs=[
            pl.BlockSpec((ROWS, LRU_WIDTH + SSD_WIDTH), lambda o, i: (o, 0)),
            pl.BlockSpec((n_seq, CONV_WIDTH - 1, LRU_WIDTH), lambda o, i: (o, 0, 0)),
            pl.BlockSpec((n_seq, LRU_WIDTH), lambda o, i: (o, 0)),
            pl.BlockSpec((n_seq, CONV_WIDTH - 1, SSD_CONV_DIM), lambda o, i: (o, 0, 0)),
            pl.BlockSpec((sps, SSD_WIDTH, SSD_STATE), lambda o, i: (o * n_inner + i, 0, 0)),
        ],
        out_shape=[
            jax.ShapeDtypeStruct((batch * seq_len, LRU_WIDTH + SSD_WIDTH), BF16),
            jax.ShapeDtypeStruct((batch, CONV_WIDTH - 1, LRU_WIDTH), F32),
            jax.ShapeDtypeStruct((batch, LRU_WIDTH), F32),
            jax.ShapeDtypeStruct((batch, CONV_WIDTH - 1, SSD_CONV_DIM), F32),
            jax.ShapeDtypeStruct((batch, SSD_WIDTH, SSD_STATE), F32),
        ],
        scratch_shapes=_mixer_scratch(n_seq),
        compiler_params=_cparams(2),
        name="mixer_sample",
    )(proj_a, proj_b, lprev, sprev, h0, s_in, *weights)


def _row(v):
    return v.reshape(1, -1).astype(F32)


def _pad_lanes(v):
    v = v.reshape(1, -1).astype(F32)
    return jnp.pad(v, ((0, 0), (0, LANES - v.shape[1])))


def kernel(x_prompt, mem_prompt, x_sample, cache_mem_k, cache_mem_v, state_lru_conv, state_lru_h, state_ssd_conv, state_ssd_h, ffn1_norm_g, ffn1_w_gate, ffn1_w_up, ffn1_w_down, mix_norm_g, w_in, lru_conv_w, lru_conv_b, lru_w_a, lru_b_a, lru_w_x, lru_b_x, lru_lambda, lru_out_norm_g, ssd_conv_w, ssd_conv_b, ssd_dt_bias, ssd_a_log, ssd_d, ssd_out_norm_g, w_out, xattn_norm_g, mem_norm_g, xattn_w_q, xattn_w_k, xattn_w_v, xattn_w_o, ffn2_norm_g, ffn2_w_gate, ffn2_w_up, ffn2_w_down, final_norm_g):
    depth = ffn1_norm_g.shape[0]
    assert depth == 1
    bp, tp, d = x_prompt.shape
    bs, ts, _ = x_sample.shape
    mp = bp * tp
    ms = bs * ts
    assert tp % ROWS == 0 and ROWS % ts == 0 and ts == SUBLANES and mp % ROWS == 0
    l = 0

    mixer_weights = (
        lru_conv_w[l], _row(lru_conv_b[l]), lru_w_a[l].astype(BF16), lru_w_x[l].astype(BF16),
        _row(lru_b_a[l]), _row(lru_b_x[l]), _row(lru_lambda[l]), _row(lru_out_norm_g[l]),
        ssd_conv_w[l], _row(ssd_conv_b[l]), _pad_lanes(ssd_dt_bias[l]), _pad_lanes(ssd_a_log[l]),
        _row(jnp.repeat(ssd_d[l], SSD_HEADDIM)), _row(ssd_out_norm_g[l]),
    )

    x1, w_out_bf, w_q_bf, w_k_bf, w_v_bf, w_o_bf = _ffn(
        (x_prompt.reshape(mp, d), x_sample.reshape(ms, d)), _row(ffn1_norm_g[l]),
        ffn1_w_gate[l].astype(BF16), ffn1_w_up[l].astype(BF16), ffn1_w_down[l].astype(BF16),
        _row(final_norm_g), out_rows=(mp + ms,), final_norm=False,
        cast_srcs=(w_out[l], xattn_w_q[l], xattn_w_k[l], xattn_w_v[l], xattn_w_o[l]))
    w_in_bf = w_in[l].astype(BF16)
    w_in_b = jnp.pad(w_in_bf[:, D_A:], ((0, 0), (0, D_B_PAD - D_B)))
    mix_g = _row(mix_norm_g[l])
    proj_b, ffn2_wg = _norm_matmul(x1, mix_g, w_in_b, out_dtype=F32, name="in_proj_b", tn=D_B_PAD,
                                   tm=RESIDENT_TM, cast_src=ffn2_w_gate[l])
    proj_a_s = _norm_matmul(x1, mix_g, w_in_bf, out_dtype=F32, name="in_proj_a_sample",
                            row_offset=mp, n_cols=D_A)

    mixed_p, p_lc, p_lh, p_sc, p_sh = _mixer_prompt(x1, mix_g, w_in_bf, proj_b, mixer_weights,
                                                    batch=bp, seq_len=tp)

    pad_rows = ((0, 0), (SUBLANES - (CONV_WIDTH - 1), 0), (0, 0))
    lprev = jnp.pad(state_lru_conv[l], pad_rows).reshape(ms, LRU_WIDTH)
    sprev = jnp.pad(state_ssd_conv[l], pad_rows).reshape(ms, SSD_CONV_DIM)
    mixed_s, s_lc, s_lh, s_sc, s_sh = _mixer_sample(
        proj_a_s, proj_b, lprev, sprev, state_lru_h[l], state_ssd_h[l].reshape(bs, SSD_WIDTH, SSD_STATE),
        mixer_weights, batch=bs, seq_len=ts, row_block_offset=mp // ROWS)

    x2 = _matmul_residual(mixed_p, mixed_s, w_out_bf, x1, tn=d, tm=RESIDENT_TM, name="out_proj")

    q, ffn2_wu = _norm_matmul(x2, _row(xattn_norm_g[l]), w_q_bf, out_dtype=BF16,
                              name="q_proj", tn=d, cast_src=ffn2_w_up[l])
    mem = mem_prompt.reshape(bp * N_MEM, d)
    mk = _norm_matmul(mem, _row(mem_norm_g[l]), w_k_bf, out_dtype=F32, name="mem_k",
                      tn=d, tm=RESIDENT_TM)
    mv = _norm_matmul(mem, _row(mem_norm_g[l]), w_v_bf, out_dtype=F32, name="mem_v",
                      tn=d, tm=RESIDENT_TM)
    o_p = _xattn_prompt(q, mk.reshape(bp, N_MEM, d), mv.reshape(bp, N_MEM, d), seq_len=tp, tq=XATTN_TQ)
    o_s = _xattn_sample(q, cache_mem_k[l], cache_mem_v[l], seq_len=ts, n_seq=XATTN_CACHE_SEQS,
                        row_block_offset=mp // (XATTN_CACHE_SEQS * ts))
    x3, ffn2_wd = _matmul_residual(o_p, o_s, w_o_bf, x2, tn=d, tm=RESIDENT_TM,
                                   name="o_proj", cast_src=ffn2_w_down[l])

    y_p, y_s = _ffn((x3,), _row(ffn2_norm_g[l]), ffn2_wg, ffn2_wu, ffn2_wd, _row(final_norm_g),
                    out_rows=(mp, ms), final_norm=True)

    y_prompt = y_p.reshape(bp, tp, d)
    y_sample = y_s.reshape(bs, ts, d)
    hshape = (SSD_HEADS, SSD_HEADDIM, SSD_STATE)
    return (y_prompt, y_sample,
            p_lc[None], p_lh.reshape(1, bp, LRU_WIDTH), p_sc[None], p_sh.reshape((1, bp) + hshape),
            mk.reshape(1, bp, N_MEM, XATTN_HEADS, XATTN_HEAD_DIM),
            mv.reshape(1, bp, N_MEM, XATTN_HEADS, XATTN_HEAD_DIM),
            s_lc[None], s_lh[None], s_sc[None], s_sh.reshape((1, bs) + hshape))
```
